```python
import math
import jax, jax.numpy as jnp
from jax import lax
import numpy as np

D_MODEL = 1024
BATCH = 8
SEQ = 2048
DEPTH = 2

EPS = 1e-6
POOL_WINDOWS = (2, 4, 8, 16)
N_POOL_GROUPS = len(POOL_WINDOWS)
POOL_GROUP_DIM = D_MODEL // N_POOL_GROUPS
HEAD_DIM = 64
HEADS_PER_GROUP = D_MODEL // HEAD_DIM
DILATED_PATTERNS = ((128, 1), (512, 4), (2048, 16))
N_ATT_GROUPS = len(DILATED_PATTERNS)
N_HEADS_TOTAL = N_ATT_GROUPS * HEADS_PER_GROUP
Q_WIDTH = N_HEADS_TOTAL * HEAD_DIM
KV_WIDTH = 2 * Q_WIDTH
MERGED_WIDTH = HEADS_PER_GROUP * HEAD_DIM
QBLK = 128
NEG_INF = -1e30
N_BUCKETS = 32
MAX_EXACT = N_BUCKETS // 2
MAX_DISTANCE = 2048
D_FF_DENSE = ((8 * D_MODEL // 3 + 255) // 256) * 256
N_EXPERTS = 8
TOP_K = 2
D_FF_EXPERT = 7 * D_MODEL // 2
MOE_BLOCK = 256
N_A_LAYERS = DEPTH // 2
N_B_LAYERS = DEPTH - N_A_LAYERS
N_DENSE = (DEPTH + 1) // 2
N_MOE = DEPTH // 2

kernel_name = "yoco_pool_dilated_moe_hybrid"


def rmsnorm(x, g):
    xf = x.astype(jnp.float32)
    y = xf * lax.rsqrt(jnp.mean(xf * xf, axis=-1, keepdims=True) + EPS)
    return (y * g.astype(jnp.float32)).astype(x.dtype)


def pool_mixer(xn, proj, scale):
    B, S, D = xn.shape
    xg = xn.astype(jnp.float32).reshape(B, S, N_POOL_GROUPS, POOL_GROUP_DIM)
    c = jnp.pad(jnp.cumsum(xg, axis=1), ((0, 0), (1, 0), (0, 0), (0, 0)))
    hi = jnp.arange(1, S + 1, dtype=jnp.int32)
    win = jnp.array(POOL_WINDOWS, dtype=jnp.int32)[:, None]
    lo = jnp.maximum(hi[None, :] - win, 0)
    c_lo = c[:, lo.T, jnp.arange(N_POOL_GROUPS)[None, :], :]
    cnt = (hi[None, :] - lo).astype(jnp.float32).T
    pooled = (c[:, 1:] - c_lo) / cnt[None, :, :, None] - xg
    out = jnp.einsum('bsgc,gcd->bsgd', pooled, proj.astype(jnp.float32))
    out = out.reshape(B, S, D) * scale.astype(jnp.float32)
    return out.astype(xn.dtype)


def t5_bucket(n):
    nf = jnp.maximum(n, 1).astype(jnp.float32)
    large = MAX_EXACT + (jnp.log(nf / MAX_EXACT) / math.log(MAX_DISTANCE / MAX_EXACT)
                         * (N_BUCKETS - MAX_EXACT)).astype(jnp.int32)
    large = jnp.minimum(large, N_BUCKETS - 1)
    return jnp.where(n < MAX_EXACT, n, large)


def dilated_group_attention(q, k, v, bias_table, window, dilation):
    B, S, H, Dh = q.shape
    W = window // dilation
    assert W <= QBLK
    L = S // dilation
    n = -(-L // QBLK)
    Lp = n * QBLK

    def to_sub(t, left):
        t = t.reshape(B, L, dilation, H, Dh).transpose(0, 2, 1, 3, 4)
        return jnp.pad(t, ((0, 0), (0, 0), (left, Lp - L), (0, 0), (0, 0)))

    def key_blocks(t):
        t = to_sub(t, QBLK).reshape(B, dilation, n + 1, QBLK, H, Dh)
        return jnp.concatenate([t[:, :, :-1], t[:, :, 1:]], axis=3)

    qs = to_sub(q, 0).reshape(B, dilation, n, QBLK, H, Dh)
    kb = key_blocks(k)
    vb = key_blocks(v)

    a = jnp.arange(QBLK, dtype=jnp.int32)[:, None]
    c = jnp.arange(2 * QBLK, dtype=jnp.int32)[None, :]
    m = a + QBLK - c
    band = (m >= 0) & (m <= W)
    kpos = jnp.arange(n, dtype=jnp.int32)[:, None, None] * QBLK - QBLK + c[None]
    valid = band[None] & (kpos >= 0) & (kpos < L)
    bias = bias_table[t5_bucket(jnp.maximum(m, 0) * dilation)]
    bias = bias.transpose(2, 0, 1).astype(jnp.float32)

    logits = jnp.einsum('brnqhd,brnkhd->brnhqk', qs, kb,
                        preferred_element_type=jnp.float32) * (HEAD_DIM ** -0.5)
    logits = jnp.where(valid[None, None, :, None], logits + bias[None, None, None], NEG_INF)
    mx = jnp.max(logits, axis=-1, keepdims=True)
    p = jnp.exp(logits - mx)
    s = jnp.sum(p, axis=-1)
    o = jnp.einsum('brnhqk,brnkhd->brnqhd', p, vb.astype(jnp.float32))
    s_t = s.transpose(0, 1, 2, 4, 3)
    o = o / s_t[..., None]
    lse = (mx[..., 0] + jnp.log(s)).transpose(0, 1, 2, 4, 3)

    o = o.reshape(B, dilation, Lp, H, Dh)[:, :, :L].transpose(0, 2, 1, 3, 4).reshape(B, S, H, Dh)
    lse = lse.reshape(B, dilation, Lp, H)[:, :, :L].transpose(0, 2, 1, 3).reshape(B, S, H)
    return o, lse


def dilated_attention_layer(hn, w_q, w_o, k_sh, v_sh, rel_bias):
    B, S, _ = hn.shape
    q = (hn @ w_q).reshape(B, S, N_ATT_GROUPS, HEADS_PER_GROUP, HEAD_DIM)
    outs, lses = [], []
    for g, (window, dilation) in enumerate(DILATED_PATTERNS):
        o, lse = dilated_group_attention(
            q[:, :, g], k_sh[:, :, g], v_sh[:, :, g],
            rel_bias[:, g * HEADS_PER_GROUP:(g + 1) * HEADS_PER_GROUP], window, dilation)
        outs.append(o)
        lses.append(lse)
    alpha = jax.nn.softmax(jnp.stack(lses, axis=0), axis=0)
    o = jnp.einsum('gbsh,gbshd->bshd', alpha, jnp.stack(outs, axis=0))
    return o.reshape(B, S, MERGED_WIDTH).astype(hn.dtype) @ w_o


def dense_swiglu(xn, w_gate, w_up, w_down):
    return (jax.nn.silu(xn @ w_gate) * (xn @ w_up)) @ w_down


def moe_swiglu(xn, w_router, w_gate, w_up, w_down):
    B, S, D = xn.shape
    T = B * S
    xf = xn.reshape(T, D)
    logits = (xf @ w_router).astype(jnp.float32)
    top_val, top_idx = lax.top_k(logits, TOP_K)
    gates = jax.nn.softmax(top_val, axis=-1)
    e_flat = top_idx.reshape(-1).astype(jnp.int32)
    tok = jnp.repeat(jnp.arange(T, dtype=jnp.int32), TOP_K)
    g_flat = gates.reshape(-1)
    order = jnp.argsort(e_flat)
    e_s, tok_s, g_s = e_flat[order], tok[order], g_flat[order]

    n_assign = T * TOP_K
    n_pad = n_assign + N_EXPERTS * MOE_BLOCK
    n_blocks = n_pad // MOE_BLOCK
    counts = jnp.zeros((N_EXPERTS,), jnp.int32).at[e_flat].add(1)
    padded = ((counts + MOE_BLOCK - 1) // MOE_BLOCK) * MOE_BLOCK
    ends_pad = jnp.cumsum(padded)
    starts_pad = ends_pad - padded
    starts_raw = jnp.cumsum(counts) - counts
    rank = jnp.arange(n_assign, dtype=jnp.int32) - starts_raw[e_s]
    dest = starts_pad[e_s] + rank
    block_start = jnp.arange(n_blocks, dtype=jnp.int32) * MOE_BLOCK
    blk_e = jnp.minimum(jnp.searchsorted(ends_pad, block_start, side='right'),
                        N_EXPERTS - 1).astype(jnp.int32)
    xs = jnp.zeros((n_pad, D), xn.dtype).at[dest].set(xf[tok_s])

    def expert_block(args):
        xb, e = args
        h = jax.nn.silu(xb @ w_gate[e]) * (xb @ w_up[e])
        return h @ w_down[e]

    ys = lax.map(expert_block, (xs.reshape(n_blocks, MOE_BLOCK, D), blk_e)).reshape(n_pad, D)
    y = jnp.zeros((T, D), jnp.float32).at[tok_s].add(
        g_s[:, None] * ys[dest].astype(jnp.float32))
    return y.reshape(B, S, D).astype(xn.dtype)


def setup_inputs(seed: int = 0) -> dict:
    key = jax.random.key(seed)
    ks = jax.random.split(key, 20)

    def nrm(k, shape, fan_in):
        return jax.random.normal(k, shape, jnp.float32) * (fan_in ** -0.5)

    def gain(k, shape):
        return 1.0 + 0.05 * jax.random.normal(k, shape, jnp.float32)

    return {
        "x": jax.random.normal(ks[0], (BATCH, SEQ, D_MODEL), jnp.float32),
        "a_norm": gain(ks[1], (N_A_LAYERS, D_MODEL)),
        "a_proj": nrm(ks[2], (N_A_LAYERS, N_POOL_GROUPS, POOL_GROUP_DIM, POOL_GROUP_DIM), POOL_GROUP_DIM),
        "a_scale": gain(ks[3], (N_A_LAYERS, D_MODEL)),
        "kv_norm": gain(ks[4], (D_MODEL,)),
        "w_kv": nrm(ks[5], (D_MODEL, KV_WIDTH), D_MODEL),
        "b_norm": gain(ks[6], (N_B_LAYERS, D_MODEL)),
        "w_q": nrm(ks[7], (N_B_LAYERS, D_MODEL, Q_WIDTH), D_MODEL),
        "w_o": nrm(ks[8], (N_B_LAYERS, MERGED_WIDTH, D_MODEL), MERGED_WIDTH),
        "rel_bias": 0.5 * jax.random.normal(ks[9], (N_BUCKETS, N_HEADS_TOTAL), jnp.float32),
        "ffn_norm": gain(ks[10], (DEPTH, D_MODEL)),
        "dense_w_gate": nrm(ks[11], (N_DENSE, D_MODEL, D_FF_DENSE), D_MODEL),
        "dense_w_up": nrm(ks[12], (N_DENSE, D_MODEL, D_FF_DENSE), D_MODEL),
        "dense_w_down": nrm(ks[13], (N_DENSE, D_FF_DENSE, D_MODEL), D_FF_DENSE),
        "moe_router": nrm(ks[14], (N_MOE, D_MODEL, N_EXPERTS), D_MODEL),
        "moe_w_gate": nrm(ks[15], (N_MOE, N_EXPERTS, D_MODEL, D_FF_EXPERT), D_MODEL),
        "moe_w_up": nrm(ks[16], (N_MOE, N_EXPERTS, D_MODEL, D_FF_EXPERT), D_MODEL),
        "moe_w_down": nrm(ks[17], (N_MOE, N_EXPERTS, D_FF_EXPERT, D_MODEL), D_FF_EXPERT),
        "final_norm": gain(ks[18], (D_MODEL,)),
    }


def reference(x, a_norm, a_proj, a_scale, kv_norm, w_kv, b_norm, w_q, w_o, rel_bias,
              ffn_norm, dense_w_gate, dense_w_up, dense_w_down,
              moe_router, moe_w_gate, moe_w_up, moe_w_down, final_norm):
    B, S, _ = x.shape
    h = x
    k_sh = v_sh = None
    for l in range(DEPTH):
        if l < N_A_LAYERS:
            h = h + pool_mixer(rmsnorm(h, a_norm[l]), a_proj[l], a_scale[l])
        else:
            if l == N_A_LAYERS:
                kv = (rmsnorm(h, kv_norm) @ w_kv).reshape(
                    B, S, 2, N_ATT_GROUPS, HEADS_PER_GROUP, HEAD_DIM)
                k_sh, v_sh = kv[:, :, 0], kv[:, :, 1]
            j = l - N_A_LAYERS
            h = h + dilated_attention_layer(rmsnorm(h, b_norm[j]), w_q[j], w_o[j],
                                            k_sh, v_sh, rel_bias)
        hn = rmsnorm(h, ffn_norm[l])
        if l % 2 == 0:
            i = l // 2
            h = h + dense_swiglu(hn, dense_w_gate[i], dense_w_up[i], dense_w_down[i])
        else:
            i = l // 2
            h = h + moe_swiglu(hn, moe_router[i], moe_w_gate[i], moe_w_up[i], moe_w_down[i])
    return rmsnorm(h, final_norm)
```

```python
import functools
import math

import jax
import jax.numpy as jnp
from jax import lax
from jax.experimental import pallas as pl
from jax.experimental.pallas import tpu as pltpu

D_MODEL = 1024
EPS = 1e-6
POOL_WINDOWS = (2, 4, 8, 16)
POOL_GROUP_DIM = D_MODEL // len(POOL_WINDOWS)
MAX_POOL_WINDOW = max(POOL_WINDOWS)
HEAD_DIM = 64
HEADS_PER_GROUP = D_MODEL // HEAD_DIM
DILATED_PATTERNS = ((128, 1), (512, 4), (2048, 16))
N_ATT_GROUPS = len(DILATED_PATTERNS)
Q_WIDTH = N_ATT_GROUPS * D_MODEL
QBLK = 128
NEG_INF = -1e30
N_BUCKETS = 32
MAX_EXACT = N_BUCKETS // 2
MAX_DISTANCE = 2048
N_EXPERTS = 8
TOP_K = 2

LANES = 128
HEAD_PAIR = 2 * HEAD_DIM
assert HEAD_PAIR == LANES

POOL_ROWS = 512
DENSE_ROWS = 512
DENSE_FT = 1408
PROJ_ROWS = 1024
PROJ_COLS = 1024
MERGE_ROWS = 512
MOE_ROWS = 1024
MOE_FT = 896
DISPATCH_ROWS = 1024
COMBINE_ROWS = 256
VMEM_LIMIT = 56 * 1024 * 1024

R_E0, R_E1, R_RANK0, R_RANK1, R_G0, R_G1 = 0, 1, 2, 3, 4, 5


def _rms_scale(x):
    return lax.rsqrt(jnp.mean(x * x, axis=-1, keepdims=True) + EPS)


def _silu(g):
    return g * (1.0 / (1.0 + jnp.exp(-g)))


def _split_bf16(x):
    hi = x.astype(jnp.bfloat16)
    lo = (x - hi.astype(jnp.float32)).astype(jnp.bfloat16)
    return hi, lo


def _pool_kernel(x_ref, halo_ref, an_ref, proj_ref, asc_ref, fn_ref, h_ref, hn_ref):
    i = pl.program_id(1)
    x = x_ref[...]
    halo = halo_ref[...]
    gain = an_ref[...]
    xn = x * _rms_scale(x) * gain
    hn_halo = halo * _rms_scale(halo) * gain
    hn_halo = jnp.where(i > 0, hn_halo, 0.0)
    full = jnp.concatenate([hn_halo, xn], axis=0)
    ts = x.shape[0]
    pos = i * ts + lax.broadcasted_iota(jnp.int32, (ts, 1), 0)
    outs = []
    for g, w in enumerate(POOL_WINDOWS):
        c0 = g * POOL_GROUP_DIM
        s = full[:, c0:c0 + POOL_GROUP_DIM]
        span = 1
        while span < w:
            s = s + pltpu.roll(s, span, axis=0)
            span *= 2
        s = s[MAX_POOL_WINDOW:, :]
        cnt = jnp.minimum(pos + 1, w).astype(jnp.float32)
        pooled = s / cnt - xn[:, c0:c0 + POOL_GROUP_DIM]
        outs.append(jnp.dot(pooled.astype(jnp.bfloat16), proj_ref[g],
                            preferred_element_type=jnp.float32))
    mix = jnp.concatenate(outs, axis=1) * asc_ref[...]
    h = x + mix
    h_ref[...] = h
    hn_ref[...] = (h * _rms_scale(h) * fn_ref[...]).astype(hn_ref.dtype)


def _pool_layer(x, a_norm, a_proj, a_scale, ffn_norm):
    B, S, D = x.shape
    ts = POOL_ROWS
    halo_blocks = ts // MAX_POOL_WINDOW
    vec = pl.BlockSpec((1, D), lambda b, i: (0, 0))
    return pl.pallas_call(
        _pool_kernel,
        grid=(B, S // ts),
        in_specs=[
            pl.BlockSpec((None, ts, D), lambda b, i: (b, i, 0)),
            pl.BlockSpec((None, MAX_POOL_WINDOW, D),
                         lambda b, i: (b, jnp.maximum(i * halo_blocks - 1, 0), 0)),
            vec,
            pl.BlockSpec(a_proj.shape, lambda b, i: (0, 0, 0)),
            vec, vec,
        ],
        out_specs=[pl.BlockSpec((None, ts, D), lambda b, i: (b, i, 0)),
                   pl.BlockSpec((None, ts, D), lambda b, i: (b, i, 0))],
        out_shape=[jax.ShapeDtypeStruct((B, S, D), jnp.float32),
                   jax.ShapeDtypeStruct((B, S, D), jnp.bfloat16)],
        compiler_params=pltpu.CompilerParams(
            dimension_semantics=("arbitrary", "arbitrary"), vmem_limit_bytes=VMEM_LIMIT),
        name="pool_mixer",
    )(x, x, a_norm.reshape(1, D), a_proj.astype(jnp.bfloat16), a_scale.reshape(1, D),
      ffn_norm.reshape(1, D))


def _dense_kernel(hn_ref, h_ref, wg_ref, wu_ref, wd_ref, kvn_g_ref, qn_g_ref,
                  h_out_ref, kvn_ref, qn_ref, acc_ref):
    f = pl.program_id(1)
    x = hn_ref[...]
    g = jnp.dot(x, wg_ref[...], preferred_element_type=jnp.float32)
    u = jnp.dot(x, wu_ref[...], preferred_element_type=jnp.float32)
    hid = (_silu(g) * u).astype(jnp.bfloat16)
    part = jnp.dot(hid, wd_ref[...], preferred_element_type=jnp.float32)

    @pl.when(f == 0)
    def _():
        acc_ref[...] = part

    @pl.when(f > 0)
    def _():
        acc_ref[...] += part

    @pl.when(f == pl.num_programs(1) - 1)
    def _():
        h = h_ref[...] + acc_ref[...]
        h_out_ref[...] = h
        hs = h * _rms_scale(h)
        kvn_ref[...] = (hs * kvn_g_ref[...]).astype(kvn_ref.dtype)
        qn_ref[...] = (hs * qn_g_ref[...]).astype(qn_ref.dtype)


def _dense_layer(hn, h, w_gate, w_up, w_down, kv_norm, b_norm):
    T, D = h.shape
    F = w_gate.shape[1]
    tm, tf = DENSE_ROWS, DENSE_FT
    row = lambda i, f: (i, 0)
    vec = pl.BlockSpec((1, D), lambda i, f: (0, 0))
    return pl.pallas_call(
        _dense_kernel,
        grid=(T // tm, F // tf),
        in_specs=[
            pl.BlockSpec((tm, D), row),
            pl.BlockSpec((tm, D), row),
            pl.BlockSpec((D, tf), lambda i, f: (0, f)),
            pl.BlockSpec((D, tf), lambda i, f: (0, f)),
            pl.BlockSpec((tf, D), lambda i, f: (f, 0)),
            vec, vec,
        ],
        out_specs=[pl.BlockSpec((tm, D), row)] * 3,
        out_shape=[jax.ShapeDtypeStruct((T, D), jnp.float32),
                   jax.ShapeDtypeStruct((T, D), jnp.bfloat16),
                   jax.ShapeDtypeStruct((T, D), jnp.bfloat16)],
        scratch_shapes=[pltpu.VMEM((tm, D), jnp.float32)],
        compiler_params=pltpu.CompilerParams(
            dimension_semantics=("arbitrary", "arbitrary"), vmem_limit_bytes=VMEM_LIMIT),
        name="dense_swiglu",
    )(hn, h, w_gate.astype(jnp.bfloat16), w_up.astype(jnp.bfloat16),
      w_down.astype(jnp.bfloat16), kv_norm.reshape(1, D), b_norm.reshape(1, D))


def _proj_kernel(x_ref, w_ref, o_ref, *, out_scale):
    acc = jnp.dot(x_ref[...], w_ref[...], preferred_element_type=jnp.float32)
    o_ref[...] = (acc * out_scale).astype(o_ref.dtype)


def _project(x, w, out_scale, name):
    T, D = x.shape
    N = w.shape[1]
    tm, tn = PROJ_ROWS, PROJ_COLS
    return pl.pallas_call(
        functools.partial(_proj_kernel, out_scale=out_scale),
        grid=(T // tm, N // tn),
        in_specs=[pl.BlockSpec((tm, D), lambda i, j: (i, 0)),
                  pl.BlockSpec((D, tn), lambda i, j: (0, j))],
        out_specs=pl.BlockSpec((tm, tn), lambda i, j: (i, j)),
        out_shape=jax.ShapeDtypeStruct((T, N), jnp.bfloat16),
        compiler_params=pltpu.CompilerParams(
            dimension_semantics=("arbitrary", "arbitrary"), vmem_limit_bytes=VMEM_LIMIT),
        name=name,
    )(x, w.astype(jnp.bfloat16))


def _attn_kernel(q_ref, kp_ref, kc_ref, vp_ref, vc_ref, bias_ref, o_ref, lse_ref):
    j = pl.program_id(2)
    lane = lax.broadcasted_iota(jnp.int32, (QBLK, LANES), 1)
    first_head = lane < HEAD_DIM
    key_col = lax.broadcasted_iota(jnp.int32, (2 * QBLK, 2 * QBLK), 1)
    key_ok = jnp.logical_or(key_col >= QBLK, j > 0)
    stats = jnp.zeros((QBLK, LANES), jnp.float32)
    for hp in range(HEADS_PER_GROUP // 2):
        sl = slice(hp * HEAD_PAIR, (hp + 1) * HEAD_PAIR)
        qp = q_ref[:, sl]
        zero = jnp.zeros_like(qp)
        q2 = jnp.concatenate([jnp.where(first_head, qp, zero),
                              jnp.where(first_head, zero, qp)], axis=0)
        kcat = jnp.concatenate([kp_ref[:, sl], kc_ref[:, sl]], axis=0)
        s = lax.dot_general(q2, kcat, (((1,), (1,)), ((), ())),
                            preferred_element_type=jnp.float32)
        s = s + bias_ref[hp * 2 * QBLK:(hp + 1) * 2 * QBLK, :]
        s = jnp.where(key_ok, s, NEG_INF)
        m = jnp.max(s, axis=1, keepdims=True)
        p = jnp.exp(s - m)
        l = jnp.sum(p, axis=1, keepdims=True)
        vcat = jnp.concatenate([vp_ref[:, sl], vc_ref[:, sl]], axis=0)
        o2 = jnp.dot(p.astype(jnp.bfloat16), vcat, preferred_element_type=jnp.float32)
        o2 = o2 / l
        o_ref[:, sl] = jnp.where(first_head, o2[:QBLK], o2[QBLK:]).astype(o_ref.dtype)
        lse = m + jnp.log(l)
        stats = jnp.where(lane == 2 * hp, lse[:QBLK], stats)
        stats = jnp.where(lane == 2 * hp + 1, lse[QBLK:], stats)
    lse_ref[...] = stats


def _attn_bias(rel_bias, g, dilation):
    window = DILATED_PATTERNS[g][0]
    W = window // dilation
    a = jnp.arange(QBLK, dtype=jnp.int32)[:, None]
    c = jnp.arange(2 * QBLK, dtype=jnp.int32)[None, :]
    m = a + QBLK - c
    band = (m >= 0) & (m <= W)
    n = jnp.maximum(m, 0) * dilation
    nf = jnp.maximum(n, 1).astype(jnp.float32)
    large = MAX_EXACT + (jnp.log(nf / MAX_EXACT) / math.log(MAX_DISTANCE / MAX_EXACT)
                         * (N_BUCKETS - MAX_EXACT)).astype(jnp.int32)
    large = jnp.minimum(large, N_BUCKETS - 1)
    bucket = jnp.where(n < MAX_EXACT, n, large)
    table = rel_bias[:, g * HEADS_PER_GROUP:(g + 1) * HEADS_PER_GROUP]
    bias = table[bucket].transpose(2, 0, 1).astype(jnp.float32)
    bias = jnp.where(band[None], bias, NEG_INF)
    return bias.reshape(HEADS_PER_GROUP * QBLK, 2 * QBLK)


def _attention_group(q, kv, rel_bias, g, B, S):
    _, d = DILATED_PATTERNS[g]
    L = S // d
    n = L // QBLK
    D = D_MODEL
    qv = q.reshape(B, L, d * N_ATT_GROUPS * D)
    kvv = kv.reshape(B, L, d * 2 * N_ATT_GROUPS * D)
    bias = _attn_bias(rel_bias, g, d)
    qcol = lambda r: r * N_ATT_GROUPS + g
    kcol = lambda r: r * 2 * N_ATT_GROUPS + g
    vcol = lambda r: r * 2 * N_ATT_GROUPS + N_ATT_GROUPS + g
    blk = (None, QBLK, D)
    prev = lambda j: jnp.maximum(j - 1, 0)
    o, lse = pl.pallas_call(
        _attn_kernel,
        grid=(B, d, n),
        in_specs=[
            pl.BlockSpec(blk, lambda b, r, j: (b, j, qcol(r))),
            pl.BlockSpec(blk, lambda b, r, j: (b, prev(j), kcol(r))),
            pl.BlockSpec(blk, lambda b, r, j: (b, j, kcol(r))),
            pl.BlockSpec(blk, lambda b, r, j: (b, prev(j), vcol(r))),
            pl.BlockSpec(blk, lambda b, r, j: (b, j, vcol(r))),
            pl.BlockSpec(bias.shape, lambda b, r, j: (0, 0)),
        ],
        out_specs=[pl.BlockSpec(blk, lambda b, r, j: (b, j, r)),
                   pl.BlockSpec((None, QBLK, LANES), lambda b, r, j: (b, j, r))],
        out_shape=[jax.ShapeDtypeStruct((B, L, d * D), jnp.float32),
                   jax.ShapeDtypeStruct((B, L, d * LANES), jnp.float32)],
        compiler_params=pltpu.CompilerParams(
            dimension_semantics=("arbitrary",) * 3, vmem_limit_bytes=VMEM_LIMIT),
        name=f"dilated_attn_g{g}",
    )(qv, kvv, kvv, kvv, kvv, bias)
    return o.reshape(B * S, D), lse.reshape(B * S, LANES)


def _merge_kernel(o0_ref, o1_ref, o2_ref, s0_ref, s1_ref, s2_ref, h_ref, wo_ref, fn_ref,
                  rhi_ref, rlo_ref, expand_ref,
                  h_out_ref, hn_ref, route_ref, counts_ref, carry_ref):
    i = pl.program_id(0)
    tm = h_ref.shape[0]

    @pl.when(i == 0)
    def _():
        carry_ref[...] = jnp.zeros_like(carry_ref)

    lses = [s0_ref[...], s1_ref[...], s2_ref[...]]
    mx = jnp.maximum(jnp.maximum(lses[0], lses[1]), lses[2])
    es = [jnp.exp(v - mx) for v in lses]
    inv = 1.0 / (es[0] + es[1] + es[2])
    merged = None
    for e, o_ref in zip(es, (o0_ref, o1_ref, o2_ref)):
        hi, lo = _split_bf16(e * inv)
        wide = (jnp.dot(hi, expand_ref[...], preferred_element_type=jnp.float32)
                + jnp.dot(lo, expand_ref[...], preferred_element_type=jnp.float32))
        term = wide * o_ref[...]
        merged = term if merged is None else merged + term
    h = h_ref[...] + jnp.dot(merged.astype(jnp.bfloat16), wo_ref[...],
                             preferred_element_type=jnp.float32)
    h_out_ref[...] = h
    hn = h * _rms_scale(h) * fn_ref[...]
    hn_ref[...] = hn

    hi, lo = _split_bf16(hn)
    logits = (jnp.dot(hi, rhi_ref[...], preferred_element_type=jnp.float32)
              + jnp.dot(lo, rhi_ref[...], preferred_element_type=jnp.float32)
              + jnp.dot(hi, rlo_ref[...], preferred_element_type=jnp.float32))
    lane = lax.broadcasted_iota(jnp.int32, (tm, LANES), 1)
    logits = jnp.where(lane < N_EXPERTS, logits, -jnp.inf)
    v0 = jnp.max(logits, axis=1, keepdims=True)
    e0 = jnp.min(jnp.where(logits == v0, lane, LANES), axis=1, keepdims=True)
    rest = jnp.where(lane == e0, -jnp.inf, logits)
    v1 = jnp.max(rest, axis=1, keepdims=True)
    e1 = jnp.min(jnp.where(rest == v1, lane, LANES), axis=1, keepdims=True)
    t = jnp.exp(v1 - v0)
    g0 = 1.0 / (1.0 + t)
    g1 = t / (1.0 + t)

    hit0 = lane == e0
    hit1 = lane == e1
    onehot = jnp.where(jnp.logical_or(hit0, hit1), 1.0, 0.0)
    r_i = lax.broadcasted_iota(jnp.int32, (tm, tm), 0)
    c_i = lax.broadcasted_iota(jnp.int32, (tm, tm), 1)
    lower = jnp.where(c_i < r_i, 1.0, 0.0).astype(jnp.bfloat16)
    before = jnp.dot(lower, onehot.astype(jnp.bfloat16),
                     preferred_element_type=jnp.float32) + carry_ref[0:1, :]
    rank0 = jnp.sum(jnp.where(hit0, before, 0.0), axis=1, keepdims=True)
    rank1 = jnp.sum(jnp.where(hit1, before, 0.0), axis=1, keepdims=True)
    total = carry_ref[0:1, :] + jnp.sum(onehot, axis=0, keepdims=True)
    carry_ref[...] = jnp.broadcast_to(total, carry_ref.shape)
    counts_ref[...] = jnp.broadcast_to(total, counts_ref.shape)

    route = jnp.zeros((tm, LANES), jnp.float32)
    for ln, val in ((R_E0, e0.astype(jnp.float32)), (R_E1, e1.astype(jnp.float32)),
                    (R_RANK0, rank0), (R_RANK1, rank1), (R_G0, g0), (R_G1, g1)):
        route = jnp.where(lane == ln, val, route)
    route_ref[...] = route


def _merge_layer(outs, lses, h, w_o, ffn_norm, w_router):
    T, D = h.shape
    tm = MERGE_ROWS
    r_pad = jnp.zeros((D, LANES), jnp.float32).at[:, :N_EXPERTS].set(w_router)
    r_hi, r_lo = _split_bf16(r_pad)
    head_of_col = jnp.arange(D, dtype=jnp.int32) // HEAD_DIM
    expand = (jnp.arange(LANES, dtype=jnp.int32)[:, None] == head_of_col[None, :]
              ).astype(jnp.bfloat16)
    row = lambda i: (i, 0)
    const = lambda i: (0, 0)
    wide = pl.BlockSpec((tm, D), row)
    narrow = pl.BlockSpec((tm, LANES), row)
    return pl.pallas_call(
        _merge_kernel,
        grid=(T // tm,),
        in_specs=[wide, wide, wide, narrow, narrow, narrow, wide,
                  pl.BlockSpec((D, D), const), pl.BlockSpec((1, D), const),
                  pl.BlockSpec((D, LANES), const), pl.BlockSpec((D, LANES), const),
                  pl.BlockSpec((LANES, D), const)],
        out_specs=[wide, wide, narrow, pl.BlockSpec((8, LANES), const)],
        out_shape=[jax.ShapeDtypeStruct((T, D), jnp.float32),
                   jax.ShapeDtypeStruct((T, D), jnp.float32),
                   jax.ShapeDtypeStruct((T, LANES), jnp.float32),
                   jax.ShapeDtypeStruct((8, LANES), jnp.float32)],
        scratch_shapes=[pltpu.VMEM((8, LANES), jnp.float32)],
        compiler_params=pltpu.CompilerParams(
            dimension_semantics=("arbitrary",), vmem_limit_bytes=VMEM_LIMIT),
        name="merge_outproj_router",
    )(*outs, *lses, h, w_o.astype(jnp.bfloat16), ffn_norm.reshape(1, D), r_hi, r_lo, expand)


def _dispatch_kernel(dest_ref, hn_ref, xs_in_ref, xs_ref, sem):
    del xs_in_ref
    base = pl.program_id(0) * DISPATCH_ROWS

    def copy(t, k):
        tok = base + t
        return pltpu.make_async_copy(hn_ref.at[pl.ds(tok, 1), :],
                                     xs_ref.at[pl.ds(dest_ref[TOP_K * tok + k], 1), :], sem)

    def start(t, c):
        for k in range(TOP_K):
            copy(t, k).start()
        return c

    def wait(t, c):
        for k in range(TOP_K):
            copy(t, k).wait()
        return c

    lax.fori_loop(0, DISPATCH_ROWS, start, 0)
    lax.fori_loop(0, DISPATCH_ROWS, wait, 0)


def _dispatch(dest_flat, hn, n_slots):
    T, D = hn.shape
    xs0 = jnp.zeros((n_slots, D), hn.dtype)
    return pl.pallas_call(
        _dispatch_kernel,
        grid_spec=pltpu.PrefetchScalarGridSpec(
            num_scalar_prefetch=1,
            grid=(T // DISPATCH_ROWS,),
            in_specs=[pl.BlockSpec(memory_space=pl.ANY), pl.BlockSpec(memory_space=pl.ANY)],
            out_specs=pl.BlockSpec(memory_space=pl.ANY),
            scratch_shapes=[pltpu.SemaphoreType.DMA(())],
        ),
        out_shape=jax.ShapeDtypeStruct((n_slots, D), hn.dtype),
        input_output_aliases={2: 0},
        compiler_params=pltpu.CompilerParams(dimension_semantics=("arbitrary",)),
        name="moe_dispatch",
    )(dest_flat, hn, xs0)


def _expert_kernel(tile_e_ref, used_ref, x_ref, wg_ref, wu_ref, wd_ref, y_ref, xb_ref, acc_ref):
    del tile_e_ref
    i = pl.program_id(0)
    f = pl.program_id(1)
    active = i < used_ref[0]

    @pl.when(jnp.logical_and(active, f == 0))
    def _():
        xb_ref[...] = x_ref[...].astype(xb_ref.dtype)

    @pl.when(active)
    def _():
        x = xb_ref[...]
        g = jnp.dot(x, wg_ref[...], preferred_element_type=jnp.float32)
        u = jnp.dot(x, wu_ref[...], preferred_element_type=jnp.float32)
        hid = (_silu(g) * u).astype(jnp.bfloat16)
        part = jnp.dot(hid, wd_ref[...], preferred_element_type=jnp.float32)

        @pl.when(f == 0)
        def _():
            acc_ref[...] = part

        @pl.when(f > 0)
        def _():
            acc_ref[...] += part

        @pl.when(f == pl.num_programs(1) - 1)
        def _():
            y_ref[...] = acc_ref[...]

    @pl.when(jnp.logical_and(jnp.logical_not(active), f == pl.num_programs(1) - 1))
    def _():
        y_ref[...] = jnp.zeros_like(y_ref)


def _expert_ffn(tile_e, used, xs, w_gate, w_up, w_down):
    n_slots, D = xs.shape
    F = w_gate.shape[2]
    tm, tf = MOE_ROWS, MOE_FT
    n_f = F // tf
    n_tiles = n_slots // tm

    def live(i, used_ref):
        return jnp.minimum(i, used_ref[0] - 1)

    def f_eff(i, f, used_ref):
        return jnp.where(i < used_ref[0], f, n_f - 1)

    row = lambda i, f, te, us: (live(i, us), 0)
    return pl.pallas_call(
        _expert_kernel,
        grid_spec=pltpu.PrefetchScalarGridSpec(
            num_scalar_prefetch=2,
            grid=(n_tiles, n_f),
            in_specs=[
                pl.BlockSpec((tm, D), row),
                pl.BlockSpec((None, D, tf), lambda i, f, te, us: (te[live(i, us)], 0, f_eff(i, f, us))),
                pl.BlockSpec((None, D, tf), lambda i, f, te, us: (te[live(i, us)], 0, f_eff(i, f, us))),
                pl.BlockSpec((None, tf, D), lambda i, f, te, us: (te[live(i, us)], f_eff(i, f, us), 0)),
            ],
            out_specs=pl.BlockSpec((tm, D), lambda i, f, te, us: (i, 0)),
            scratch_shapes=[pltpu.VMEM((tm, D), jnp.bfloat16), pltpu.VMEM((tm, D), jnp.float32)],
        ),
        out_shape=jax.ShapeDtypeStruct((n_slots, D), jnp.float32),
        compiler_params=pltpu.CompilerParams(
            dimension_semantics=("arbitrary", "arbitrary"), vmem_limit_bytes=VMEM_LIMIT),
        name="moe_experts",
    )(tile_e, used, xs, w_gate.astype(jnp.bfloat16), w_up.astype(jnp.bfloat16),
      w_down.astype(jnp.bfloat16))


def _combine_kernel(dest_ref, h_ref, route_ref, fin_ref, ys_ref, o_ref, ybuf, sem):
    base = pl.program_id(0) * COMBINE_ROWS

    def copy(t, k):
        src = dest_ref[TOP_K * (base + t) + k]
        return pltpu.make_async_copy(ys_ref.at[pl.ds(src, 1), :],
                                     ybuf.at[k, pl.ds(t, 1), :], sem)

    def start(t, c):
        for k in range(TOP_K):
            copy(t, k).start()
        return c

    def wait(t, c):
        for k in range(TOP_K):
            copy(t, k).wait()
        return c

    lax.fori_loop(0, COMBINE_ROWS, start, 0)
    lax.fori_loop(0, COMBINE_ROWS, wait, 0)
    route = route_ref[...]
    g0 = route[:, R_G0:R_G0 + 1]
    g1 = route[:, R_G1:R_G1 + 1]
    h = h_ref[...] + (g0 * ybuf[0] + g1 * ybuf[1])
    o_ref[...] = h * _rms_scale(h) * fin_ref[...]


def _combine(dest_flat, h, route, final_norm, ys):
    T, D = h.shape
    tc = COMBINE_ROWS
    return pl.pallas_call(
        _combine_kernel,
        grid_spec=pltpu.PrefetchScalarGridSpec(
            num_scalar_prefetch=1,
            grid=(T // tc,),
            in_specs=[pl.BlockSpec((tc, D), lambda i, d: (i, 0)),
                      pl.BlockSpec((tc, LANES), lambda i, d: (i, 0)),
                      pl.BlockSpec((1, D), lambda i, d: (0, 0)),
                      pl.BlockSpec(memory_space=pl.ANY)],
            out_specs=pl.BlockSpec((tc, D), lambda i, d: (i, 0)),
            scratch_shapes=[pltpu.VMEM((TOP_K, tc, D), jnp.float32),
                            pltpu.SemaphoreType.DMA(())],
        ),
        out_shape=jax.ShapeDtypeStruct((T, D), jnp.float32),
        compiler_params=pltpu.CompilerParams(
            dimension_semantics=("arbitrary",), vmem_limit_bytes=VMEM_LIMIT),
        name="moe_combine_final_norm",
    )(dest_flat, h, route, final_norm.reshape(1, D), ys)


def _routing_tables(route, counts_row):
    T = route.shape[0]
    counts = counts_row[0, :N_EXPERTS].astype(jnp.int32)
    padded = ((counts + MOE_ROWS - 1) // MOE_ROWS) * MOE_ROWS
    ends = jnp.cumsum(padded)
    starts = ends - padded
    expert = route[:, R_E0:R_E1 + 1].astype(jnp.int32)
    rank = route[:, R_RANK0:R_RANK1 + 1].astype(jnp.int32)
    onehot = expert[..., None] == jnp.arange(N_EXPERTS, dtype=jnp.int32)
    dest = jnp.sum(jnp.where(onehot, starts, 0), axis=-1) + rank
    n_tiles = (TOP_K * T) // MOE_ROWS + N_EXPERTS
    tile_start = jnp.arange(n_tiles, dtype=jnp.int32) * MOE_ROWS
    tile_e = jnp.minimum(jnp.sum(ends[None, :] <= tile_start[:, None], axis=1),
                         N_EXPERTS - 1).astype(jnp.int32)
    used = (ends[-1:] // MOE_ROWS).astype(jnp.int32)
    return dest.reshape(-1), tile_e, used, n_tiles * MOE_ROWS


def kernel(x, a_norm, a_proj, a_scale, kv_norm, w_kv, b_norm, w_q, w_o, rel_bias, ffn_norm,
           dense_w_gate, dense_w_up, dense_w_down, moe_router, moe_w_gate, moe_w_up,
           moe_w_down, final_norm):
    B, S, D = x.shape
    T = B * S
    h1, hn1 = _pool_layer(x, a_norm[0], a_proj[0], a_scale[0], ffn_norm[0])
    h2, kvn, qn = _dense_layer(hn1.reshape(T, D), h1.reshape(T, D), dense_w_gate[0],
                               dense_w_up[0], dense_w_down[0], kv_norm, b_norm[0])
    kv = _project(kvn, w_kv, 1.0, "kv_proj")
    q = _project(qn, w_q[0], HEAD_DIM ** -0.5, "q_proj")
    outs, lses = [], []
    for g in range(N_ATT_GROUPS):
        o, lse = _attention_group(q, kv, rel_bias, g, B, S)
        outs.append(o)
        lses.append(lse)
    h3, hn3, route, counts = _merge_layer(outs, lses, h2, w_o[0], ffn_norm[1], moe_router[0])
    dest, tile_e, used, n_slots = _routing_tables(route, counts)
    xs = _dispatch(dest, hn3, n_slots)
    ys = _expert_ffn(tile_e, used, xs, moe_w_gate[0], moe_w_up[0], moe_w_down[0])
    out = _combine(dest, h3, route, final_norm, ys)
    return out.reshape(B, S, D)
```

```python
import functools
import math

import jax
import jax.numpy as jnp
from jax import lax
from jax.experimental import pallas as pl
from jax.experimental.pallas import tpu as pltpu

D_MODEL = 1024
EPS = 1e-6
POOL_WINDOWS = (2, 4, 8, 16)
POOL_GROUP_DIM = D_MODEL // len(POOL_WINDOWS)
MAX_POOL_WINDOW = max(POOL_WINDOWS)
HEAD_DIM = 64
HEADS_PER_GROUP = D_MODEL // HEAD_DIM
DILATED_PATTERNS = ((128, 1), (512, 4), (2048, 16))
N_ATT_GROUPS = len(DILATED_PATTERNS)
QBLK = 128
NEG_INF = -1e30
N_BUCKETS = 32
MAX_EXACT = N_BUCKETS // 2
MAX_DISTANCE = 2048
N_EXPERTS = 8
TOP_K = 2

LANES = 128
N_SLABS = D_MODEL // LANES
HEAD_PAIR = 2 * HEAD_DIM
assert HEAD_PAIR == LANES

POOL_ROWS = 512
DENSE_ROWS = 512
DENSE_FT = 1408
QKV_ROWS = 512
MERGE_ROWS = 512
MOE_ROWS = 1024
MOE_FT = 896
DISPATCH_ROWS = 1024
COMBINE_ROWS = 512
DMA_UNROLL = 8
VMEM_LIMIT = 56 * 1024 * 1024

R_E0, R_E1, R_RANK0, R_RANK1, R_G0, R_G1 = 0, 1, 2, 3, 4, 5


def _rms_scale(x):
    return lax.rsqrt(jnp.mean(x * x, axis=-1, keepdims=True) + EPS)


def _silu(g):
    return g * (1.0 / (1.0 + jnp.exp(-g)))


def _split_bf16(x):
    hi = x.astype(jnp.bfloat16)
    lo = (x - hi.astype(jnp.float32)).astype(jnp.bfloat16)
    return hi, lo


def _pool_kernel(x_ref, halo_ref, an_ref, proj_ref, asc_ref, fn_ref, h_ref, hn_ref):
    i = pl.program_id(1)
    x = x_ref[...]
    halo = halo_ref[...]
    gain = an_ref[...]
    xn = x * _rms_scale(x) * gain
    hn_halo = halo * _rms_scale(halo) * gain
    hn_halo = jnp.where(i > 0, hn_halo, 0.0)
    full = jnp.concatenate([hn_halo, xn], axis=0)
    ts = x.shape[0]
    pos = i * ts + lax.broadcasted_iota(jnp.int32, (ts, 1), 0)
    outs = []
    for g, w in enumerate(POOL_WINDOWS):
        c0 = g * POOL_GROUP_DIM
        s = full[:, c0:c0 + POOL_GROUP_DIM]
        span = 1
        while span < w:
            s = s + pltpu.roll(s, span, axis=0)
            span *= 2
        s = s[MAX_POOL_WINDOW:, :]
        cnt = jnp.minimum(pos + 1, w).astype(jnp.float32)
        pooled = s / cnt - xn[:, c0:c0 + POOL_GROUP_DIM]
        outs.append(jnp.dot(pooled.astype(jnp.bfloat16), proj_ref[g],
                            preferred_element_type=jnp.float32))
    mix = jnp.concatenate(outs, axis=1) * asc_ref[...]
    h = x + mix
    h_ref[...] = h
    hn_ref[...] = (h * _rms_scale(h) * fn_ref[...]).astype(hn_ref.dtype)


def _pool_layer(x, a_norm, a_proj, a_scale, ffn_norm):
    B, S, D = x.shape
    ts = POOL_ROWS
    halo_blocks = ts // MAX_POOL_WINDOW
    vec = pl.BlockSpec((1, D), lambda b, i: (0, 0))
    return pl.pallas_call(
        _pool_kernel,
        grid=(B, S // ts),
        in_specs=[
            pl.BlockSpec((None, ts, D), lambda b, i: (b, i, 0)),
            pl.BlockSpec((None, MAX_POOL_WINDOW, D),
                         lambda b, i: (b, jnp.maximum(i * halo_blocks - 1, 0), 0)),
            vec,
            pl.BlockSpec(a_proj.shape, lambda b, i: (0, 0, 0)),
            vec, vec,
        ],
        out_specs=[pl.BlockSpec((None, ts, D), lambda b, i: (b, i, 0)),
                   pl.BlockSpec((None, ts, D), lambda b, i: (b, i, 0))],
        out_shape=[jax.ShapeDtypeStruct((B, S, D), jnp.float32),
                   jax.ShapeDtypeStruct((B, S, D), jnp.bfloat16)],
        compiler_params=pltpu.CompilerParams(
            dimension_semantics=("arbitrary", "arbitrary"), vmem_limit_bytes=VMEM_LIMIT),
        name="pool_mixer",
    )(x, x, a_norm.reshape(1, D), a_proj.astype(jnp.bfloat16), a_scale.reshape(1, D),
      ffn_norm.reshape(1, D))


def _dense_kernel(hn_ref, h_ref, wg_ref, wu_ref, wd_ref, h_out_ref, acc_ref):
    f = pl.program_id(1)
    x = hn_ref[...]
    g = jnp.dot(x, wg_ref[...], preferred_element_type=jnp.float32)
    u = jnp.dot(x, wu_ref[...], preferred_element_type=jnp.float32)
    hid = (_silu(g) * u).astype(jnp.bfloat16)
    part = jnp.dot(hid, wd_ref[...], preferred_element_type=jnp.float32)

    @pl.when(f == 0)
    def _():
        acc_ref[...] = part

    @pl.when(f > 0)
    def _():
        acc_ref[...] += part

    @pl.when(f == pl.num_programs(1) - 1)
    def _():
        h_out_ref[...] = h_ref[...] + acc_ref[...]


def _dense_layer(hn, h, w_gate, w_up, w_down):
    T, D = h.shape
    F = w_gate.shape[1]
    tm, tf = DENSE_ROWS, DENSE_FT
    row = lambda i, f: (i, 0)
    return pl.pallas_call(
        _dense_kernel,
        grid=(T // tm, F // tf),
        in_specs=[
            pl.BlockSpec((tm, D), row),
            pl.BlockSpec((tm, D), row),
            pl.BlockSpec((D, tf), lambda i, f: (0, f)),
            pl.BlockSpec((D, tf), lambda i, f: (0, f)),
            pl.BlockSpec((tf, D), lambda i, f: (f, 0)),
        ],
        out_specs=pl.BlockSpec((tm, D), row),
        out_shape=jax.ShapeDtypeStruct((T, D), jnp.float32),
        scratch_shapes=[pltpu.VMEM((tm, D), jnp.float32)],
        compiler_params=pltpu.CompilerParams(
            dimension_semantics=("arbitrary", "arbitrary"), vmem_limit_bytes=VMEM_LIMIT),
        name="dense_swiglu",
    )(hn, h, w_gate.astype(jnp.bfloat16), w_up.astype(jnp.bfloat16),
      w_down.astype(jnp.bfloat16))


def _qkv_kernel(*refs, dilation):
    slabs = refs[:N_SLABS]
    kvg_ref, qg_ref, w_ref, q_ref, k_ref, v_ref = refs[N_SLABS:]
    tm = slabs[0].shape[0]
    n = tm // dilation
    cols = []
    for slab in slabs:
        if dilation == 1:
            cols.append(slab[...])
        else:
            cols.append(jnp.concatenate(
                [slab[pl.ds(r, n, stride=dilation), :] for r in range(dilation)], axis=0))
    x = jnp.concatenate(cols, axis=1)
    xs = x * _rms_scale(x)
    xq = (xs * qg_ref[...]).astype(jnp.bfloat16)
    xkv = (xs * kvg_ref[...]).astype(jnp.bfloat16)
    q = jnp.dot(xq, w_ref[0], preferred_element_type=jnp.float32) * (HEAD_DIM ** -0.5)
    k = jnp.dot(xkv, w_ref[1], preferred_element_type=jnp.float32)
    v = jnp.dot(xkv, w_ref[2], preferred_element_type=jnp.float32)
    for out_ref, val in ((q_ref, q), (k_ref, k), (v_ref, v)):
        for r in range(dilation):
            out_ref[r] = val[r * n:(r + 1) * n].astype(out_ref.dtype)


def _qkv_project(h, kv_norm, b_norm, w_q, w_kv, g, B, S):
    T, D = h.shape
    d = DILATED_PATTERNS[g][1]
    L = S // d
    tm = QKV_ROWS
    n = tm // d
    tiles_per_seq = S // tm
    w = jnp.stack([w_q[:, g * D:(g + 1) * D],
                   w_kv[:, g * D:(g + 1) * D],
                   w_kv[:, (N_ATT_GROUPS + g) * D:(N_ATT_GROUPS + g + 1) * D]]).astype(jnp.bfloat16)
    slab_specs = [pl.BlockSpec((tm, LANES), functools.partial(lambda i, c: (i, c), c=c))
                  for c in range(N_SLABS)]
    vec = pl.BlockSpec((1, D), lambda i: (0, 0))
    out_spec = pl.BlockSpec((None, d, n, D),
                            lambda i: (i // tiles_per_seq, 0, i % tiles_per_seq, 0))
    out_shape = jax.ShapeDtypeStruct((B, d, L, D), jnp.bfloat16)
    return pl.pallas_call(
        functools.partial(_qkv_kernel, dilation=d),
        grid=(T // tm,),
        in_specs=slab_specs + [vec, vec, pl.BlockSpec((3, D, D), lambda i: (0, 0, 0))],
        out_specs=[out_spec] * 3,
        out_shape=[out_shape] * 3,
        compiler_params=pltpu.CompilerParams(
            dimension_semantics=("arbitrary",), vmem_limit_bytes=VMEM_LIMIT),
        name=f"qkv_proj_g{g}",
    )(*([h] * N_SLABS), kv_norm.reshape(1, D), b_norm.reshape(1, D), w)


def _attn_kernel(*refs, dilation, has_prev):
    if has_prev:
        (table_ref, bmap_ref, q_ref, kp_ref, kc_ref, vp_ref, vc_ref,
         o_ref, lse_ref, bias_ref) = refs
    else:
        table_ref, bmap_ref, q_ref, kc_ref, vc_ref, o_ref, lse_ref, bias_ref = refs
    b, j, r = pl.program_id(0), pl.program_id(1), pl.program_id(2)
    n_keys = bias_ref.shape[1]

    @pl.when(jnp.logical_and(jnp.logical_and(b == 0, j == 0), r == 0))
    def _():
        bmap = bmap_ref[...]

        def head(h, carry):
            acc = jnp.full(bmap.shape, NEG_INF, jnp.float32)
            for bucket in range(N_BUCKETS):
                acc = jnp.where(bmap == bucket, table_ref[bucket, h], acc)
            bias_ref[pl.ds(pl.multiple_of(h * QBLK, QBLK), QBLK), :] = acc
            return carry

        lax.fori_loop(0, HEADS_PER_GROUP, head, 0)

    lane = lax.broadcasted_iota(jnp.int32, (QBLK, LANES), 1)
    first_head = lane < HEAD_DIM
    if has_prev:
        key_col = lax.broadcasted_iota(jnp.int32, (2 * QBLK, n_keys), 1)
        key_ok = jnp.logical_or(key_col >= QBLK, j > 0)
    rows = pl.ds(r, QBLK, stride=dilation) if dilation > 1 else slice(None)
    stats = jnp.zeros((QBLK, LANES), jnp.float32)
    for hp in range(HEADS_PER_GROUP // 2):
        sl = slice(hp * HEAD_PAIR, (hp + 1) * HEAD_PAIR)
        qp = q_ref[:, sl]
        zero = jnp.zeros_like(qp)
        q2 = jnp.concatenate([jnp.where(first_head, qp, zero),
                              jnp.where(first_head, zero, qp)], axis=0)
        if has_prev:
            kcat = jnp.concatenate([kp_ref[:, sl], kc_ref[:, sl]], axis=0)
            vcat = jnp.concatenate([vp_ref[:, sl], vc_ref[:, sl]], axis=0)
        else:
            kcat, vcat = kc_ref[:, sl], vc_ref[:, sl]
        s = lax.dot_general(q2, kcat, (((1,), (1,)), ((), ())),
                            preferred_element_type=jnp.float32)
        s = s + bias_ref[hp * 2 * QBLK:(hp + 1) * 2 * QBLK, :]
        if has_prev:
            s = jnp.where(key_ok, s, NEG_INF)
        m = jnp.max(s, axis=1, keepdims=True)
        p = jnp.exp(s - m)
        l = jnp.sum(p, axis=1, keepdims=True)
        o2 = jnp.dot(p.astype(jnp.bfloat16), vcat, preferred_element_type=jnp.float32)
        o2 = o2 / l
        o_ref[hp, rows, :] = jnp.where(first_head, o2[:QBLK], o2[QBLK:])
        lse = m + jnp.log(l)
        stats = jnp.where(lane == 2 * hp, lse[:QBLK], stats)
        stats = jnp.where(lane == 2 * hp + 1, lse[QBLK:], stats)
    lse_ref[rows, :] = stats


def _bucket_map(dilation, window, n_keys):
    W = window // dilation
    a = jnp.arange(QBLK, dtype=jnp.int32)[:, None]
    c = jnp.arange(n_keys, dtype=jnp.int32)[None, :]
    m = a + (n_keys - QBLK) - c
    band = (m >= 0) & (m <= W)
    n = jnp.maximum(m, 0) * dilation
    nf = jnp.maximum(n, 1).astype(jnp.float32)
    large = MAX_EXACT + (jnp.log(nf / MAX_EXACT) / math.log(MAX_DISTANCE / MAX_EXACT)
                         * (N_BUCKETS - MAX_EXACT)).astype(jnp.int32)
    large = jnp.minimum(large, N_BUCKETS - 1)
    bucket = jnp.where(n < MAX_EXACT, n, large)
    return jnp.where(band, bucket, -1)


def _attention_group(q, k, v, rel_bias, g, B, S):
    window, d = DILATED_PATTERNS[g]
    L = S // d
    n = L // QBLK
    D = D_MODEL
    has_prev = n > 1
    n_keys = 2 * QBLK if has_prev else QBLK
    table = rel_bias[:, g * HEADS_PER_GROUP:(g + 1) * HEADS_PER_GROUP].astype(jnp.float32)
    bmap = _bucket_map(d, window, n_keys)
    blk = (None, None, QBLK, D)
    cur = lambda b, j, r: (b, r, j, 0)
    prev = lambda b, j, r: (b, r, jnp.maximum(j - 1, 0), 0)
    if has_prev:
        kv_specs = [pl.BlockSpec(blk, prev), pl.BlockSpec(blk, cur),
                    pl.BlockSpec(blk, prev), pl.BlockSpec(blk, cur)]
        kv_args = (k, k, v, v)
    else:
        kv_specs = [pl.BlockSpec(blk, cur), pl.BlockSpec(blk, cur)]
        kv_args = (k, v)
    span = QBLK * d
    return pl.pallas_call(
        functools.partial(_attn_kernel, dilation=d, has_prev=has_prev),
        grid=(B, n, d),
        in_specs=[pl.BlockSpec(memory_space=pltpu.SMEM),
                  pl.BlockSpec(bmap.shape, lambda b, j, r: (0, 0)),
                  pl.BlockSpec(blk, cur)] + kv_specs,
        out_specs=[pl.BlockSpec((N_SLABS, span, LANES), lambda b, j, r: (0, b * n + j, 0)),
                   pl.BlockSpec((span, LANES), lambda b, j, r: (b * n + j, 0))],
        out_shape=[jax.ShapeDtypeStruct((N_SLABS, B * S, LANES), jnp.float32),
                   jax.ShapeDtypeStruct((B * S, LANES), jnp.float32)],
        scratch_shapes=[pltpu.VMEM((HEADS_PER_GROUP * QBLK, n_keys), jnp.float32)],
        compiler_params=pltpu.CompilerParams(
            dimension_semantics=("arbitrary",) * 3, vmem_limit_bytes=VMEM_LIMIT),
        name=f"dilated_attn_g{g}",
    )(table, bmap, q, *kv_args)


def _merge_kernel(o0_ref, o1_ref, o2_ref, s0_ref, s1_ref, s2_ref, h_ref, wo_ref, fn_ref,
                  rhi_ref, rlo_ref, expand_ref,
                  h_out_ref, hn_ref, route_ref, counts_ref, carry_ref):
    i = pl.program_id(0)
    tm = h_ref.shape[0]

    @pl.when(i == 0)
    def _():
        carry_ref[...] = jnp.zeros_like(carry_ref)

    lses = [s0_ref[...], s1_ref[...], s2_ref[...]]
    mx = jnp.maximum(jnp.maximum(lses[0], lses[1]), lses[2])
    es = [jnp.exp(v - mx) for v in lses]
    inv = 1.0 / (es[0] + es[1] + es[2])
    merged = None
    for e, o_ref in zip(es, (o0_ref, o1_ref, o2_ref)):
        hi, lo = _split_bf16(e * inv)
        wide = (jnp.dot(hi, expand_ref[...], preferred_element_type=jnp.float32)
                + jnp.dot(lo, expand_ref[...], preferred_element_type=jnp.float32))
        o = jnp.concatenate([o_ref[c] for c in range(N_SLABS)], axis=1)
        term = wide * o
        merged = term if merged is None else merged + term
    h = h_ref[...] + jnp.dot(merged.astype(jnp.bfloat16), wo_ref[...],
                             preferred_element_type=jnp.float32)
    h_out_ref[...] = h
    hn = h * _rms_scale(h) * fn_ref[...]
    hn_ref[...] = hn

    hi, lo = _split_bf16(hn)
    logits = (jnp.dot(hi, rhi_ref[...], preferred_element_type=jnp.float32)
              + jnp.dot(lo, rhi_ref[...], preferred_element_type=jnp.float32)
              + jnp.dot(hi, rlo_ref[...], preferred_element_type=jnp.float32))
    lane = lax.broadcasted_iota(jnp.int32, (tm, LANES), 1)
    logits = jnp.where(lane < N_EXPERTS, logits, -jnp.inf)
    v0 = jnp.max(logits, axis=1, keepdims=True)
    e0 = jnp.min(jnp.where(logits == v0, lane, LANES), axis=1, keepdims=True)
    rest = jnp.where(lane == e0, -jnp.inf, logits)
    v1 = jnp.max(rest, axis=1, keepdims=True)
    e1 = jnp.min(jnp.where(rest == v1, lane, LANES), axis=1, keepdims=True)
    t = jnp.exp(v1 - v0)
    g0 = 1.0 / (1.0 + t)
    g1 = t / (1.0 + t)

    hit0 = lane == e0
    hit1 = lane == e1
    onehot = jnp.where(jnp.logical_or(hit0, hit1), 1.0, 0.0)
    r_i = lax.broadcasted_iota(jnp.int32, (tm, tm), 0)
    c_i = lax.broadcasted_iota(jnp.int32, (tm, tm), 1)
    lower = jnp.where(c_i < r_i, 1.0, 0.0).astype(jnp.bfloat16)
    before = jnp.dot(lower, onehot.astype(jnp.bfloat16),
                     preferred_element_type=jnp.float32) + carry_ref[0:1, :]
    rank0 = jnp.sum(jnp.where(hit0, before, 0.0), axis=1, keepdims=True)
    rank1 = jnp.sum(jnp.where(hit1, before, 0.0), axis=1, keepdims=True)
    total = carry_ref[0:1, :] + jnp.sum(onehot, axis=0, keepdims=True)
    carry_ref[...] = jnp.broadcast_to(total, carry_ref.shape)
    counts_ref[...] = jnp.broadcast_to(total, counts_ref.shape)

    route = jnp.zeros((tm, LANES), jnp.float32)
    for ln, val in ((R_E0, e0.astype(jnp.float32)), (R_E1, e1.astype(jnp.float32)),
                    (R_RANK0, rank0), (R_RANK1, rank1), (R_G0, g0), (R_G1, g1)):
        route = jnp.where(lane == ln, val, route)
    route_ref[...] = route


def _merge_layer(outs, lses, h, w_o, ffn_norm, w_router):
    T, D = h.shape
    tm = MERGE_ROWS
    r_pad = jnp.zeros((D, LANES), jnp.float32).at[:, :N_EXPERTS].set(w_router)
    r_hi, r_lo = _split_bf16(r_pad)
    head_of_col = jnp.arange(D, dtype=jnp.int32) // HEAD_DIM
    expand = (jnp.arange(LANES, dtype=jnp.int32)[:, None] == head_of_col[None, :]
              ).astype(jnp.bfloat16)
    row = lambda i: (i, 0)
    const = lambda i: (0, 0)
    wide = pl.BlockSpec((tm, D), row)
    narrow = pl.BlockSpec((tm, LANES), row)
    slabs = pl.BlockSpec((N_SLABS, tm, LANES), lambda i: (0, i, 0))
    return pl.pallas_call(
        _merge_kernel,
        grid=(T // tm,),
        in_specs=[slabs, slabs, slabs, narrow, narrow, narrow, wide,
                  pl.BlockSpec((D, D), const), pl.BlockSpec((1, D), const),
                  pl.BlockSpec((D, LANES), const), pl.BlockSpec((D, LANES), const),
                  pl.BlockSpec((LANES, D), const)],
        out_specs=[wide, wide, narrow, pl.BlockSpec((8, LANES), const)],
        out_shape=[jax.ShapeDtypeStruct((T, D), jnp.float32),
                   jax.ShapeDtypeStruct((T, D), jnp.float32),
                   jax.ShapeDtypeStruct((T, LANES), jnp.float32),
                   jax.ShapeDtypeStruct((8, LANES), jnp.float32)],
        scratch_shapes=[pltpu.VMEM((8, LANES), jnp.float32)],
        compiler_params=pltpu.CompilerParams(
            dimension_semantics=("arbitrary",), vmem_limit_bytes=VMEM_LIMIT),
        name="merge_outproj_router",
    )(*outs, *lses, h, w_o.astype(jnp.bfloat16), ffn_norm.reshape(1, D), r_hi, r_lo, expand)


def _dispatch_kernel(dest_ref, hn_ref, xs_in_ref, xs_ref, sem):
    del xs_in_ref
    tile = hn_ref.shape[0]
    base = pl.program_id(0) * tile

    def start(t, c):
        for k in range(TOP_K):
            pltpu.make_async_copy(hn_ref.at[pl.ds(t, 1), :],
                                  xs_ref.at[pl.ds(dest_ref[TOP_K * (base + t) + k], 1), :],
                                  sem).start()
        return c

    lax.fori_loop(0, tile, start, 0, unroll=DMA_UNROLL)
    for k in range(TOP_K):
        pltpu.make_async_copy(hn_ref, xs_ref.at[pl.ds(0, tile), :], sem).wait()


def _dispatch(dest_flat, hn, n_slots):
    T, D = hn.shape
    xs0 = jnp.zeros((n_slots, D), hn.dtype)
    return pl.pallas_call(
        _dispatch_kernel,
        grid_spec=pltpu.PrefetchScalarGridSpec(
            num_scalar_prefetch=1,
            grid=(T // DISPATCH_ROWS,),
            in_specs=[pl.BlockSpec((DISPATCH_ROWS, D), lambda i, d: (i, 0)),
                      pl.BlockSpec(memory_space=pl.ANY)],
            out_specs=pl.BlockSpec(memory_space=pl.ANY),
            scratch_shapes=[pltpu.SemaphoreType.DMA(())],
        ),
        out_shape=jax.ShapeDtypeStruct((n_slots, D), hn.dtype),
        input_output_aliases={2: 0},
        compiler_params=pltpu.CompilerParams(
            dimension_semantics=("arbitrary",), vmem_limit_bytes=VMEM_LIMIT),
        name="moe_dispatch",
    )(dest_flat, hn, xs0)


def _expert_kernel(tile_e_ref, used_ref, x_ref, wg_ref, wu_ref, wd_ref, y_ref, xb_ref, acc_ref):
    del tile_e_ref
    i = pl.program_id(0)
    f = pl.program_id(1)
    active = i < used_ref[0]

    @pl.when(jnp.logical_and(active, f == 0))
    def _():
        xb_ref[...] = x_ref[...].astype(xb_ref.dtype)

    @pl.when(active)
    def _():
        x = xb_ref[...]
        g = jnp.dot(x, wg_ref[...], preferred_element_type=jnp.float32)
        u = jnp.dot(x, wu_ref[...], preferred_element_type=jnp.float32)
        hid = (_silu(g) * u).astype(jnp.bfloat16)
        part = jnp.dot(hid, wd_ref[...], preferred_element_type=jnp.float32)

        @pl.when(f == 0)
        def _():
            acc_ref[...] = part

        @pl.when(f > 0)
        def _():
            acc_ref[...] += part

        @pl.when(f == pl.num_programs(1) - 1)
        def _():
            y_ref[...] = acc_ref[...]

    @pl.when(jnp.logical_and(jnp.logical_not(active), f == pl.num_programs(1) - 1))
    def _():
        y_ref[...] = jnp.zeros_like(y_ref)


def _expert_ffn(tile_e, used, xs, w_gate, w_up, w_down):
    n_slots, D = xs.shape
    F = w_gate.shape[2]
    tm, tf = MOE_ROWS, MOE_FT
    n_f = F // tf
    n_tiles = n_slots // tm

    def live(i, used_ref):
        return jnp.minimum(i, used_ref[0] - 1)

    def f_eff(i, f, used_ref):
        return jnp.where(i < used_ref[0], f, n_f - 1)

    return pl.pallas_call(
        _expert_kernel,
        grid_spec=pltpu.PrefetchScalarGridSpec(
            num_scalar_prefetch=2,
            grid=(n_tiles, n_f),
            in_specs=[
                pl.BlockSpec((tm, D), lambda i, f, te, us: (live(i, us), 0)),
                pl.BlockSpec((None, D, tf), lambda i, f, te, us: (te[live(i, us)], 0, f_eff(i, f, us))),
                pl.BlockSpec((None, D, tf), lambda i, f, te, us: (te[live(i, us)], 0, f_eff(i, f, us))),
                pl.BlockSpec((None, tf, D), lambda i, f, te, us: (te[live(i, us)], f_eff(i, f, us), 0)),
            ],
            out_specs=pl.BlockSpec((tm, D), lambda i, f, te, us: (i, 0)),
            scratch_shapes=[pltpu.VMEM((tm, D), jnp.bfloat16), pltpu.VMEM((tm, D), jnp.float32)],
        ),
        out_shape=jax.ShapeDtypeStruct((n_slots, D), jnp.float32),
        compiler_params=pltpu.CompilerParams(
            dimension_semantics=("arbitrary", "arbitrary"), vmem_limit_bytes=VMEM_LIMIT),
        name="moe_experts",
    )(tile_e, used, xs, w_gate.astype(jnp.bfloat16), w_up.astype(jnp.bfloat16),
      w_down.astype(jnp.bfloat16))


def _combine_kernel(dest_ref, h_ref, route_ref, fin_ref, ys_ref, o_ref, ybuf, sems):
    i = pl.program_id(0)
    n_steps = pl.num_programs(0)
    tile = h_ref.shape[0]

    def issue(step, slot):
        base = step * tile

        def start(t, c):
            for k in range(TOP_K):
                pltpu.make_async_copy(
                    ys_ref.at[pl.ds(dest_ref[TOP_K * (base + t) + k], 1), :],
                    ybuf.at[slot, k, pl.ds(t, 1), :], sems.at[slot]).start()
            return c

        lax.fori_loop(0, tile, start, 0, unroll=DMA_UNROLL)

    slot = lax.rem(i, 2)

    @pl.when(i == 0)
    def _():
        issue(0, 0)

    @pl.when(i + 1 < n_steps)
    def _():
        issue(i + 1, 1 - slot)

    for k in range(TOP_K):
        pltpu.make_async_copy(ys_ref.at[pl.ds(0, tile), :], ybuf.at[slot, k], sems.at[slot]).wait()
    route = route_ref[...]
    g0 = route[:, R_G0:R_G0 + 1]
    g1 = route[:, R_G1:R_G1 + 1]
    h = h_ref[...] + (g0 * ybuf[slot, 0] + g1 * ybuf[slot, 1])
    o_ref[...] = h * _rms_scale(h) * fin_ref[...]


def _combine(dest_flat, h, route, final_norm, ys):
    T, D = h.shape
    tc = COMBINE_ROWS
    return pl.pallas_call(
        _combine_kernel,
        grid_spec=pltpu.PrefetchScalarGridSpec(
            num_scalar_prefetch=1,
            grid=(T // tc,),
            in_specs=[pl.BlockSpec((tc, D), lambda i, d: (i, 0)),
                      pl.BlockSpec((tc, LANES), lambda i, d: (i, 0)),
                      pl.BlockSpec((1, D), lambda i, d: (0, 0)),
                      pl.BlockSpec(memory_space=pl.ANY)],
            out_specs=pl.BlockSpec((tc, D), lambda i, d: (i, 0)),
            scratch_shapes=[pltpu.VMEM((2, TOP_K, tc, D), jnp.float32),
                            pltpu.SemaphoreType.DMA((2,))],
        ),
        out_shape=jax.ShapeDtypeStruct((T, D), jnp.float32),
        compiler_params=pltpu.CompilerParams(
            dimension_semantics=("arbitrary",), vmem_limit_bytes=VMEM_LIMIT),
        name="moe_combine_final_norm",
    )(dest_flat, h, route, final_norm.reshape(1, D), ys)


def _routing_tables(route, counts_row):
    T = route.shape[0]
    counts = counts_row[0, :N_EXPERTS].astype(jnp.int32)
    padded = ((counts + MOE_ROWS - 1) // MOE_ROWS) * MOE_ROWS
    ends = jnp.cumsum(padded)
    starts = ends - padded
    expert = route[:, R_E0:R_E1 + 1].astype(jnp.int32)
    rank = route[:, R_RANK0:R_RANK1 + 1].astype(jnp.int32)
    onehot = expert[..., None] == jnp.arange(N_EXPERTS, dtype=jnp.int32)
    dest = jnp.sum(jnp.where(onehot, starts, 0), axis=-1) + rank
    n_tiles = (TOP_K * T) // MOE_ROWS + N_EXPERTS
    tile_start = jnp.arange(n_tiles, dtype=jnp.int32) * MOE_ROWS
    tile_e = jnp.minimum(jnp.sum(ends[None, :] <= tile_start[:, None], axis=1),
                         N_EXPERTS - 1).astype(jnp.int32)
    used = (ends[-1:] // MOE_ROWS).astype(jnp.int32)
    return dest.reshape(-1), tile_e, used, n_tiles * MOE_ROWS


def kernel(x, a_norm, a_proj, a_scale, kv_norm, w_kv, b_norm, w_q, w_o, rel_bias, ffn_norm,
           dense_w_gate, dense_w_up, dense_w_down, moe_router, moe_w_gate, moe_w_up,
           moe_w_down, final_norm):
    B, S, D = x.shape
    T = B * S
    h1, hn1 = _pool_layer(x, a_norm[0], a_proj[0], a_scale[0], ffn_norm[0])
    h2 = _dense_layer(hn1.reshape(T, D), h1.reshape(T, D), dense_w_gate[0],
                      dense_w_up[0], dense_w_down[0])
    outs, lses = [], []
    for g in range(N_ATT_GROUPS):
        q, k, v = _qkv_project(h2, kv_norm, b_norm[0], w_q[0], w_kv, g, B, S)
        o, lse = _attention_group(q, k, v, rel_bias, g, B, S)
        outs.append(o)
        lses.append(lse)
    h3, hn3, route, counts = _merge_layer(outs, lses, h2, w_o[0], ffn_norm[1], moe_router[0])
    dest, tile_e, used, n_slots = _routing_tables(route, counts)
    xs = _dispatch(dest, hn3, n_slots)
    ys = _expert_ffn(tile_e, used, xs, moe_w_gate[0], moe_w_up[0], moe_w_down[0])
    out = _combine(dest, h3, route, final_norm, ys)
    return out.reshape(B, S, D)
```

```python
import functools
import math

import jax
import jax.numpy as jnp
from jax import lax
from jax.experimental import pallas as pl
from jax.experimental.pallas import tpu as pltpu

D_MODEL = 1024
EPS = 1e-6
POOL_WINDOWS = (2, 4, 8, 16)
POOL_GROUP_DIM = D_MODEL // len(POOL_WINDOWS)
MAX_POOL_WINDOW = max(POOL_WINDOWS)
HEAD_DIM = 64
HEADS_PER_GROUP = D_MODEL // HEAD_DIM
DILATED_PATTERNS = ((128, 1), (512, 4), (2048, 16))
N_ATT_GROUPS = len(DILATED_PATTERNS)
QBLK = 128
NEG_INF = -1e30
N_BUCKETS = 32
MAX_EXACT = N_BUCKETS // 2
MAX_DISTANCE = 2048
N_EXPERTS = 8
TOP_K = 2

LANES = 128
SUBLANES = 8
N_SLABS = D_MODEL // LANES
HEAD_PAIR = 2 * HEAD_DIM
assert HEAD_PAIR == LANES

POOL_ROWS = 512
DENSE_ROWS = 512
QKV_ROWS = 512
MERGE_ROWS = 512
MOE_ROWS = 1024
MOE_FT = 512
MOE_NT = 256
DISPATCH_ROWS = 1024
COMBINE_ROWS = 512
DMA_UNROLL = 8
VMEM_LIMIT = 56 * 1024 * 1024

R_E0, R_E1, R_RANK0, R_RANK1, R_G0, R_G1 = 0, 1, 2, 3, 4, 5


def _rms_scale(x):
    return lax.rsqrt(jnp.mean(x * x, axis=-1, keepdims=True) + EPS)


def _silu(g):
    return g * (1.0 / (1.0 + jnp.exp(-g)))


def _split_bf16(x):
    hi = x.astype(jnp.bfloat16)
    lo = (x - hi.astype(jnp.float32)).astype(jnp.bfloat16)
    return hi, lo


def _pool_kernel(x_ref, halo_ref, an_ref, proj_ref, asc_ref, fn_ref, h_ref, hn_ref):
    i = pl.program_id(1)
    x = x_ref[...]
    halo = halo_ref[...]
    gain = an_ref[...]
    xn = x * _rms_scale(x) * gain
    hn_halo = halo * _rms_scale(halo) * gain
    hn_halo = jnp.where(i > 0, hn_halo, 0.0)
    full = jnp.concatenate([hn_halo, xn], axis=0)
    ts = x.shape[0]
    pos = i * ts + lax.broadcasted_iota(jnp.int32, (ts, 1), 0)
    outs = []
    for g, w in enumerate(POOL_WINDOWS):
        c0 = g * POOL_GROUP_DIM
        s = full[:, c0:c0 + POOL_GROUP_DIM]
        span = 1
        while span < w:
            s = s + pltpu.roll(s, span, axis=0)
            span *= 2
        s = s[MAX_POOL_WINDOW:, :]
        cnt = jnp.minimum(pos + 1, w).astype(jnp.float32)
        pooled = s / cnt - xn[:, c0:c0 + POOL_GROUP_DIM]
        outs.append(jnp.dot(pooled.astype(jnp.bfloat16), proj_ref[g],
                            preferred_element_type=jnp.float32))
    mix = jnp.concatenate(outs, axis=1) * asc_ref[...]
    h = x + mix
    h_ref[...] = h
    hn_ref[...] = (h * _rms_scale(h) * fn_ref[...]).astype(hn_ref.dtype)


def _pool_layer(x, a_norm, a_proj, a_scale, ffn_norm):
    B, S, D = x.shape
    ts = POOL_ROWS
    halo_blocks = ts // MAX_POOL_WINDOW
    vec = pl.BlockSpec((1, D), lambda b, i: (0, 0))
    return pl.pallas_call(
        _pool_kernel,
        grid=(B, S // ts),
        in_specs=[
            pl.BlockSpec((None, ts, D), lambda b, i: (b, i, 0)),
            pl.BlockSpec((None, MAX_POOL_WINDOW, D),
                         lambda b, i: (b, jnp.maximum(i * halo_blocks - 1, 0), 0)),
            vec,
            pl.BlockSpec(a_proj.shape, lambda b, i: (0, 0, 0)),
            vec, vec,
        ],
        out_specs=[pl.BlockSpec((None, ts, D), lambda b, i: (b, i, 0)),
                   pl.BlockSpec((None, ts, D), lambda b, i: (b, i, 0))],
        out_shape=[jax.ShapeDtypeStruct((B, S, D), jnp.float32),
                   jax.ShapeDtypeStruct((B, S, D), jnp.bfloat16)],
        compiler_params=pltpu.CompilerParams(
            dimension_semantics=("arbitrary", "arbitrary"), vmem_limit_bytes=VMEM_LIMIT),
        name="pool_mixer",
    )(x, x, a_norm.reshape(1, D), a_proj.astype(jnp.bfloat16), a_scale.reshape(1, D),
      ffn_norm.reshape(1, D))


def _dense_kernel(hn_ref, h_ref, wg_ref, wu_ref, wd_ref, h_out_ref):
    x = hn_ref[...]
    g = jnp.dot(x, wg_ref[...], preferred_element_type=jnp.float32)
    u = jnp.dot(x, wu_ref[...], preferred_element_type=jnp.float32)
    hid = (_silu(g) * u).astype(jnp.bfloat16)
    h_out_ref[...] = h_ref[...] + jnp.dot(hid, wd_ref[...], preferred_element_type=jnp.float32)


def _dense_layer(hn, h, w_gate, w_up, w_down):
    T, D = h.shape
    F = w_gate.shape[1]
    tm = DENSE_ROWS
    row = lambda i: (i, 0)
    resident = dict(index_map=lambda i: (0, 0), pipeline_mode=pl.Buffered(1))
    return pl.pallas_call(
        _dense_kernel,
        grid=(T // tm,),
        in_specs=[
            pl.BlockSpec((tm, D), row),
            pl.BlockSpec((tm, D), row),
            pl.BlockSpec((D, F), **resident),
            pl.BlockSpec((D, F), **resident),
            pl.BlockSpec((F, D), **resident),
        ],
        out_specs=pl.BlockSpec((tm, D), row),
        out_shape=jax.ShapeDtypeStruct((T, D), jnp.float32),
        compiler_params=pltpu.CompilerParams(
            dimension_semantics=("arbitrary",), vmem_limit_bytes=VMEM_LIMIT),
        name="dense_swiglu",
    )(hn, h, w_gate.astype(jnp.bfloat16), w_up.astype(jnp.bfloat16),
      w_down.astype(jnp.bfloat16))


def _qkv_kernel(*refs, dilation):
    slabs = refs[:N_SLABS]
    kvg_ref, qg_ref, w_ref, q_ref, k_ref, v_ref = refs[N_SLABS:]
    tm = slabs[0].shape[0]
    n = tm // dilation
    cols = []
    for slab in slabs:
        if dilation == 1:
            cols.append(slab[...])
        else:
            cols.append(jnp.concatenate(
                [slab[pl.ds(r, n, stride=dilation), :] for r in range(dilation)], axis=0))
    x = jnp.concatenate(cols, axis=1)
    xs = x * _rms_scale(x)
    xq = (xs * qg_ref[...]).astype(jnp.bfloat16)
    xkv = (xs * kvg_ref[...]).astype(jnp.bfloat16)
    q = jnp.dot(xq, w_ref[0], preferred_element_type=jnp.float32) * (HEAD_DIM ** -0.5)
    k = jnp.dot(xkv, w_ref[1], preferred_element_type=jnp.float32)
    v = jnp.dot(xkv, w_ref[2], preferred_element_type=jnp.float32)
    for out_ref, val in ((q_ref, q), (k_ref, k), (v_ref, v)):
        for r in range(dilation):
            out_ref[r] = val[r * n:(r + 1) * n].astype(out_ref.dtype)


def _qkv_project(h, kv_norm, b_norm, w_q, w_kv, g, B, S):
    T, D = h.shape
    d = DILATED_PATTERNS[g][1]
    L = S // d
    tm = QKV_ROWS
    n = tm // d
    tiles_per_seq = S // tm
    w = jnp.stack([w_q[:, g * D:(g + 1) * D],
                   w_kv[:, g * D:(g + 1) * D],
                   w_kv[:, (N_ATT_GROUPS + g) * D:(N_ATT_GROUPS + g + 1) * D]]).astype(jnp.bfloat16)
    slab_specs = [pl.BlockSpec((tm, LANES), functools.partial(lambda i, c: (i, c), c=c))
                  for c in range(N_SLABS)]
    vec = pl.BlockSpec((1, D), lambda i: (0, 0))
    out_spec = pl.BlockSpec((None, d, n, D),
                            lambda i: (i // tiles_per_seq, 0, i % tiles_per_seq, 0))
    out_shape = jax.ShapeDtypeStruct((B, d, L, D), jnp.bfloat16)
    return pl.pallas_call(
        functools.partial(_qkv_kernel, dilation=d),
        grid=(T // tm,),
        in_specs=slab_specs + [vec, vec, pl.BlockSpec((3, D, D), lambda i: (0, 0, 0))],
        out_specs=[out_spec] * 3,
        out_shape=[out_shape] * 3,
        compiler_params=pltpu.CompilerParams(
            dimension_semantics=("arbitrary",), vmem_limit_bytes=VMEM_LIMIT),
        name=f"qkv_proj_g{g}",
    )(*([h] * N_SLABS), kv_norm.reshape(1, D), b_norm.reshape(1, D), w)


def _attn_kernel(*refs, dilation, has_prev):
    if has_prev:
        (table_ref, bmap_ref, q_ref, kp_ref, kc_ref, vp_ref, vc_ref,
         o_ref, lse_ref, bias_ref) = refs
    else:
        table_ref, bmap_ref, q_ref, kc_ref, vc_ref, o_ref, lse_ref, bias_ref = refs
    b, j, r = pl.program_id(0), pl.program_id(1), pl.program_id(2)
    n_keys = bias_ref.shape[2]

    @pl.when(jnp.logical_and(jnp.logical_and(b == 0, j == 0), r == 0))
    def _():
        bmap = bmap_ref[...]
        in_prev = lax.broadcasted_iota(jnp.int32, bmap.shape, 1) < n_keys - QBLK

        def head(h, carry):
            acc = jnp.full(bmap.shape, NEG_INF, jnp.float32)
            for bucket in range(N_BUCKETS):
                acc = jnp.where(bmap == bucket, table_ref[bucket, h], acc)
            rows_h = pl.ds(pl.multiple_of(h * QBLK, QBLK), QBLK)
            bias_ref[0, rows_h, :] = jnp.where(in_prev, NEG_INF, acc)
            bias_ref[1, rows_h, :] = acc
            return carry

        lax.fori_loop(0, HEADS_PER_GROUP, head, 0)

    lane = lax.broadcasted_iota(jnp.int32, (QBLK, LANES), 1)
    first_head = lane < HEAD_DIM
    bias_copy = jnp.minimum(j, 1)
    rows = pl.ds(r, QBLK, stride=dilation) if dilation > 1 else slice(None)
    stats = jnp.zeros((QBLK, LANES), jnp.float32)
    for hp in range(HEADS_PER_GROUP // 2):
        sl = slice(hp * HEAD_PAIR, (hp + 1) * HEAD_PAIR)
        qp = q_ref[:, sl]
        zero = jnp.zeros_like(qp)
        q2 = jnp.concatenate([jnp.where(first_head, qp, zero),
                              jnp.where(first_head, zero, qp)], axis=0)
        if has_prev:
            kcat = jnp.concatenate([kp_ref[:, sl], kc_ref[:, sl]], axis=0)
            vcat = jnp.concatenate([vp_ref[:, sl], vc_ref[:, sl]], axis=0)
        else:
            kcat, vcat = kc_ref[:, sl], vc_ref[:, sl]
        s = lax.dot_general(q2, kcat, (((1,), (1,)), ((), ())),
                            preferred_element_type=jnp.float32)
        s = s + bias_ref[bias_copy, hp * 2 * QBLK:(hp + 1) * 2 * QBLK, :]
        m = jnp.max(s, axis=1, keepdims=True)
        p = jnp.exp(s - m)
        l = jnp.sum(p, axis=1, keepdims=True)
        o2 = jnp.dot(p.astype(jnp.bfloat16), vcat, preferred_element_type=jnp.float32)
        o2 = o2 / l
        o_ref[hp, rows, :] = jnp.where(first_head, o2[:QBLK], o2[QBLK:])
        lse = m + jnp.log(l)
        stats = jnp.where((lane & (HEADS_PER_GROUP - 1)) == 2 * hp, lse[:QBLK], stats)
        stats = jnp.where((lane & (HEADS_PER_GROUP - 1)) == 2 * hp + 1, lse[QBLK:], stats)
    lse_ref[rows, :] = stats


def _bucket_map(dilation, window, n_keys):
    W = window // dilation
    a = jnp.arange(QBLK, dtype=jnp.int32)[:, None]
    c = jnp.arange(n_keys, dtype=jnp.int32)[None, :]
    m = a + (n_keys - QBLK) - c
    band = (m >= 0) & (m <= W)
    n = jnp.maximum(m, 0) * dilation
    nf = jnp.maximum(n, 1).astype(jnp.float32)
    large = MAX_EXACT + (jnp.log(nf / MAX_EXACT) / math.log(MAX_DISTANCE / MAX_EXACT)
                         * (N_BUCKETS - MAX_EXACT)).astype(jnp.int32)
    large = jnp.minimum(large, N_BUCKETS - 1)
    bucket = jnp.where(n < MAX_EXACT, n, large)
    return jnp.where(band, bucket, -1)


def _attention_group(q, k, v, rel_bias, g, B, S):
    window, d = DILATED_PATTERNS[g]
    L = S // d
    n = L // QBLK
    D = D_MODEL
    has_prev = n > 1
    n_keys = 2 * QBLK if has_prev else QBLK
    table = rel_bias[:, g * HEADS_PER_GROUP:(g + 1) * HEADS_PER_GROUP].astype(jnp.float32)
    bmap = _bucket_map(d, window, n_keys)
    blk = (None, None, QBLK, D)
    cur = lambda b, j, r: (b, r, j, 0)
    prev = lambda b, j, r: (b, r, jnp.maximum(j - 1, 0), 0)
    if has_prev:
        kv_specs = [pl.BlockSpec(blk, prev), pl.BlockSpec(blk, cur),
                    pl.BlockSpec(blk, prev), pl.BlockSpec(blk, cur)]
        kv_args = (k, k, v, v)
    else:
        kv_specs = [pl.BlockSpec(blk, cur), pl.BlockSpec(blk, cur)]
        kv_args = (k, v)
    span = QBLK * d
    return pl.pallas_call(
        functools.partial(_attn_kernel, dilation=d, has_prev=has_prev),
        grid=(B, n, d),
        in_specs=[pl.BlockSpec(memory_space=pltpu.SMEM),
                  pl.BlockSpec(bmap.shape, lambda b, j, r: (0, 0)),
                  pl.BlockSpec(blk, cur)] + kv_specs,
        out_specs=[pl.BlockSpec((N_SLABS, span, LANES), lambda b, j, r: (0, b * n + j, 0)),
                   pl.BlockSpec((span, LANES), lambda b, j, r: (b * n + j, 0))],
        out_shape=[jax.ShapeDtypeStruct((N_SLABS, B * S, LANES), jnp.float32),
                   jax.ShapeDtypeStruct((B * S, LANES), jnp.float32)],
        scratch_shapes=[pltpu.VMEM((2, HEADS_PER_GROUP * QBLK, n_keys), jnp.float32)],
        compiler_params=pltpu.CompilerParams(
            dimension_semantics=("arbitrary",) * 3, vmem_limit_bytes=VMEM_LIMIT),
        name=f"dilated_attn_g{g}",
    )(table, bmap, q, *kv_args)


def _merge_kernel(o0_ref, o1_ref, o2_ref, s0_ref, s1_ref, s2_ref, h_ref, wo_ref, fn_ref,
                  r2_ref, expand_ref,
                  h_out_ref, hn_ref, route_ref, counts_ref, carry_ref):
    i = pl.program_id(0)
    tm = h_ref.shape[0]

    @pl.when(i == 0)
    def _():
        carry_ref[...] = jnp.zeros_like(carry_ref)

    lane = lax.broadcasted_iota(jnp.int32, (tm, LANES), 1)
    lses = [s0_ref[...], s1_ref[...], s2_ref[...]]
    mx = jnp.maximum(jnp.maximum(lses[0], lses[1]), lses[2])
    es = [jnp.exp(v - mx) for v in lses]
    inv = 1.0 / (es[0] + es[1] + es[2])
    merged = None
    for e, o_ref in zip(es, (o0_ref, o1_ref, o2_ref)):
        w = e * inv
        hi = w.astype(jnp.bfloat16).astype(jnp.float32)
        packed = jnp.where(lane < HEADS_PER_GROUP, hi, w - hi).astype(jnp.bfloat16)
        wide = jnp.dot(packed, expand_ref[...], preferred_element_type=jnp.float32)
        o = jnp.concatenate([o_ref[c] for c in range(N_SLABS)], axis=1)
        term = wide * o
        merged = term if merged is None else merged + term
    h = h_ref[...] + jnp.dot(merged.astype(jnp.bfloat16), wo_ref[...],
                             preferred_element_type=jnp.float32)
    h_out_ref[...] = h
    hn = h * _rms_scale(h) * fn_ref[...]
    hn_ref[...] = hn

    hi, lo = _split_bf16(hn)
    both = jnp.dot(hi, r2_ref[...], preferred_element_type=jnp.float32)
    logits = (both[:, :LANES] + both[:, LANES:]
              + jnp.dot(lo, r2_ref[:, :LANES], preferred_element_type=jnp.float32))
    logits = jnp.where(lane < N_EXPERTS, logits, -jnp.inf)
    v0 = jnp.max(logits, axis=1, keepdims=True)
    e0 = jnp.min(jnp.where(logits == v0, lane, LANES), axis=1, keepdims=True)
    rest = jnp.where(lane == e0, -jnp.inf, logits)
    v1 = jnp.max(rest, axis=1, keepdims=True)
    e1 = jnp.min(jnp.where(rest == v1, lane, LANES), axis=1, keepdims=True)
    t = jnp.exp(v1 - v0)
    g0 = 1.0 / (1.0 + t)
    g1 = t / (1.0 + t)

    hit0 = lane == e0
    hit1 = lane == e1
    onehot = jnp.where(jnp.logical_or(hit0, hit1), 1.0, 0.0)
    r_i = lax.broadcasted_iota(jnp.int32, (tm, tm), 0)
    c_i = lax.broadcasted_iota(jnp.int32, (tm, tm), 1)
    lower = jnp.where(c_i < r_i, 1.0, 0.0).astype(jnp.bfloat16)
    before = jnp.dot(lower, onehot.astype(jnp.bfloat16),
                     preferred_element_type=jnp.float32) + carry_ref[0:1, :]
    rank0 = jnp.sum(jnp.where(hit0, before, 0.0), axis=1, keepdims=True)
    rank1 = jnp.sum(jnp.where(hit1, before, 0.0), axis=1, keepdims=True)
    total = carry_ref[0:1, :] + jnp.sum(onehot, axis=0, keepdims=True)
    carry_ref[...] = jnp.broadcast_to(total, carry_ref.shape)
    counts_ref[...] = jnp.broadcast_to(total, counts_ref.shape)

    route = jnp.zeros((tm, LANES), jnp.float32)
    for ln, val in ((R_E0, e0.astype(jnp.float32)), (R_E1, e1.astype(jnp.float32)),
                    (R_RANK0, rank0), (R_RANK1, rank1), (R_G0, g0), (R_G1, g1)):
        route = jnp.where(lane == ln, val, route)
    route_ref[...] = route


def _merge_layer(outs, lses, h, w_o, ffn_norm, w_router):
    T, D = h.shape
    tm = MERGE_ROWS
    r_pad = jnp.zeros((D, LANES), jnp.float32).at[:, :N_EXPERTS].set(w_router)
    r2 = jnp.concatenate(_split_bf16(r_pad), axis=1)
    head_of_col = jnp.arange(D, dtype=jnp.int32) // HEAD_DIM
    lane_id = jnp.arange(LANES, dtype=jnp.int32)[:, None]
    expand = ((lane_id < 2 * HEADS_PER_GROUP)
              & (lane_id % HEADS_PER_GROUP == head_of_col[None, :])).astype(jnp.bfloat16)
    row = lambda i: (i, 0)
    const = lambda i: (0, 0)
    wide = pl.BlockSpec((tm, D), row)
    narrow = pl.BlockSpec((tm, LANES), row)
    slabs = pl.BlockSpec((N_SLABS, tm, LANES), lambda i: (0, i, 0))
    return pl.pallas_call(
        _merge_kernel,
        grid=(T // tm,),
        in_specs=[slabs, slabs, slabs, narrow, narrow, narrow, wide,
                  pl.BlockSpec((D, D), const), pl.BlockSpec((1, D), const),
                  pl.BlockSpec((D, 2 * LANES), const), pl.BlockSpec((LANES, D), const)],
        out_specs=[wide, wide, narrow, pl.BlockSpec((8, LANES), const)],
        out_shape=[jax.ShapeDtypeStruct((T, D), jnp.float32),
                   jax.ShapeDtypeStruct((T, D), jnp.float32),
                   jax.ShapeDtypeStruct((T, LANES), jnp.float32),
                   jax.ShapeDtypeStruct((8, LANES), jnp.float32)],
        scratch_shapes=[pltpu.VMEM((8, LANES), jnp.float32)],
        compiler_params=pltpu.CompilerParams(
            dimension_semantics=("arbitrary",), vmem_limit_bytes=VMEM_LIMIT),
        name="merge_outproj_router",
    )(*outs, *lses, h, w_o.astype(jnp.bfloat16), ffn_norm.reshape(1, D), r2, expand)


def _dispatch_kernel(dest_ref, pad_start_ref, pad_len_ref, used_ref, hn_ref, xs_ref,
                     zbuf, sem, zsem):
    i = pl.program_id(0)
    tile = hn_ref.shape[0]
    base = i * tile
    n_tiles = xs_ref.shape[0] // MOE_ROWS

    def zero_fill(act):
        for e in range(N_EXPERTS):
            run_start = pad_start_ref[e]
            head = (-run_start) & (SUBLANES - 1)
            for row in range(SUBLANES - 1):
                @pl.when(row < head)
                def _():
                    act(pltpu.make_async_copy(zbuf.at[pl.ds(0, 1), :],
                                              xs_ref.at[pl.ds(run_start + row, 1), :], zsem))
            body_start = run_start + head
            body_len = pad_len_ref[e] - head
            bit = MOE_ROWS // 2
            while bit >= SUBLANES:
                off = pl.multiple_of(body_start + (body_len & -(2 * bit)), SUBLANES)

                @pl.when((body_len & bit) != 0)
                def _():
                    act(pltpu.make_async_copy(zbuf.at[pl.ds(0, bit), :],
                                              xs_ref.at[pl.ds(off, bit), :], zsem))
                bit //= 2
        for t in range(n_tiles):
            @pl.when(t >= used_ref[0])
            def _():
                act(pltpu.make_async_copy(zbuf, xs_ref.at[pl.ds(t * MOE_ROWS, MOE_ROWS), :], zsem))

    @pl.when(i == 0)
    def _():
        zbuf[...] = jnp.zeros_like(zbuf)
        zero_fill(lambda c: c.start())

    def start(t, c):
        for k in range(TOP_K):
            pltpu.make_async_copy(hn_ref.at[pl.ds(t, 1), :],
                                  xs_ref.at[pl.ds(dest_ref[TOP_K * (base + t) + k], 1), :],
                                  sem).start()
        return c

    lax.fori_loop(0, tile, start, 0, unroll=DMA_UNROLL)
    for k in range(TOP_K):
        pltpu.make_async_copy(hn_ref, xs_ref.at[pl.ds(0, tile), :], sem).wait()

    @pl.when(i == 0)
    def _():
        zero_fill(lambda c: c.wait())


def _dispatch(dest_flat, pad_start, pad_len, used, hn, n_slots):
    T, D = hn.shape
    return pl.pallas_call(
        _dispatch_kernel,
        grid_spec=pltpu.PrefetchScalarGridSpec(
            num_scalar_prefetch=4,
            grid=(T // DISPATCH_ROWS,),
            in_specs=[pl.BlockSpec((DISPATCH_ROWS, D), lambda i, *_: (i, 0))],
            out_specs=pl.BlockSpec(memory_space=pl.ANY),
            scratch_shapes=[pltpu.VMEM((MOE_ROWS, D), hn.dtype),
                            pltpu.SemaphoreType.DMA(()), pltpu.SemaphoreType.DMA(())],
        ),
        out_shape=jax.ShapeDtypeStruct((n_slots, D), hn.dtype),
        compiler_params=pltpu.CompilerParams(
            dimension_semantics=("arbitrary",), vmem_limit_bytes=VMEM_LIMIT),
        name="moe_dispatch",
    )(dest_flat, pad_start, pad_len, used, hn)


def _expert_kernel(tile_e_ref, used_ref, x_ref, wg_ref, wu_ref, wd_ref, y_ref, xb_ref, hid_ref):
    del tile_e_ref
    i = pl.program_id(0)
    s = pl.program_id(1)
    n_f = hid_ref.shape[0]
    active = i < used_ref[0]

    @pl.when(jnp.logical_and(active, s == 0))
    def _():
        xb_ref[...] = x_ref[...].astype(xb_ref.dtype)

    @pl.when(jnp.logical_and(active, s < n_f))
    def _():
        x = xb_ref[...]
        g = jnp.dot(x, wg_ref[...].astype(jnp.bfloat16), preferred_element_type=jnp.float32)
        u = jnp.dot(x, wu_ref[...].astype(jnp.bfloat16), preferred_element_type=jnp.float32)
        hid_ref[s] = (_silu(g) * u).astype(hid_ref.dtype)

    @pl.when(jnp.logical_and(active, s >= n_f))
    def _():
        hid = jnp.concatenate([hid_ref[c] for c in range(n_f)], axis=1)
        y_ref[...] = jnp.dot(hid, wd_ref[...].astype(jnp.bfloat16),
                             preferred_element_type=jnp.float32)

    @pl.when(jnp.logical_and(jnp.logical_not(active), s >= n_f))
    def _():
        y_ref[...] = jnp.zeros_like(y_ref)


def _expert_ffn(tile_e, used, xs, w_gate, w_up, w_down):
    n_slots, D = xs.shape
    F = w_gate.shape[2]
    tm, tf, tn = MOE_ROWS, MOE_FT, MOE_NT
    n_f, n_n = F // tf, D // tn
    n_tiles = n_slots // tm
    last = n_f + n_n - 1

    def eff(i, s, used_ref):
        idle = i >= used_ref[0]
        return jnp.where(idle, used_ref[0] - 1, i), jnp.where(idle, last, s)

    def x_map(i, s, te, us):
        ie, se = eff(i, s, us)
        return jnp.minimum(ie + jnp.minimum(se, 1), us[0] - 1), 0

    def gate_map(i, s, te, us):
        ie, se = eff(i, s, us)
        return te[ie], 0, jnp.minimum(se, n_f - 1)

    def down_map(i, s, te, us):
        ie, se = eff(i, s, us)
        tile = jnp.where(se == 0, jnp.maximum(ie - 1, 0), ie)
        return te[tile], 0, jnp.where(se == 0, n_n - 1, jnp.clip(se - n_f, 0, n_n - 1))

    def out_map(i, s, te, us):
        return i, jnp.clip(s - n_f, 0, n_n - 1)

    return pl.pallas_call(
        _expert_kernel,
        grid_spec=pltpu.PrefetchScalarGridSpec(
            num_scalar_prefetch=2,
            grid=(n_tiles, n_f + n_n),
            in_specs=[
                pl.BlockSpec((tm, D), x_map),
                pl.BlockSpec((None, D, tf), gate_map),
                pl.BlockSpec((None, D, tf), gate_map),
                pl.BlockSpec((None, F, tn), down_map),
            ],
            out_specs=pl.BlockSpec((tm, tn), out_map),
            scratch_shapes=[pltpu.VMEM((tm, D), jnp.bfloat16),
                            pltpu.VMEM((n_f, tm, tf), jnp.bfloat16)],
        ),
        out_shape=jax.ShapeDtypeStruct((n_slots, D), jnp.float32),
        compiler_params=pltpu.CompilerParams(
            dimension_semantics=("arbitrary", "arbitrary"), vmem_limit_bytes=VMEM_LIMIT),
        name="moe_experts",
    )(tile_e, used, xs, w_gate, w_up, w_down)


def _combine_kernel(dest_ref, h_ref, route_ref, fin_ref, ys_ref, o_ref, ybuf, sems):
    i = pl.program_id(0)
    n_steps = pl.num_programs(0)
    tile = h_ref.shape[0]

    def issue(step, slot):
        base = step * tile

        def start(t, c):
            for k in range(TOP_K):
                pltpu.make_async_copy(
                    ys_ref.at[pl.ds(dest_ref[TOP_K * (base + t) + k], 1), :],
                    ybuf.at[slot, k, pl.ds(t, 1), :], sems.at[slot]).start()
            return c

        lax.fori_loop(0, tile, start, 0, unroll=DMA_UNROLL)

    slot = lax.rem(i, 2)

    @pl.when(i == 0)
    def _():
        issue(0, 0)

    @pl.when(i + 1 < n_steps)
    def _():
        issue(i + 1, 1 - slot)

    for k in range(TOP_K):
        pltpu.make_async_copy(ys_ref.at[pl.ds(0, tile), :], ybuf.at[slot, k], sems.at[slot]).wait()
    route = route_ref[...]
    g0 = route[:, R_G0:R_G0 + 1]
    g1 = route[:, R_G1:R_G1 + 1]
    h = h_ref[...] + (g0 * ybuf[slot, 0] + g1 * ybuf[slot, 1])
    o_ref[...] = h * _rms_scale(h) * fin_ref[...]


def _combine(dest_flat, h, route, final_norm, ys):
    T, D = h.shape
    tc = COMBINE_ROWS
    return pl.pallas_call(
        _combine_kernel,
        grid_spec=pltpu.PrefetchScalarGridSpec(
            num_scalar_prefetch=1,
            grid=(T // tc,),
            in_specs=[pl.BlockSpec((tc, D), lambda i, d: (i, 0)),
                      pl.BlockSpec((tc, LANES), lambda i, d: (i, 0)),
                      pl.BlockSpec((1, D), lambda i, d: (0, 0)),
                      pl.BlockSpec(memory_space=pl.ANY)],
            out_specs=pl.BlockSpec((tc, D), lambda i, d: (i, 0)),
            scratch_shapes=[pltpu.VMEM((2, TOP_K, tc, D), jnp.float32),
                            pltpu.SemaphoreType.DMA((2,))],
        ),
        out_shape=jax.ShapeDtypeStruct((T, D), jnp.float32),
        compiler_params=pltpu.CompilerParams(
            dimension_semantics=("arbitrary",), vmem_limit_bytes=VMEM_LIMIT),
        name="moe_combine_final_norm",
    )(dest_flat, h, route, final_norm.reshape(1, D), ys)


def _routing_tables(route, counts_row):
    T = route.shape[0]
    counts = counts_row[0, :N_EXPERTS].astype(jnp.int32)
    padded = ((counts + MOE_ROWS - 1) // MOE_ROWS) * MOE_ROWS
    ends = jnp.cumsum(padded)
    starts = ends - padded
    expert = route[:, R_E0:R_E1 + 1].astype(jnp.int32)
    rank = route[:, R_RANK0:R_RANK1 + 1].astype(jnp.int32)
    onehot = expert[..., None] == jnp.arange(N_EXPERTS, dtype=jnp.int32)
    dest = jnp.sum(jnp.where(onehot, starts, 0), axis=-1) + rank
    n_tiles = (TOP_K * T) // MOE_ROWS + N_EXPERTS
    tile_start = jnp.arange(n_tiles, dtype=jnp.int32) * MOE_ROWS
    tile_e = jnp.minimum(jnp.sum(ends[None, :] <= tile_start[:, None], axis=1),
                         N_EXPERTS - 1).astype(jnp.int32)
    used = (ends[-1:] // MOE_ROWS).astype(jnp.int32)
    pad_start = (starts + counts).astype(jnp.int32)
    pad_len = (padded - counts).astype(jnp.int32)
    return dest.reshape(-1), tile_e, used, pad_start, pad_len, n_tiles * MOE_ROWS


def kernel(x, a_norm, a_proj, a_scale, kv_norm, w_kv, b_norm, w_q, w_o, rel_bias, ffn_norm,
           dense_w_gate, dense_w_up, dense_w_down, moe_router, moe_w_gate, moe_w_up,
           moe_w_down, final_norm):
    B, S, D = x.shape
    T = B * S
    h1, hn1 = _pool_layer(x, a_norm[0], a_proj[0], a_scale[0], ffn_norm[0])
    h2 = _dense_layer(hn1.reshape(T, D), h1.reshape(T, D), dense_w_gate[0],
                      dense_w_up[0], dense_w_down[0])
    outs, lses = [], []
    for g in range(N_ATT_GROUPS):
        q, k, v = _qkv_project(h2, kv_norm, b_norm[0], w_q[0], w_kv, g, B, S)
        o, lse = _attention_group(q, k, v, rel_bias, g, B, S)
        outs.append(o)
        lses.append(lse)
    h3, hn3, route, counts = _merge_layer(outs, lses, h2, w_o[0], ffn_norm[1], moe_router[0])
    dest, tile_e, used, pad_start, pad_len, n_slots = _routing_tables(route, counts)
    xs = _dispatch(dest, pad_start, pad_len, used, hn3, n_slots)
    ys = _expert_ffn(tile_e, used, xs, moe_w_gate[0], moe_w_up[0], moe_w_down[0])
    out = _combine(dest, h3, route, final_norm, ys)
    return out.reshape(B, S, D)
```

```python
import functools
import math

import jax
import jax.numpy as jnp
from jax import lax
from jax.experimental import pallas as pl
from jax.experimental.pallas import tpu as pltpu

D_MODEL = 1024
EPS = 1e-6
POOL_WINDOWS = (2, 4, 8, 16)
POOL_GROUP_DIM = D_MODEL // len(POOL_WINDOWS)
MAX_POOL_WINDOW = max(POOL_WINDOWS)
HEAD_DIM = 64
HEADS_PER_GROUP = D_MODEL // HEAD_DIM
DILATED_PATTERNS = ((128, 1), (512, 4), (2048, 16))
N_ATT_GROUPS = len(DILATED_PATTERNS)
QBLK = 128
NEG_INF = -1e30
LOG2E = 1.4426950408889634
Q_SCALE = HEAD_DIM ** -0.5 * LOG2E
N_BUCKETS = 32
MAX_EXACT = N_BUCKETS // 2
MAX_DISTANCE = 2048
N_EXPERTS = 8
TOP_K = 2

LANES = 128
SUBLANES = 8
N_SLABS = D_MODEL // LANES
HEAD_PAIR = 2 * HEAD_DIM
assert HEAD_PAIR == LANES

POOL_ROWS = 512
DENSE_ROWS = 512
QKV_ROWS = 512
MERGE_ROWS = 512
MOE_ROWS = 1024
MOE_FT = 512
MOE_NT = 512
DISPATCH_ROWS = 1024
COMBINE_ROWS = 512
DMA_UNROLL = 8
VMEM_LIMIT = 56 * 1024 * 1024

R_E0, R_E1, R_RANK0, R_RANK1, R_G0, R_G1 = 0, 1, 2, 3, 4, 5


def _rms_scale(x):
    return lax.rsqrt(jnp.mean(x * x, axis=-1, keepdims=True) + EPS)


def _silu(g):
    return g * (1.0 / (1.0 + jnp.exp(-g)))


def _split_bf16(x):
    hi = x.astype(jnp.bfloat16)
    lo = (x - hi.astype(jnp.float32)).astype(jnp.bfloat16)
    return hi, lo


def _pool_kernel(x_ref, halo_ref, an_ref, proj_ref, asc_ref, fn_ref, h_ref, hn_ref):
    i = pl.program_id(1)
    x = x_ref[...]
    halo = halo_ref[...]
    gain = an_ref[...]
    xn = x * _rms_scale(x) * gain
    hn_halo = halo * _rms_scale(halo) * gain
    hn_halo = jnp.where(i > 0, hn_halo, 0.0)
    full = jnp.concatenate([hn_halo, xn], axis=0)
    ts = x.shape[0]
    pos = i * ts + lax.broadcasted_iota(jnp.int32, (ts, 1), 0)
    outs = []
    for g, w in enumerate(POOL_WINDOWS):
        c0 = g * POOL_GROUP_DIM
        s = full[:, c0:c0 + POOL_GROUP_DIM]
        span = 1
        while span < w:
            s = s + pltpu.roll(s, span, axis=0)
            span *= 2
        s = s[MAX_POOL_WINDOW:, :]
        cnt = jnp.minimum(pos + 1, w).astype(jnp.float32)
        pooled = s / cnt - xn[:, c0:c0 + POOL_GROUP_DIM]
        outs.append(jnp.dot(pooled.astype(jnp.bfloat16), proj_ref[g],
                            preferred_element_type=jnp.float32))
    mix = jnp.concatenate(outs, axis=1) * asc_ref[...]
    h = x + mix
    h_ref[...] = h
    hn_ref[...] = (h * _rms_scale(h) * fn_ref[...]).astype(hn_ref.dtype)


def _pool_layer(x, a_norm, a_proj, a_scale, ffn_norm):
    B, S, D = x.shape
    ts = POOL_ROWS
    halo_blocks = ts // MAX_POOL_WINDOW
    vec = pl.BlockSpec((1, D), lambda b, i: (0, 0))
    return pl.pallas_call(
        _pool_kernel,
        grid=(B, S // ts),
        in_specs=[
            pl.BlockSpec((None, ts, D), lambda b, i: (b, i, 0)),
            pl.BlockSpec((None, MAX_POOL_WINDOW, D),
                         lambda b, i: (b, jnp.maximum(i * halo_blocks - 1, 0), 0)),
            vec,
            pl.BlockSpec(a_proj.shape, lambda b, i: (0, 0, 0)),
            vec, vec,
        ],
        out_specs=[pl.BlockSpec((None, ts, D), lambda b, i: (b, i, 0)),
                   pl.BlockSpec((None, ts, D), lambda b, i: (b, i, 0))],
        out_shape=[jax.ShapeDtypeStruct((B, S, D), jnp.float32),
                   jax.ShapeDtypeStruct((B, S, D), jnp.bfloat16)],
        compiler_params=pltpu.CompilerParams(
            dimension_semantics=("arbitrary", "arbitrary"), vmem_limit_bytes=VMEM_LIMIT),
        name="pool_mixer",
    )(x, x, a_norm.reshape(1, D), a_proj.astype(jnp.bfloat16), a_scale.reshape(1, D),
      ffn_norm.reshape(1, D))


def _dense_kernel(hn_ref, h_ref, wg_ref, wu_ref, wd_ref, h_out_ref):
    x = hn_ref[...]
    g = jnp.dot(x, wg_ref[...], preferred_element_type=jnp.float32)
    u = jnp.dot(x, wu_ref[...], preferred_element_type=jnp.float32)
    hid = (_silu(g) * u).astype(jnp.bfloat16)
    h_out_ref[...] = h_ref[...] + jnp.dot(hid, wd_ref[...], preferred_element_type=jnp.float32)


def _dense_layer(hn, h, w_gate, w_up, w_down):
    T, D = h.shape
    F = w_gate.shape[1]
    tm = DENSE_ROWS
    row = lambda i: (i, 0)
    resident = dict(index_map=lambda i: (0, 0), pipeline_mode=pl.Buffered(1))
    return pl.pallas_call(
        _dense_kernel,
        grid=(T // tm,),
        in_specs=[
            pl.BlockSpec((tm, D), row),
            pl.BlockSpec((tm, D), row),
            pl.BlockSpec((D, F), **resident),
            pl.BlockSpec((D, F), **resident),
            pl.BlockSpec((F, D), **resident),
        ],
        out_specs=pl.BlockSpec((tm, D), row),
        out_shape=jax.ShapeDtypeStruct((T, D), jnp.float32),
        compiler_params=pltpu.CompilerParams(
            dimension_semantics=("arbitrary",), vmem_limit_bytes=VMEM_LIMIT),
        name="dense_swiglu",
    )(hn, h, w_gate.astype(jnp.bfloat16), w_up.astype(jnp.bfloat16),
      w_down.astype(jnp.bfloat16))


def _qkv_kernel(*refs):
    slabs = refs[:N_SLABS]
    kvg_ref, qg_ref, w_ref = refs[N_SLABS:N_SLABS + 3]
    out_refs = refs[N_SLABS + 3:]
    tm = slabs[0].shape[0]
    for g, (_, dilation) in enumerate(DILATED_PATTERNS):
        n = tm // dilation
        cols = []
        for slab in slabs:
            if dilation == 1:
                cols.append(slab[...])
            else:
                cols.append(jnp.concatenate(
                    [slab[pl.ds(r, n, stride=dilation), :] for r in range(dilation)], axis=0))
        x = jnp.concatenate(cols, axis=1)
        xs = x * _rms_scale(x)
        xq = (xs * qg_ref[...]).astype(jnp.bfloat16)
        xkv = (xs * kvg_ref[...]).astype(jnp.bfloat16)
        q = jnp.dot(xq, w_ref[3 * g], preferred_element_type=jnp.float32) * Q_SCALE
        k = jnp.dot(xkv, w_ref[3 * g + 1], preferred_element_type=jnp.float32)
        v = jnp.dot(xkv, w_ref[3 * g + 2], preferred_element_type=jnp.float32)
        for out_ref, val in zip(out_refs[3 * g:3 * g + 3], (q, k, v)):
            for r in range(dilation):
                out_ref[r] = val[r * n:(r + 1) * n].astype(out_ref.dtype)


def _qkv_project(h, kv_norm, b_norm, w_q, w_kv, B, S):
    T, D = h.shape
    tm = QKV_ROWS
    tiles_per_seq = S // tm
    blocks = []
    for g in range(N_ATT_GROUPS):
        blocks += [w_q[:, g * D:(g + 1) * D], w_kv[:, g * D:(g + 1) * D],
                   w_kv[:, (N_ATT_GROUPS + g) * D:(N_ATT_GROUPS + g + 1) * D]]
    w = jnp.stack(blocks).astype(jnp.bfloat16)
    slab_specs = [pl.BlockSpec((tm, LANES), functools.partial(lambda i, c: (i, c), c=c))
                  for c in range(N_SLABS)]
    vec = pl.BlockSpec((1, D), lambda i: (0, 0))
    out_specs, out_shapes = [], []
    for _, d in DILATED_PATTERNS:
        spec = pl.BlockSpec((None, d, tm // d, D),
                            lambda i: (i // tiles_per_seq, 0, i % tiles_per_seq, 0))
        out_specs += [spec] * 3
        out_shapes += [jax.ShapeDtypeStruct((B, d, S // d, D), jnp.bfloat16)] * 3
    outs = pl.pallas_call(
        _qkv_kernel,
        grid=(T // tm,),
        in_specs=slab_specs + [vec, vec,
                               pl.BlockSpec(w.shape, lambda i: (0, 0, 0),
                                            pipeline_mode=pl.Buffered(1))],
        out_specs=out_specs,
        out_shape=out_shapes,
        compiler_params=pltpu.CompilerParams(
            dimension_semantics=("arbitrary",), vmem_limit_bytes=VMEM_LIMIT),
        name="qkv_proj",
    )(*([h] * N_SLABS), kv_norm.reshape(1, D), b_norm.reshape(1, D), w)
    return [tuple(outs[3 * g:3 * g + 3]) for g in range(N_ATT_GROUPS)]


def _attn_kernel(*refs, dilation, has_prev):
    if has_prev:
        (table_ref, bmap_ref, q_ref, kp_ref, kc_ref, vp_ref, vc_ref,
         o_ref, stat_ref, bias_ref, stat_scr) = refs
    else:
        table_ref, bmap_ref, q_ref, kc_ref, vc_ref, o_ref, stat_ref, bias_ref, stat_scr = refs
    b, j, r = pl.program_id(0), pl.program_id(1), pl.program_id(2)
    n_keys = bias_ref.shape[2]

    @pl.when(jnp.logical_and(jnp.logical_and(b == 0, j == 0), r == 0))
    def _():
        stat_scr[...] = jnp.zeros_like(stat_scr)
        bmap = bmap_ref[...]
        in_prev = lax.broadcasted_iota(jnp.int32, bmap.shape, 1) < n_keys - QBLK

        def head(h, carry):
            acc = jnp.full(bmap.shape, NEG_INF, jnp.float32)
            for bucket in range(N_BUCKETS):
                acc = jnp.where(bmap == bucket, table_ref[bucket, h], acc)
            rows_h = pl.ds(pl.multiple_of(h * QBLK, QBLK), QBLK)
            bias_ref[0, rows_h, :] = jnp.where(in_prev, NEG_INF, acc)
            bias_ref[1, rows_h, :] = acc
            return carry

        lax.fori_loop(0, HEADS_PER_GROUP, head, 0)

    lane = lax.broadcasted_iota(jnp.int32, (QBLK, LANES), 1)
    first_head = lane < HEAD_DIM
    bias_copy = jnp.minimum(j, 1)
    rows = pl.ds(r, QBLK, stride=dilation) if dilation > 1 else slice(None)
    for hp in range(HEADS_PER_GROUP // 2):
        sl = slice(hp * HEAD_PAIR, (hp + 1) * HEAD_PAIR)
        qp = q_ref[:, sl]
        zero = jnp.zeros_like(qp)
        q2 = jnp.concatenate([jnp.where(first_head, qp, zero),
                              jnp.where(first_head, zero, qp)], axis=0)
        if has_prev:
            kcat = jnp.concatenate([kp_ref[:, sl], kc_ref[:, sl]], axis=0)
            vcat = jnp.concatenate([vp_ref[:, sl], vc_ref[:, sl]], axis=0)
        else:
            kcat, vcat = kc_ref[:, sl], vc_ref[:, sl]
        s = lax.dot_general(q2, kcat, (((1,), (1,)), ((), ())),
                            preferred_element_type=jnp.float32)
        s = s + bias_ref[bias_copy, hp * 2 * QBLK:(hp + 1) * 2 * QBLK, :]
        m = jnp.max(s, axis=1, keepdims=True)
        p = jnp.exp2(s - m)
        l = jnp.sum(p, axis=1, keepdims=True)
        o2 = jnp.dot(p.astype(jnp.bfloat16), vcat, preferred_element_type=jnp.float32)
        o_ref[hp, rows, :] = jnp.where(first_head, o2[:QBLK], o2[QBLK:])
        for half, head in ((slice(0, QBLK), 2 * hp), (slice(QBLK, 2 * QBLK), 2 * hp + 1)):
            stat_scr[:, head:head + 1] = m[half]
            stat_scr[:, HEADS_PER_GROUP + head:HEADS_PER_GROUP + head + 1] = l[half]
    stat_ref[rows, :] = stat_scr[...]


def _bucket_map(dilation, window, n_keys):
    W = window // dilation
    a = jnp.arange(QBLK, dtype=jnp.int32)[:, None]
    c = jnp.arange(n_keys, dtype=jnp.int32)[None, :]
    m = a + (n_keys - QBLK) - c
    band = (m >= 0) & (m <= W)
    n = jnp.maximum(m, 0) * dilation
    nf = jnp.maximum(n, 1).astype(jnp.float32)
    large = MAX_EXACT + (jnp.log(nf / MAX_EXACT) / math.log(MAX_DISTANCE / MAX_EXACT)
                         * (N_BUCKETS - MAX_EXACT)).astype(jnp.int32)
    large = jnp.minimum(large, N_BUCKETS - 1)
    bucket = jnp.where(n < MAX_EXACT, n, large)
    return jnp.where(band, bucket, -1)


def _attention_group(q, k, v, rel_bias, g, B, S):
    window, d = DILATED_PATTERNS[g]
    L = S // d
    n = L // QBLK
    D = D_MODEL
    has_prev = n > 1
    n_keys = 2 * QBLK if has_prev else QBLK
    table = rel_bias[:, g * HEADS_PER_GROUP:(g + 1) * HEADS_PER_GROUP].astype(jnp.float32) * LOG2E
    bmap = _bucket_map(d, window, n_keys)
    blk = (None, None, QBLK, D)
    cur = lambda b, j, r: (b, r, j, 0)
    prev = lambda b, j, r: (b, r, jnp.maximum(j - 1, 0), 0)
    if has_prev:
        kv_specs = [pl.BlockSpec(blk, prev), pl.BlockSpec(blk, cur),
                    pl.BlockSpec(blk, prev), pl.BlockSpec(blk, cur)]
        kv_args = (k, k, v, v)
    else:
        kv_specs = [pl.BlockSpec(blk, cur), pl.BlockSpec(blk, cur)]
        kv_args = (k, v)
    span = QBLK * d
    return pl.pallas_call(
        functools.partial(_attn_kernel, dilation=d, has_prev=has_prev),
        grid=(B, n, d),
        in_specs=[pl.BlockSpec(memory_space=pltpu.SMEM),
                  pl.BlockSpec(bmap.shape, lambda b, j, r: (0, 0)),
                  pl.BlockSpec(blk, cur)] + kv_specs,
        out_specs=[pl.BlockSpec((N_SLABS, span, LANES), lambda b, j, r: (0, b * n + j, 0)),
                   pl.BlockSpec((span, LANES), lambda b, j, r: (b * n + j, 0))],
        out_shape=[jax.ShapeDtypeStruct((N_SLABS, B * S, LANES), jnp.float32),
                   jax.ShapeDtypeStruct((B * S, LANES), jnp.float32)],
        scratch_shapes=[pltpu.VMEM((2, HEADS_PER_GROUP * QBLK, n_keys), jnp.float32),
                        pltpu.VMEM((QBLK, LANES), jnp.float32)],
        compiler_params=pltpu.CompilerParams(
            dimension_semantics=("arbitrary",) * 3, vmem_limit_bytes=VMEM_LIMIT),
        name=f"dilated_attn_g{g}",
    )(table, bmap, q, *kv_args)


def _merge_kernel(o0_ref, o1_ref, o2_ref, s0_ref, s1_ref, s2_ref, h_ref, wo_ref, fn_ref,
                  r2_ref, expand_ref,
                  h_out_ref, hn_ref, route_ref, counts_ref, carry_ref):
    i = pl.program_id(0)
    tm = h_ref.shape[0]

    @pl.when(i == 0)
    def _():
        carry_ref[...] = jnp.zeros_like(carry_ref)

    lane = lax.broadcasted_iota(jnp.int32, (tm, LANES), 1)
    head_lane = lane < HEADS_PER_GROUP
    stats = [s0_ref[...], s1_ref[...], s2_ref[...]]
    dens = [pltpu.roll(st, LANES - HEADS_PER_GROUP, axis=1) for st in stats]
    mx = jnp.maximum(jnp.maximum(stats[0], stats[1]), stats[2])
    es = [jnp.exp2(st - mx) for st in stats]
    inv = 1.0 / (es[0] * dens[0] + es[1] * dens[1] + es[2] * dens[2])
    merged = None
    for e, o_ref in zip(es, (o0_ref, o1_ref, o2_ref)):
        w = jnp.where(head_lane, e * inv, 0.0)
        hi = w.astype(jnp.bfloat16).astype(jnp.float32)
        packed = (hi + pltpu.roll(w - hi, HEADS_PER_GROUP, axis=1)).astype(jnp.bfloat16)
        wide = jnp.dot(packed, expand_ref[...], preferred_element_type=jnp.float32)
        o = jnp.concatenate([o_ref[c] for c in range(N_SLABS)], axis=1)
        term = wide * o
        merged = term if merged is None else merged + term
    h = h_ref[...] + jnp.dot(merged.astype(jnp.bfloat16), wo_ref[...],
                             preferred_element_type=jnp.float32)
    h_out_ref[...] = h
    hn = h * _rms_scale(h) * fn_ref[...]
    hn_ref[...] = hn

    hi, lo = _split_bf16(hn)
    both = jnp.dot(hi, r2_ref[...], preferred_element_type=jnp.float32)
    logits = (both[:, :LANES] + both[:, LANES:]
              + jnp.dot(lo, r2_ref[:, :LANES], preferred_element_type=jnp.float32))
    logits = jnp.where(lane < N_EXPERTS, logits, -jnp.inf)
    v0 = jnp.max(logits, axis=1, keepdims=True)
    e0 = jnp.min(jnp.where(logits == v0, lane, LANES), axis=1, keepdims=True)
    rest = jnp.where(lane == e0, -jnp.inf, logits)
    v1 = jnp.max(rest, axis=1, keepdims=True)
    e1 = jnp.min(jnp.where(rest == v1, lane, LANES), axis=1, keepdims=True)
    t = jnp.exp(v1 - v0)
    g0 = 1.0 / (1.0 + t)
    g1 = t / (1.0 + t)

    hit0 = lane == e0
    hit1 = lane == e1
    onehot = jnp.where(jnp.logical_or(hit0, hit1), 1.0, 0.0)
    r_i = lax.broadcasted_iota(jnp.int32, (tm, tm), 0)
    c_i = lax.broadcasted_iota(jnp.int32, (tm, tm), 1)
    lower = jnp.where(c_i < r_i, 1.0, 0.0).astype(jnp.bfloat16)
    before = jnp.dot(lower, onehot.astype(jnp.bfloat16),
                     preferred_element_type=jnp.float32) + carry_ref[0:1, :]
    rank0 = jnp.sum(jnp.where(hit0, before, 0.0), axis=1, keepdims=True)
    rank1 = jnp.sum(jnp.where(hit1, before, 0.0), axis=1, keepdims=True)
    total = carry_ref[0:1, :] + jnp.sum(onehot, axis=0, keepdims=True)
    carry_ref[...] = jnp.broadcast_to(total, carry_ref.shape)
    counts_ref[...] = jnp.broadcast_to(total, counts_ref.shape)

    route = jnp.zeros((tm, LANES), jnp.float32)
    for ln, val in ((R_E0, e0.astype(jnp.float32)), (R_E1, e1.astype(jnp.float32)),
                    (R_RANK0, rank0), (R_RANK1, rank1), (R_G0, g0), (R_G1, g1)):
        route = jnp.where(lane == ln, val, route)
    route_ref[...] = route


def _merge_layer(outs, lses, h, w_o, ffn_norm, w_router):
    T, D = h.shape
    tm = MERGE_ROWS
    r_pad = jnp.zeros((D, LANES), jnp.float32).at[:, :N_EXPERTS].set(w_router)
    r2 = jnp.concatenate(_split_bf16(r_pad), axis=1)
    head_of_col = jnp.arange(D, dtype=jnp.int32) // HEAD_DIM
    lane_id = jnp.arange(LANES, dtype=jnp.int32)[:, None]
    expand = ((lane_id < 2 * HEADS_PER_GROUP)
              & (lane_id % HEADS_PER_GROUP == head_of_col[None, :])).astype(jnp.bfloat16)
    row = lambda i: (i, 0)
    const = lambda i: (0, 0)
    wide = pl.BlockSpec((tm, D), row)
    narrow = pl.BlockSpec((tm, LANES), row)
    slabs = pl.BlockSpec((N_SLABS, tm, LANES), lambda i: (0, i, 0))
    return pl.pallas_call(
        _merge_kernel,
        grid=(T // tm,),
        in_specs=[slabs, slabs, slabs, narrow, narrow, narrow, wide,
                  pl.BlockSpec((D, D), const), pl.BlockSpec((1, D), const),
                  pl.BlockSpec((D, 2 * LANES), const), pl.BlockSpec((LANES, D), const)],
        out_specs=[wide, wide, narrow, pl.BlockSpec((8, LANES), const)],
        out_shape=[jax.ShapeDtypeStruct((T, D), jnp.float32),
                   jax.ShapeDtypeStruct((T, D), jnp.float32),
                   jax.ShapeDtypeStruct((T, LANES), jnp.float32),
                   jax.ShapeDtypeStruct((8, LANES), jnp.float32)],
        scratch_shapes=[pltpu.VMEM((8, LANES), jnp.float32)],
        compiler_params=pltpu.CompilerParams(
            dimension_semantics=("arbitrary",), vmem_limit_bytes=VMEM_LIMIT),
        name="merge_outproj_router",
    )(*outs, *lses, h, w_o.astype(jnp.bfloat16), ffn_norm.reshape(1, D), r2, expand)


def _dispatch_kernel(dest_ref, pad_start_ref, pad_len_ref, used_ref, hn_ref, xs_ref,
                     zbuf, sem, zsem):
    i = pl.program_id(0)
    tile = hn_ref.shape[0]
    base = i * tile
    n_tiles = xs_ref.shape[0] // MOE_ROWS

    def zero_fill(act):
        for e in range(N_EXPERTS):
            run_start = pad_start_ref[e]
            head = (-run_start) & (SUBLANES - 1)
            for row in range(SUBLANES - 1):
                @pl.when(row < head)
                def _():
                    act(pltpu.make_async_copy(zbuf.at[pl.ds(0, 1), :],
                                              xs_ref.at[pl.ds(run_start + row, 1), :], zsem))
            body_start = run_start + head
            body_len = pad_len_ref[e] - head
            bit = MOE_ROWS // 2
            while bit >= SUBLANES:
                off = pl.multiple_of(body_start + (body_len & -(2 * bit)), SUBLANES)

                @pl.when((body_len & bit) != 0)
                def _():
                    act(pltpu.make_async_copy(zbuf.at[pl.ds(0, bit), :],
                                              xs_ref.at[pl.ds(off, bit), :], zsem))
                bit //= 2
        for t in range(n_tiles):
            @pl.when(t >= used_ref[0])
            def _():
                act(pltpu.make_async_copy(zbuf, xs_ref.at[pl.ds(t * MOE_ROWS, MOE_ROWS), :], zsem))

    @pl.when(i == 0)
    def _():
        zbuf[...] = jnp.zeros_like(zbuf)
        zero_fill(lambda c: c.start())

    def start(t, c):
        for k in range(TOP_K):
            pltpu.make_async_copy(hn_ref.at[pl.ds(t, 1), :],
                                  xs_ref.at[pl.ds(dest_ref[TOP_K * (base + t) + k], 1), :],
                                  sem).start()
        return c

    lax.fori_loop(0, tile, start, 0, unroll=DMA_UNROLL)
    for k in range(TOP_K):
        pltpu.make_async_copy(hn_ref, xs_ref.at[pl.ds(0, tile), :], sem).wait()

    @pl.when(i == 0)
    def _():
        zero_fill(lambda c: c.wait())


def _dispatch(dest_flat, pad_start, pad_len, used, hn, n_slots):
    T, D = hn.shape
    return pl.pallas_call(
        _dispatch_kernel,
        grid_spec=pltpu.PrefetchScalarGridSpec(
            num_scalar_prefetch=4,
            grid=(T // DISPATCH_ROWS,),
            in_specs=[pl.BlockSpec((DISPATCH_ROWS, D), lambda i, *_: (i, 0))],
            out_specs=pl.BlockSpec(memory_space=pl.ANY),
            scratch_shapes=[pltpu.VMEM((MOE_ROWS, D), hn.dtype),
                            pltpu.SemaphoreType.DMA(()), pltpu.SemaphoreType.DMA(())],
        ),
        out_shape=jax.ShapeDtypeStruct((n_slots, D), hn.dtype),
        compiler_params=pltpu.CompilerParams(
            dimension_semantics=("arbitrary",), vmem_limit_bytes=VMEM_LIMIT),
        name="moe_dispatch",
    )(dest_flat, pad_start, pad_len, used, hn)


def _expert_kernel(tile_e_ref, used_ref, x_ref, wg_ref, wu_ref, wd_ref, y_ref, xb_ref, hid_ref):
    del tile_e_ref
    i = pl.program_id(0)
    s = pl.program_id(1)
    n_f = hid_ref.shape[0]
    active = i < used_ref[0]

    @pl.when(jnp.logical_and(active, s == 0))
    def _():
        xb_ref[...] = x_ref[...].astype(xb_ref.dtype)

    @pl.when(jnp.logical_and(active, s < n_f))
    def _():
        x = xb_ref[...]
        g = jnp.dot(x, wg_ref[...].astype(jnp.bfloat16), preferred_element_type=jnp.float32)
        u = jnp.dot(x, wu_ref[...].astype(jnp.bfloat16), preferred_element_type=jnp.float32)
        hid_ref[s] = (_silu(g) * u).astype(hid_ref.dtype)

    @pl.when(jnp.logical_and(active, s >= n_f))
    def _():
        hid = jnp.concatenate([hid_ref[c] for c in range(n_f)], axis=1)
        y_ref[...] = jnp.dot(hid, wd_ref[...].astype(jnp.bfloat16),
                             preferred_element_type=jnp.float32)

    @pl.when(jnp.logical_and(jnp.logical_not(active), s >= n_f))
    def _():
        y_ref[...] = jnp.zeros_like(y_ref)


def _expert_ffn(tile_e, used, xs, w_gate, w_up, w_down):
    n_slots, D = xs.shape
    F = w_gate.shape[2]
    tm, tf, tn = MOE_ROWS, MOE_FT, MOE_NT
    n_f, n_n = F // tf, D // tn
    n_tiles = n_slots // tm
    last = n_f + n_n - 1

    def eff(i, s, used_ref):
        idle = i >= used_ref[0]
        return jnp.where(idle, used_ref[0] - 1, i), jnp.where(idle, last, s)

    def x_map(i, s, te, us):
        ie, se = eff(i, s, us)
        return jnp.minimum(ie + jnp.minimum(se, 1), us[0] - 1), 0

    def gate_map(i, s, te, us):
        ie, se = eff(i, s, us)
        return te[ie], 0, jnp.minimum(se, n_f - 1)

    def down_map(i, s, te, us):
        ie, se = eff(i, s, us)
        tile = jnp.where(se == 0, jnp.maximum(ie - 1, 0), ie)
        return te[tile], 0, jnp.where(se == 0, n_n - 1, jnp.clip(se - n_f, 0, n_n - 1))

    def out_map(i, s, te, us):
        return i, jnp.clip(s - n_f, 0, n_n - 1)

    return pl.pallas_call(
        _expert_kernel,
        grid_spec=pltpu.PrefetchScalarGridSpec(
            num_scalar_prefetch=2,
            grid=(n_tiles, n_f + n_n),
            in_specs=[
                pl.BlockSpec((tm, D), x_map),
                pl.BlockSpec((None, D, tf), gate_map),
                pl.BlockSpec((None, D, tf), gate_map),
                pl.BlockSpec((None, F, tn), down_map),
            ],
            out_specs=pl.BlockSpec((tm, tn), out_map),
            scratch_shapes=[pltpu.VMEM((tm, D), jnp.bfloat16),
                            pltpu.VMEM((n_f, tm, tf), jnp.bfloat16)],
        ),
        out_shape=jax.ShapeDtypeStruct((n_slots, D), jnp.float32),
        compiler_params=pltpu.CompilerParams(
            dimension_semantics=("arbitrary", "arbitrary"), vmem_limit_bytes=VMEM_LIMIT),
        name="moe_experts",
    )(tile_e, used, xs, w_gate, w_up, w_down)


def _combine_kernel(dest_ref, h_ref, route_ref, fin_ref, ys_ref, o_ref, ybuf, sems):
    i = pl.program_id(0)
    n_steps = pl.num_programs(0)
    tile = h_ref.shape[0]

    def issue(step, slot):
        base = step * tile

        def start(t, c):
            for k in range(TOP_K):
                pltpu.make_async_copy(
                    ys_ref.at[pl.ds(dest_ref[TOP_K * (base + t) + k], 1), :],
                    ybuf.at[slot, k, pl.ds(t, 1), :], sems.at[slot]).start()
            return c

        lax.fori_loop(0, tile, start, 0, unroll=DMA_UNROLL)

    slot = lax.rem(i, 2)

    @pl.when(i == 0)
    def _():
        issue(0, 0)

    @pl.when(i + 1 < n_steps)
    def _():
        issue(i + 1, 1 - slot)

    for k in range(TOP_K):
        pltpu.make_async_copy(ys_ref.at[pl.ds(0, tile), :], ybuf.at[slot, k], sems.at[slot]).wait()
    route = route_ref[...]
    g0 = route[:, R_G0:R_G0 + 1]
    g1 = route[:, R_G1:R_G1 + 1]
    h = h_ref[...] + (g0 * ybuf[slot, 0] + g1 * ybuf[slot, 1])
    o_ref[...] = h * _rms_scale(h) * fin_ref[...]


def _combine(dest_flat, h, route, final_norm, ys):
    T, D = h.shape
    tc = COMBINE_ROWS
    return pl.pallas_call(
        _combine_kernel,
        grid_spec=pltpu.PrefetchScalarGridSpec(
            num_scalar_prefetch=1,
            grid=(T // tc,),
            in_specs=[pl.BlockSpec((tc, D), lambda i, d: (i, 0)),
                      pl.BlockSpec((tc, LANES), lambda i, d: (i, 0)),
                      pl.BlockSpec((1, D), lambda i, d: (0, 0)),
                      pl.BlockSpec(memory_space=pl.ANY)],
            out_specs=pl.BlockSpec((tc, D), lambda i, d: (i, 0)),
            scratch_shapes=[pltpu.VMEM((2, TOP_K, tc, D), jnp.float32),
                            pltpu.SemaphoreType.DMA((2,))],
        ),
        out_shape=jax.ShapeDtypeStruct((T, D), jnp.float32),
        compiler_params=pltpu.CompilerParams(
            dimension_semantics=("arbitrary",), vmem_limit_bytes=VMEM_LIMIT),
        name="moe_combine_final_norm",
    )(dest_flat, h, route, final_norm.reshape(1, D), ys)


def _routing_tables(route, counts_row):
    T = route.shape[0]
    counts = counts_row[0, :N_EXPERTS].astype(jnp.int32)
    padded = ((counts + MOE_ROWS - 1) // MOE_ROWS) * MOE_ROWS
    ends = jnp.cumsum(padded)
    starts = ends - padded
    expert = route[:, R_E0:R_E1 + 1].astype(jnp.int32)
    rank = route[:, R_RANK0:R_RANK1 + 1].astype(jnp.int32)
    onehot = expert[..., None] == jnp.arange(N_EXPERTS, dtype=jnp.int32)
    dest = jnp.sum(jnp.where(onehot, starts, 0), axis=-1) + rank
    n_tiles = (TOP_K * T) // MOE_ROWS + N_EXPERTS
    tile_start = jnp.arange(n_tiles, dtype=jnp.int32) * MOE_ROWS
    tile_e = jnp.minimum(jnp.sum(ends[None, :] <= tile_start[:, None], axis=1),
                         N_EXPERTS - 1).astype(jnp.int32)
    used = (ends[-1:] // MOE_ROWS).astype(jnp.int32)
    pad_start = (starts + counts).astype(jnp.int32)
    pad_len = (padded - counts).astype(jnp.int32)
    return dest.reshape(-1), tile_e, used, pad_start, pad_len, n_tiles * MOE_ROWS


def kernel(x, a_norm, a_proj, a_scale, kv_norm, w_kv, b_norm, w_q, w_o, rel_bias, ffn_norm,
           dense_w_gate, dense_w_up, dense_w_down, moe_router, moe_w_gate, moe_w_up,
           moe_w_down, final_norm):
    B, S, D = x.shape
    T = B * S
    h1, hn1 = _pool_layer(x, a_norm[0], a_proj[0], a_scale[0], ffn_norm[0])
    h2 = _dense_layer(hn1.reshape(T, D), h1.reshape(T, D), dense_w_gate[0],
                      dense_w_up[0], dense_w_down[0])
    outs, stats = [], []
    for g, (q, k, v) in enumerate(_qkv_project(h2, kv_norm, b_norm[0], w_q[0], w_kv, B, S)):
        o, st = _attention_group(q, k, v, rel_bias, g, B, S)
        outs.append(o)
        stats.append(st)
    h3, hn3, route, counts = _merge_layer(outs, stats, h2, w_o[0], ffn_norm[1], moe_router[0])
    dest, tile_e, used, pad_start, pad_len, n_slots = _routing_tables(route, counts)
    xs = _dispatch(dest, pad_start, pad_len, used, hn3, n_slots)
    ys = _expert_ffn(tile_e, used, xs, moe_w_gate[0], moe_w_up[0], moe_w_down[0])
    out = _combine(dest, h3, route, final_norm, ys)
    return out.reshape(B, S, D)
```

```python
import functools
import math

import jax
import jax.numpy as jnp
from jax import lax
from jax.experimental import pallas as pl
from jax.experimental.pallas import tpu as pltpu

D_MODEL = 1024
EPS = 1e-6
POOL_WINDOWS = (2, 4, 8, 16)
POOL_GROUP_DIM = D_MODEL // len(POOL_WINDOWS)
MAX_POOL_WINDOW = max(POOL_WINDOWS)
HEAD_DIM = 64
HEADS_PER_GROUP = D_MODEL // HEAD_DIM
DILATED_PATTERNS = ((128, 1), (512, 4), (2048, 16))
N_ATT_GROUPS = len(DILATED_PATTERNS)
QBLK = 128
NEG_INF = -1e30
LOG2E = 1.4426950408889634
Q_SCALE = HEAD_DIM ** -0.5 * LOG2E
N_BUCKETS = 32
MAX_EXACT = N_BUCKETS // 2
MAX_DISTANCE = 2048
N_EXPERTS = 8
TOP_K = 2

LANES = 128
SUBLANES = 8
N_SLABS = D_MODEL // LANES
HEAD_PAIR = 2 * HEAD_DIM
assert HEAD_PAIR == LANES

POOL_ROWS = 512
DENSE_ROWS = 512
QKV_ROWS = 512
MERGE_ROWS = 512
MOE_ROWS = 1024
MOE_FT = 1792
MOE_CHUNK = 512
MOE_NT = 512
DISPATCH_ROWS = 1024
COMBINE_ROWS = 512
DMA_UNROLL = 8
VMEM_LIMIT = 56 * 1024 * 1024

R_E0, R_E1, R_RANK0, R_RANK1, R_G0, R_G1 = 0, 1, 2, 3, 4, 5


def _rms_scale(x):
    return lax.rsqrt(jnp.mean(x * x, axis=-1, keepdims=True) + EPS)


def _silu(g):
    return g * (1.0 / (1.0 + jnp.exp(-g)))


def _split_bf16(x):
    hi = x.astype(jnp.bfloat16)
    lo = (x - hi.astype(jnp.float32)).astype(jnp.bfloat16)
    return hi, lo


def _pool_kernel(x_ref, halo_ref, an_ref, proj_ref, asc_ref, fn_ref, h_ref, hn_ref):
    i = pl.program_id(1)
    x = x_ref[...]
    halo = halo_ref[...]
    gain = an_ref[...]
    xn = x * _rms_scale(x) * gain
    hn_halo = halo * _rms_scale(halo) * gain
    hn_halo = jnp.where(i > 0, hn_halo, 0.0)
    full = jnp.concatenate([hn_halo, xn], axis=0)
    ts = x.shape[0]
    pos = i * ts + lax.broadcasted_iota(jnp.int32, (ts, 1), 0)
    outs = []
    for g, w in enumerate(POOL_WINDOWS):
        c0 = g * POOL_GROUP_DIM
        s = full[:, c0:c0 + POOL_GROUP_DIM]
        span = 1
        while span < w:
            s = s + pltpu.roll(s, span, axis=0)
            span *= 2
        s = s[MAX_POOL_WINDOW:, :]
        cnt = jnp.minimum(pos + 1, w).astype(jnp.float32)
        pooled = s / cnt - xn[:, c0:c0 + POOL_GROUP_DIM]
        outs.append(jnp.dot(pooled.astype(jnp.bfloat16), proj_ref[g],
                            preferred_element_type=jnp.float32))
    mix = jnp.concatenate(outs, axis=1) * asc_ref[...]
    h = x + mix
    h_ref[...] = h
    hn_ref[...] = (h * _rms_scale(h) * fn_ref[...]).astype(hn_ref.dtype)


def _pool_layer(x, a_norm, a_proj, a_scale, ffn_norm):
    B, S, D = x.shape
    ts = POOL_ROWS
    halo_blocks = ts // MAX_POOL_WINDOW
    vec = pl.BlockSpec((1, D), lambda b, i: (0, 0))
    return pl.pallas_call(
        _pool_kernel,
        grid=(B, S // ts),
        in_specs=[
            pl.BlockSpec((None, ts, D), lambda b, i: (b, i, 0)),
            pl.BlockSpec((None, MAX_POOL_WINDOW, D),
                         lambda b, i: (b, jnp.maximum(i * halo_blocks - 1, 0), 0)),
            vec,
            pl.BlockSpec(a_proj.shape, lambda b, i: (0, 0, 0)),
            vec, vec,
        ],
        out_specs=[pl.BlockSpec((None, ts, D), lambda b, i: (b, i, 0)),
                   pl.BlockSpec((None, ts, D), lambda b, i: (b, i, 0))],
        out_shape=[jax.ShapeDtypeStruct((B, S, D), jnp.float32),
                   jax.ShapeDtypeStruct((B, S, D), jnp.bfloat16)],
        compiler_params=pltpu.CompilerParams(
            dimension_semantics=("arbitrary", "arbitrary"), vmem_limit_bytes=VMEM_LIMIT),
        name="pool_mixer",
    )(x, x, a_norm.reshape(1, D), a_proj.astype(jnp.bfloat16), a_scale.reshape(1, D),
      ffn_norm.reshape(1, D))


def _dense_kernel(hn_ref, h_ref, wg_ref, wu_ref, wd_ref, h_out_ref):
    x = hn_ref[...]
    g = jnp.dot(x, wg_ref[...], preferred_element_type=jnp.float32)
    u = jnp.dot(x, wu_ref[...], preferred_element_type=jnp.float32)
    hid = (_silu(g) * u).astype(jnp.bfloat16)
    h_out_ref[...] = h_ref[...] + jnp.dot(hid, wd_ref[...], preferred_element_type=jnp.float32)


def _dense_layer(hn, h, w_gate, w_up, w_down):
    T, D = h.shape
    F = w_gate.shape[1]
    tm = DENSE_ROWS
    row = lambda i: (i, 0)
    resident = dict(index_map=lambda i: (0, 0), pipeline_mode=pl.Buffered(1))
    return pl.pallas_call(
        _dense_kernel,
        grid=(T // tm,),
        in_specs=[
            pl.BlockSpec((tm, D), row),
            pl.BlockSpec((tm, D), row),
            pl.BlockSpec((D, F), **resident),
            pl.BlockSpec((D, F), **resident),
            pl.BlockSpec((F, D), **resident),
        ],
        out_specs=pl.BlockSpec((tm, D), row),
        out_shape=jax.ShapeDtypeStruct((T, D), jnp.float32),
        compiler_params=pltpu.CompilerParams(
            dimension_semantics=("arbitrary",), vmem_limit_bytes=VMEM_LIMIT),
        name="dense_swiglu",
    )(hn, h, w_gate.astype(jnp.bfloat16), w_up.astype(jnp.bfloat16),
      w_down.astype(jnp.bfloat16))


def _qkv_kernel(*refs):
    slabs = refs[:N_SLABS]
    kvg_ref, qg_ref, w_ref = refs[N_SLABS:N_SLABS + 3]
    out_refs = refs[N_SLABS + 3:]
    tm = slabs[0].shape[0]
    for g, (_, dilation) in enumerate(DILATED_PATTERNS):
        n = tm // dilation
        cols = []
        for slab in slabs:
            if dilation == 1:
                cols.append(slab[...])
            else:
                cols.append(jnp.concatenate(
                    [slab[pl.ds(r, n, stride=dilation), :] for r in range(dilation)], axis=0))
        x = jnp.concatenate(cols, axis=1)
        xs = x * _rms_scale(x)
        xq = (xs * qg_ref[...]).astype(jnp.bfloat16)
        xkv = (xs * kvg_ref[...]).astype(jnp.bfloat16)
        q = jnp.dot(xq, w_ref[3 * g], preferred_element_type=jnp.float32) * Q_SCALE
        k = jnp.dot(xkv, w_ref[3 * g + 1], preferred_element_type=jnp.float32)
        v = jnp.dot(xkv, w_ref[3 * g + 2], preferred_element_type=jnp.float32)
        for out_ref, val in zip(out_refs[3 * g:3 * g + 3], (q, k, v)):
            for r in range(dilation):
                out_ref[r] = val[r * n:(r + 1) * n].astype(out_ref.dtype)


def _qkv_project(h, kv_norm, b_norm, w_q, w_kv, B, S):
    T, D = h.shape
    tm = QKV_ROWS
    tiles_per_seq = S // tm
    blocks = []
    for g in range(N_ATT_GROUPS):
        blocks += [w_q[:, g * D:(g + 1) * D], w_kv[:, g * D:(g + 1) * D],
                   w_kv[:, (N_ATT_GROUPS + g) * D:(N_ATT_GROUPS + g + 1) * D]]
    w = jnp.stack(blocks).astype(jnp.bfloat16)
    slab_specs = [pl.BlockSpec((tm, LANES), functools.partial(lambda i, c: (i, c), c=c))
                  for c in range(N_SLABS)]
    vec = pl.BlockSpec((1, D), lambda i: (0, 0))
    out_specs, out_shapes = [], []
    for _, d in DILATED_PATTERNS:
        spec = pl.BlockSpec((None, d, tm // d, D),
                            lambda i: (i // tiles_per_seq, 0, i % tiles_per_seq, 0))
        out_specs += [spec] * 3
        out_shapes += [jax.ShapeDtypeStruct((B, d, S // d, D), jnp.bfloat16)] * 3
    outs = pl.pallas_call(
        _qkv_kernel,
        grid=(T // tm,),
        in_specs=slab_specs + [vec, vec,
                               pl.BlockSpec(w.shape, lambda i: (0, 0, 0),
                                            pipeline_mode=pl.Buffered(1))],
        out_specs=out_specs,
        out_shape=out_shapes,
        compiler_params=pltpu.CompilerParams(
            dimension_semantics=("arbitrary",), vmem_limit_bytes=VMEM_LIMIT),
        name="qkv_proj",
    )(*([h] * N_SLABS), kv_norm.reshape(1, D), b_norm.reshape(1, D), w)
    return [tuple(outs[3 * g:3 * g + 3]) for g in range(N_ATT_GROUPS)]


def _attn_kernel(*refs, dilation, has_prev):
    if has_prev:
        (table_ref, bmap_ref, q_ref, kp_ref, kc_ref, vp_ref, vc_ref, side_in_ref,
         o_ref, stat_ref, side_out_ref, bias_ref, stat_scr) = refs
    else:
        (table_ref, bmap_ref, q_ref, kc_ref, vc_ref, side_in_ref,
         o_ref, stat_ref, side_out_ref, bias_ref, stat_scr) = refs
    b, j, r = pl.program_id(0), pl.program_id(1), pl.program_id(2)
    n_keys = bias_ref.shape[2]

    side_out_ref[...] = side_in_ref[...].astype(side_out_ref.dtype)

    @pl.when(jnp.logical_and(jnp.logical_and(b == 0, j == 0), r == 0))
    def _():
        stat_scr[...] = jnp.zeros_like(stat_scr)
        bmap = bmap_ref[...]
        in_prev = lax.broadcasted_iota(jnp.int32, bmap.shape, 1) < n_keys - QBLK

        def head(h, carry):
            acc = jnp.full(bmap.shape, NEG_INF, jnp.float32)
            for bucket in range(N_BUCKETS):
                acc = jnp.where(bmap == bucket, table_ref[bucket, h], acc)
            rows_h = pl.ds(pl.multiple_of(h * QBLK, QBLK), QBLK)
            bias_ref[0, rows_h, :] = jnp.where(in_prev, NEG_INF, acc)
            bias_ref[1, rows_h, :] = acc
            return carry

        lax.fori_loop(0, HEADS_PER_GROUP, head, 0)

    lane = lax.broadcasted_iota(jnp.int32, (QBLK, LANES), 1)
    first_head = lane < HEAD_DIM
    bias_copy = jnp.minimum(j, 1)
    rows = pl.ds(r, QBLK, stride=dilation) if dilation > 1 else slice(None)
    for hp in range(HEADS_PER_GROUP // 2):
        sl = slice(hp * HEAD_PAIR, (hp + 1) * HEAD_PAIR)
        qp = q_ref[:, sl]
        zero = jnp.zeros_like(qp)
        q2 = jnp.concatenate([jnp.where(first_head, qp, zero),
                              jnp.where(first_head, zero, qp)], axis=0)
        if has_prev:
            kcat = jnp.concatenate([kp_ref[:, sl], kc_ref[:, sl]], axis=0)
            vcat = jnp.concatenate([vp_ref[:, sl], vc_ref[:, sl]], axis=0)
        else:
            kcat, vcat = kc_ref[:, sl], vc_ref[:, sl]
        s = lax.dot_general(q2, kcat, (((1,), (1,)), ((), ())),
                            preferred_element_type=jnp.float32)
        s = s + bias_ref[bias_copy, hp * 2 * QBLK:(hp + 1) * 2 * QBLK, :]
        m = jnp.max(s, axis=1, keepdims=True)
        p = jnp.exp2(s - m)
        l = jnp.sum(p, axis=1, keepdims=True)
        o2 = jnp.dot(p.astype(jnp.bfloat16), vcat, preferred_element_type=jnp.float32)
        o_ref[hp, rows, :] = jnp.where(first_head, o2[:QBLK], o2[QBLK:])
        for half, head in ((slice(0, QBLK), 2 * hp), (slice(QBLK, 2 * QBLK), 2 * hp + 1)):
            stat_scr[:, head:head + 1] = m[half]
            stat_scr[:, HEADS_PER_GROUP + head:HEADS_PER_GROUP + head + 1] = l[half]
    stat_ref[rows, :] = stat_scr[...]


def _bucket_map(dilation, window, n_keys):
    W = window // dilation
    a = jnp.arange(QBLK, dtype=jnp.int32)[:, None]
    c = jnp.arange(n_keys, dtype=jnp.int32)[None, :]
    m = a + (n_keys - QBLK) - c
    band = (m >= 0) & (m <= W)
    n = jnp.maximum(m, 0) * dilation
    nf = jnp.maximum(n, 1).astype(jnp.float32)
    large = MAX_EXACT + (jnp.log(nf / MAX_EXACT) / math.log(MAX_DISTANCE / MAX_EXACT)
                         * (N_BUCKETS - MAX_EXACT)).astype(jnp.int32)
    large = jnp.minimum(large, N_BUCKETS - 1)
    bucket = jnp.where(n < MAX_EXACT, n, large)
    return jnp.where(band, bucket, -1)


def _attention_group(q, k, v, rel_bias, g, B, S, side):
    window, d = DILATED_PATTERNS[g]
    L = S // d
    n = L // QBLK
    D = D_MODEL
    has_prev = n > 1
    n_keys = 2 * QBLK if has_prev else QBLK
    table = rel_bias[:, g * HEADS_PER_GROUP:(g + 1) * HEADS_PER_GROUP].astype(jnp.float32) * LOG2E
    bmap = _bucket_map(d, window, n_keys)
    blk = (None, None, QBLK, D)
    cur = lambda b, j, r: (b, r, j, 0)
    prev = lambda b, j, r: (b, r, jnp.maximum(j - 1, 0), 0)
    if has_prev:
        kv_specs = [pl.BlockSpec(blk, prev), pl.BlockSpec(blk, cur),
                    pl.BlockSpec(blk, prev), pl.BlockSpec(blk, cur)]
        kv_args = (k, k, v, v)
    else:
        kv_specs = [pl.BlockSpec(blk, cur), pl.BlockSpec(blk, cur)]
        kv_args = (k, v)
    span = QBLK * d
    side_rows = side.shape[0] // (B * n * d)
    side_spec = pl.BlockSpec((side_rows, side.shape[1]), lambda b, j, r: ((b * n + j) * d + r, 0))
    return pl.pallas_call(
        functools.partial(_attn_kernel, dilation=d, has_prev=has_prev),
        grid=(B, n, d),
        in_specs=[pl.BlockSpec(memory_space=pltpu.SMEM),
                  pl.BlockSpec(bmap.shape, lambda b, j, r: (0, 0)),
                  pl.BlockSpec(blk, cur)] + kv_specs + [side_spec],
        out_specs=[pl.BlockSpec((N_SLABS, span, LANES), lambda b, j, r: (0, b * n + j, 0)),
                   pl.BlockSpec((span, LANES), lambda b, j, r: (b * n + j, 0)),
                   side_spec],
        out_shape=[jax.ShapeDtypeStruct((N_SLABS, B * S, LANES), jnp.float32),
                   jax.ShapeDtypeStruct((B * S, LANES), jnp.float32),
                   jax.ShapeDtypeStruct(side.shape, jnp.bfloat16)],
        scratch_shapes=[pltpu.VMEM((2, HEADS_PER_GROUP * QBLK, n_keys), jnp.float32),
                        pltpu.VMEM((QBLK, LANES), jnp.float32)],
        compiler_params=pltpu.CompilerParams(
            dimension_semantics=("arbitrary",) * 3, vmem_limit_bytes=VMEM_LIMIT),
        name=f"dilated_attn_g{g}",
    )(table, bmap, q, *kv_args, side)


def _merge_kernel(o0_ref, o1_ref, o2_ref, s0_ref, s1_ref, s2_ref, h_ref, wo_ref, fn_ref,
                  r2_ref, expand_ref,
                  h_out_ref, hn_ref, route_ref, counts_ref, carry_ref):
    i = pl.program_id(0)
    tm = h_ref.shape[0]

    @pl.when(i == 0)
    def _():
        carry_ref[...] = jnp.zeros_like(carry_ref)

    lane = lax.broadcasted_iota(jnp.int32, (tm, LANES), 1)
    head_lane = lane < HEADS_PER_GROUP
    stats = [s0_ref[...], s1_ref[...], s2_ref[...]]
    dens = [pltpu.roll(st, LANES - HEADS_PER_GROUP, axis=1) for st in stats]
    mx = jnp.maximum(jnp.maximum(stats[0], stats[1]), stats[2])
    es = [jnp.exp2(st - mx) for st in stats]
    inv = 1.0 / (es[0] * dens[0] + es[1] * dens[1] + es[2] * dens[2])
    merged = None
    for e, o_ref in zip(es, (o0_ref, o1_ref, o2_ref)):
        w = jnp.where(head_lane, e * inv, 0.0)
        hi = w.astype(jnp.bfloat16).astype(jnp.float32)
        packed = (hi + pltpu.roll(w - hi, HEADS_PER_GROUP, axis=1)).astype(jnp.bfloat16)
        wide = jnp.dot(packed, expand_ref[...], preferred_element_type=jnp.float32)
        o = jnp.concatenate([o_ref[c] for c in range(N_SLABS)], axis=1)
        term = wide * o
        merged = term if merged is None else merged + term
    h = h_ref[...] + jnp.dot(merged.astype(jnp.bfloat16), wo_ref[...],
                             preferred_element_type=jnp.float32)
    h_out_ref[...] = h
    hn = h * _rms_scale(h) * fn_ref[...]
    hn_ref[...] = hn

    hi, lo = _split_bf16(hn)
    both = jnp.dot(hi, r2_ref[...], preferred_element_type=jnp.float32)
    logits = (both[:, :LANES] + both[:, LANES:]
              + jnp.dot(lo, r2_ref[:, :LANES], preferred_element_type=jnp.float32))
    logits = jnp.where(lane < N_EXPERTS, logits, -jnp.inf)
    v0 = jnp.max(logits, axis=1, keepdims=True)
    e0 = jnp.min(jnp.where(logits == v0, lane, LANES), axis=1, keepdims=True)
    rest = jnp.where(lane == e0, -jnp.inf, logits)
    v1 = jnp.max(rest, axis=1, keepdims=True)
    e1 = jnp.min(jnp.where(rest == v1, lane, LANES), axis=1, keepdims=True)
    t = jnp.exp(v1 - v0)
    g0 = 1.0 / (1.0 + t)
    g1 = t / (1.0 + t)

    hit0 = lane == e0
    hit1 = lane == e1
    onehot = jnp.where(jnp.logical_or(hit0, hit1), 1.0, 0.0)
    r_i = lax.broadcasted_iota(jnp.int32, (tm, tm), 0)
    c_i = lax.broadcasted_iota(jnp.int32, (tm, tm), 1)
    lower = jnp.where(c_i < r_i, 1.0, 0.0).astype(jnp.bfloat16)
    before = jnp.dot(lower, onehot.astype(jnp.bfloat16),
                     preferred_element_type=jnp.float32) + carry_ref[0:1, :]
    rank0 = jnp.sum(jnp.where(hit0, before, 0.0), axis=1, keepdims=True)
    rank1 = jnp.sum(jnp.where(hit1, before, 0.0), axis=1, keepdims=True)
    total = carry_ref[0:1, :] + jnp.sum(onehot, axis=0, keepdims=True)
    carry_ref[...] = jnp.broadcast_to(total, carry_ref.shape)
    counts_ref[...] = jnp.broadcast_to(total, counts_ref.shape)

    route = jnp.zeros((tm, LANES), jnp.float32)
    for ln, val in ((R_E0, e0.astype(jnp.float32)), (R_E1, e1.astype(jnp.float32)),
                    (R_RANK0, rank0), (R_RANK1, rank1), (R_G0, g0), (R_G1, g1)):
        route = jnp.where(lane == ln, val, route)
    route_ref[...] = route


def _merge_layer(outs, lses, h, w_o, ffn_norm, w_router):
    T, D = h.shape
    tm = MERGE_ROWS
    r_pad = jnp.zeros((D, LANES), jnp.float32).at[:, :N_EXPERTS].set(w_router)
    r2 = jnp.concatenate(_split_bf16(r_pad), axis=1)
    head_of_col = jnp.arange(D, dtype=jnp.int32) // HEAD_DIM
    lane_id = jnp.arange(LANES, dtype=jnp.int32)[:, None]
    expand = ((lane_id < 2 * HEADS_PER_GROUP)
              & (lane_id % HEADS_PER_GROUP == head_of_col[None, :])).astype(jnp.bfloat16)
    row = lambda i: (i, 0)
    const = lambda i: (0, 0)
    wide = pl.BlockSpec((tm, D), row)
    narrow = pl.BlockSpec((tm, LANES), row)
    slabs = pl.BlockSpec((N_SLABS, tm, LANES), lambda i: (0, i, 0))
    return pl.pallas_call(
        _merge_kernel,
        grid=(T // tm,),
        in_specs=[slabs, slabs, slabs, narrow, narrow, narrow, wide,
                  pl.BlockSpec((D, D), const), pl.BlockSpec((1, D), const),
                  pl.BlockSpec((D, 2 * LANES), const), pl.BlockSpec((LANES, D), const)],
        out_specs=[wide, wide, narrow, pl.BlockSpec((8, LANES), const)],
        out_shape=[jax.ShapeDtypeStruct((T, D), jnp.float32),
                   jax.ShapeDtypeStruct((T, D), jnp.float32),
                   jax.ShapeDtypeStruct((T, LANES), jnp.float32),
                   jax.ShapeDtypeStruct((8, LANES), jnp.float32)],
        scratch_shapes=[pltpu.VMEM((8, LANES), jnp.float32)],
        compiler_params=pltpu.CompilerParams(
            dimension_semantics=("arbitrary",), vmem_limit_bytes=VMEM_LIMIT),
        name="merge_outproj_router",
    )(*outs, *lses, h, w_o.astype(jnp.bfloat16), ffn_norm.reshape(1, D), r2, expand)


def _dispatch_kernel(dest_ref, pad_start_ref, pad_len_ref, used_ref, hn_ref, xs_ref,
                     zbuf, sem, zsem):
    i = pl.program_id(0)
    tile = hn_ref.shape[0]
    base = i * tile
    n_tiles = xs_ref.shape[0] // MOE_ROWS

    def zero_fill(act):
        for e in range(N_EXPERTS):
            run_start = pad_start_ref[e]
            head = (-run_start) & (SUBLANES - 1)
            for row in range(SUBLANES - 1):
                @pl.when(row < head)
                def _():
                    act(pltpu.make_async_copy(zbuf.at[pl.ds(0, 1), :],
                                              xs_ref.at[pl.ds(run_start + row, 1), :], zsem))
            body_start = run_start + head
            body_len = pad_len_ref[e] - head
            bit = MOE_ROWS // 2
            while bit >= SUBLANES:
                off = pl.multiple_of(body_start + (body_len & -(2 * bit)), SUBLANES)

                @pl.when((body_len & bit) != 0)
                def _():
                    act(pltpu.make_async_copy(zbuf.at[pl.ds(0, bit), :],
                                              xs_ref.at[pl.ds(off, bit), :], zsem))
                bit //= 2
        for t in range(n_tiles):
            @pl.when(t >= used_ref[0])
            def _():
                act(pltpu.make_async_copy(zbuf, xs_ref.at[pl.ds(t * MOE_ROWS, MOE_ROWS), :], zsem))

    @pl.when(i == 0)
    def _():
        zbuf[...] = jnp.zeros_like(zbuf)
        zero_fill(lambda c: c.start())

    def start(t, c):
        for k in range(TOP_K):
            pltpu.make_async_copy(hn_ref.at[pl.ds(t, 1), :],
                                  xs_ref.at[pl.ds(dest_ref[TOP_K * (base + t) + k], 1), :],
                                  sem).start()
        return c

    lax.fori_loop(0, tile, start, 0, unroll=DMA_UNROLL)
    for k in range(TOP_K):
        pltpu.make_async_copy(hn_ref, xs_ref.at[pl.ds(0, tile), :], sem).wait()

    @pl.when(i == 0)
    def _():
        zero_fill(lambda c: c.wait())


def _dispatch(dest_flat, pad_start, pad_len, used, hn, n_slots):
    T, D = hn.shape
    return pl.pallas_call(
        _dispatch_kernel,
        grid_spec=pltpu.PrefetchScalarGridSpec(
            num_scalar_prefetch=4,
            grid=(T // DISPATCH_ROWS,),
            in_specs=[pl.BlockSpec((DISPATCH_ROWS, D), lambda i, *_: (i, 0))],
            out_specs=pl.BlockSpec(memory_space=pl.ANY),
            scratch_shapes=[pltpu.VMEM((MOE_ROWS, D), hn.dtype),
                            pltpu.SemaphoreType.DMA(()), pltpu.SemaphoreType.DMA(())],
        ),
        out_shape=jax.ShapeDtypeStruct((n_slots, D), hn.dtype),
        compiler_params=pltpu.CompilerParams(
            dimension_semantics=("arbitrary",), vmem_limit_bytes=VMEM_LIMIT),
        name="moe_dispatch",
    )(dest_flat, pad_start, pad_len, used, hn)


def _expert_kernel(tile_e_ref, used_ref, x_ref, wg_ref, wu_ref, wd_ref, y_ref, xb_ref, hid_ref):
    del tile_e_ref
    i = pl.program_id(0)
    s = pl.program_id(1)
    n_f, _, tf = hid_ref.shape
    active = i < used_ref[0]

    @pl.when(jnp.logical_and(active, s == 0))
    def _():
        xb_ref[...] = x_ref[...].astype(xb_ref.dtype)

    @pl.when(jnp.logical_and(active, s < n_f))
    def _():
        x = xb_ref[...]
        for c0 in range(0, tf, MOE_CHUNK):
            cols = slice(c0, min(c0 + MOE_CHUNK, tf))
            g = jnp.dot(x, wg_ref[:, cols], preferred_element_type=jnp.float32)
            u = jnp.dot(x, wu_ref[:, cols], preferred_element_type=jnp.float32)
            hid_ref[s, :, cols] = (_silu(g) * u).astype(hid_ref.dtype)

    @pl.when(jnp.logical_and(active, s >= n_f))
    def _():
        hid = jnp.concatenate([hid_ref[c] for c in range(n_f)], axis=1)
        y_ref[...] = jnp.dot(hid, wd_ref[...], preferred_element_type=jnp.float32)

    @pl.when(jnp.logical_and(jnp.logical_not(active), s >= n_f))
    def _():
        y_ref[...] = jnp.zeros_like(y_ref)


def _expert_ffn(tile_e, used, xs, w_gate, w_up, w_down):
    n_slots, D = xs.shape
    F = w_gate.shape[2]
    tm, tf, tn = MOE_ROWS, MOE_FT, MOE_NT
    n_f, n_n = F // tf, D // tn
    n_tiles = n_slots // tm
    last = n_f + n_n - 1

    def eff(i, s, used_ref):
        idle = i >= used_ref[0]
        return jnp.where(idle, used_ref[0] - 1, i), jnp.where(idle, last, s)

    def x_map(i, s, te, us):
        ie, se = eff(i, s, us)
        return jnp.minimum(ie + jnp.minimum(se, 1), us[0] - 1), 0

    def gate_map(i, s, te, us):
        ie, se = eff(i, s, us)
        return te[ie], 0, jnp.minimum(se, n_f - 1)

    def down_map(i, s, te, us):
        ie, se = eff(i, s, us)
        tile = jnp.where(se == 0, jnp.maximum(ie - 1, 0), ie)
        return te[tile], 0, jnp.where(se == 0, n_n - 1, jnp.clip(se - n_f, 0, n_n - 1))

    def out_map(i, s, te, us):
        return i, jnp.clip(s - n_f, 0, n_n - 1)

    return pl.pallas_call(
        _expert_kernel,
        grid_spec=pltpu.PrefetchScalarGridSpec(
            num_scalar_prefetch=2,
            grid=(n_tiles, n_f + n_n),
            in_specs=[
                pl.BlockSpec((tm, D), x_map),
                pl.BlockSpec((None, D, tf), gate_map),
                pl.BlockSpec((None, D, tf), gate_map),
                pl.BlockSpec((None, F, tn), down_map),
            ],
            out_specs=pl.BlockSpec((tm, tn), out_map),
            scratch_shapes=[pltpu.VMEM((tm, D), jnp.bfloat16),
                            pltpu.VMEM((n_f, tm, tf), jnp.bfloat16)],
        ),
        out_shape=jax.ShapeDtypeStruct((n_slots, D), jnp.float32),
        compiler_params=pltpu.CompilerParams(
            dimension_semantics=("arbitrary", "arbitrary"), vmem_limit_bytes=VMEM_LIMIT),
        name="moe_experts",
    )(tile_e, used, xs, w_gate, w_up, w_down)


def _combine_kernel(dest_ref, h_ref, route_ref, fin_ref, ys_ref, o_ref, ybuf, sems):
    i = pl.program_id(0)
    n_steps = pl.num_programs(0)
    tile = h_ref.shape[0]

    def issue(step, slot):
        base = step * tile

        def start(t, c):
            for k in range(TOP_K):
                pltpu.make_async_copy(
                    ys_ref.at[pl.ds(dest_ref[TOP_K * (base + t) + k], 1), :],
                    ybuf.at[slot, k, pl.ds(t, 1), :], sems.at[slot]).start()
            return c

        lax.fori_loop(0, tile, start, 0, unroll=DMA_UNROLL)

    slot = lax.rem(i, 2)

    @pl.when(i == 0)
    def _():
        issue(0, 0)

    @pl.when(i + 1 < n_steps)
    def _():
        issue(i + 1, 1 - slot)

    for k in range(TOP_K):
        pltpu.make_async_copy(ys_ref.at[pl.ds(0, tile), :], ybuf.at[slot, k], sems.at[slot]).wait()
    route = route_ref[...]
    g0 = route[:, R_G0:R_G0 + 1]
    g1 = route[:, R_G1:R_G1 + 1]
    h = h_ref[...] + (g0 * ybuf[slot, 0] + g1 * ybuf[slot, 1])
    o_ref[...] = h * _rms_scale(h) * fin_ref[...]


def _combine(dest_flat, h, route, final_norm, ys):
    T, D = h.shape
    tc = COMBINE_ROWS
    return pl.pallas_call(
        _combine_kernel,
        grid_spec=pltpu.PrefetchScalarGridSpec(
            num_scalar_prefetch=1,
            grid=(T // tc,),
            in_specs=[pl.BlockSpec((tc, D), lambda i, d: (i, 0)),
                      pl.BlockSpec((tc, LANES), lambda i, d: (i, 0)),
                      pl.BlockSpec((1, D), lambda i, d: (0, 0)),
                      pl.BlockSpec(memory_space=pl.ANY)],
            out_specs=pl.BlockSpec((tc, D), lambda i, d: (i, 0)),
            scratch_shapes=[pltpu.VMEM((2, TOP_K, tc, D), jnp.float32),
                            pltpu.SemaphoreType.DMA((2,))],
        ),
        out_shape=jax.ShapeDtypeStruct((T, D), jnp.float32),
        compiler_params=pltpu.CompilerParams(
            dimension_semantics=("arbitrary",), vmem_limit_bytes=VMEM_LIMIT),
        name="moe_combine_final_norm",
    )(dest_flat, h, route, final_norm.reshape(1, D), ys)


def _routing_tables(route, counts_row):
    T = route.shape[0]
    counts = counts_row[0, :N_EXPERTS].astype(jnp.int32)
    padded = ((counts + MOE_ROWS - 1) // MOE_ROWS) * MOE_ROWS
    ends = jnp.cumsum(padded)
    starts = ends - padded
    expert = route[:, R_E0:R_E1 + 1].astype(jnp.int32)
    rank = route[:, R_RANK0:R_RANK1 + 1].astype(jnp.int32)
    onehot = expert[..., None] == jnp.arange(N_EXPERTS, dtype=jnp.int32)
    dest = jnp.sum(jnp.where(onehot, starts, 0), axis=-1) + rank
    n_tiles = (TOP_K * T) // MOE_ROWS + N_EXPERTS
    tile_start = jnp.arange(n_tiles, dtype=jnp.int32) * MOE_ROWS
    tile_e = jnp.minimum(jnp.sum(ends[None, :] <= tile_start[:, None], axis=1),
                         N_EXPERTS - 1).astype(jnp.int32)
    used = (ends[-1:] // MOE_ROWS).astype(jnp.int32)
    pad_start = (starts + counts).astype(jnp.int32)
    pad_len = (padded - counts).astype(jnp.int32)
    return dest.reshape(-1), tile_e, used, pad_start, pad_len, n_tiles * MOE_ROWS


def kernel(x, a_norm, a_proj, a_scale, kv_norm, w_kv, b_norm, w_q, w_o, rel_bias, ffn_norm,
           dense_w_gate, dense_w_up, dense_w_down, moe_router, moe_w_gate, moe_w_up,
           moe_w_down, final_norm):
    B, S, D = x.shape
    T = B * S
    h1, hn1 = _pool_layer(x, a_norm[0], a_proj[0], a_scale[0], ffn_norm[0])
    h2 = _dense_layer(hn1.reshape(T, D), h1.reshape(T, D), dense_w_gate[0],
                      dense_w_up[0], dense_w_down[0])
    E, _, F = moe_w_gate[0].shape
    sides = (moe_w_gate[0].reshape(E * D, F), moe_w_up[0].reshape(E * D, F),
             moe_w_down[0].reshape(E * F, D))
    outs, stats, experts_bf16 = [], [], []
    for g, (q, k, v) in enumerate(_qkv_project(h2, kv_norm, b_norm[0], w_q[0], w_kv, B, S)):
        o, st, w16 = _attention_group(q, k, v, rel_bias, g, B, S, sides[g])
        outs.append(o)
        stats.append(st)
        experts_bf16.append(w16)
    wg16 = experts_bf16[0].reshape(E, D, F)
    wu16 = experts_bf16[1].reshape(E, D, F)
    wd16 = experts_bf16[2].reshape(E, F, D)
    h3, hn3, route, counts = _merge_layer(outs, stats, h2, w_o[0], ffn_norm[1], moe_router[0])
    dest, tile_e, used, pad_start, pad_len, n_slots = _routing_tables(route, counts)
    xs = _dispatch(dest, pad_start, pad_len, used, hn3, n_slots)
    ys = _expert_ffn(tile_e, used, xs, wg16, wu16, wd16)
    out = _combine(dest, h3, route, final_norm, ys)
    return out.reshape(B, S, D)
```

```python
import functools
import math

import jax
import jax.numpy as jnp
from jax import lax
from jax.experimental import pallas as pl
from jax.experimental.pallas import tpu as pltpu

D_MODEL = 1024
EPS = 1e-6
POOL_WINDOWS = (2, 4, 8, 16)
POOL_GROUP_DIM = D_MODEL // len(POOL_WINDOWS)
MAX_POOL_WINDOW = max(POOL_WINDOWS)
HEAD_DIM = 64
HEADS_PER_GROUP = D_MODEL // HEAD_DIM
DILATED_PATTERNS = ((128, 1), (512, 4), (2048, 16))
N_ATT_GROUPS = len(DILATED_PATTERNS)
QBLK = 128
NEG_INF = -1e30
LOG2E = 1.4426950408889634
Q_SCALE = HEAD_DIM ** -0.5 * LOG2E
N_BUCKETS = 32
MAX_EXACT = N_BUCKETS // 2
MAX_DISTANCE = 2048
N_EXPERTS = 8
TOP_K = 2

LANES = 128
SUBLANES = 8
N_SLABS = D_MODEL // LANES
HEAD_PAIR = 2 * HEAD_DIM
assert HEAD_PAIR == LANES

POOL_ROWS = 512
DENSE_ROWS = 512
QKV_ROWS = 512
MERGE_ROWS = 512
MOE_ROWS = 1024
MOE_FT = 1792
MOE_CHUNK = 512
MOE_NT = 512
DISPATCH_ROWS = 1024
COMBINE_ROWS = 512
DMA_UNROLL = 8
VMEM_LIMIT = 60 * 1024 * 1024

R_E0, R_E1, R_RANK0, R_RANK1, R_G0, R_G1 = 0, 1, 2, 3, 4, 5


def _rms_scale(x):
    return lax.rsqrt(jnp.mean(x * x, axis=-1, keepdims=True) + EPS)


def _silu(g):
    return g * (1.0 / (1.0 + jnp.exp(-g)))


def _split_bf16(x):
    hi = x.astype(jnp.bfloat16)
    lo = (x - hi.astype(jnp.float32)).astype(jnp.bfloat16)
    return hi, lo


def _side_cast_specs(side, n_steps):
    rows = side.shape[0] // n_steps
    spec = pl.BlockSpec((rows, side.shape[1]), lambda i, *_: (i, 0))
    return spec, jax.ShapeDtypeStruct(side.shape, jnp.bfloat16)


def _pool_kernel(x_ref, halo_ref, an_ref, proj_ref, asc_ref, fn_ref, h_ref, hn_ref):
    i = pl.program_id(1)
    x = x_ref[...]
    halo = halo_ref[...]
    gain = an_ref[...]
    xn = x * _rms_scale(x) * gain
    hn_halo = halo * _rms_scale(halo) * gain
    hn_halo = jnp.where(i > 0, hn_halo, 0.0)
    full = jnp.concatenate([hn_halo, xn], axis=0)
    ts = x.shape[0]
    pos = i * ts + lax.broadcasted_iota(jnp.int32, (ts, 1), 0)
    outs = []
    for g, w in enumerate(POOL_WINDOWS):
        c0 = g * POOL_GROUP_DIM
        s = full[:, c0:c0 + POOL_GROUP_DIM]
        span = 1
        while span < w:
            s = s + pltpu.roll(s, span, axis=0)
            span *= 2
        s = s[MAX_POOL_WINDOW:, :]
        cnt = jnp.minimum(pos + 1, w).astype(jnp.float32)
        pooled = s / cnt - xn[:, c0:c0 + POOL_GROUP_DIM]
        outs.append(jnp.dot(pooled.astype(jnp.bfloat16), proj_ref[g],
                            preferred_element_type=jnp.float32))
    mix = jnp.concatenate(outs, axis=1) * asc_ref[...]
    h = x + mix
    h_ref[...] = h
    hn_ref[...] = (h * _rms_scale(h) * fn_ref[...]).astype(hn_ref.dtype)


def _pool_layer(x, a_norm, a_proj, a_scale, ffn_norm):
    B, S, D = x.shape
    ts = POOL_ROWS
    halo_blocks = ts // MAX_POOL_WINDOW
    vec = pl.BlockSpec((1, D), lambda b, i: (0, 0))
    return pl.pallas_call(
        _pool_kernel,
        grid=(B, S // ts),
        in_specs=[
            pl.BlockSpec((None, ts, D), lambda b, i: (b, i, 0)),
            pl.BlockSpec((None, MAX_POOL_WINDOW, D),
                         lambda b, i: (b, jnp.maximum(i * halo_blocks - 1, 0), 0)),
            vec,
            pl.BlockSpec(a_proj.shape, lambda b, i: (0, 0, 0)),
            vec, vec,
        ],
        out_specs=[pl.BlockSpec((None, ts, D), lambda b, i: (b, i, 0)),
                   pl.BlockSpec((None, ts, D), lambda b, i: (b, i, 0))],
        out_shape=[jax.ShapeDtypeStruct((B, S, D), jnp.float32),
                   jax.ShapeDtypeStruct((B, S, D), jnp.bfloat16)],
        compiler_params=pltpu.CompilerParams(
            dimension_semantics=("arbitrary", "arbitrary"), vmem_limit_bytes=VMEM_LIMIT),
        name="pool_mixer",
    )(x, x, a_norm.reshape(1, D), a_proj.astype(jnp.bfloat16), a_scale.reshape(1, D),
      ffn_norm.reshape(1, D))


def _dense_kernel(hn_ref, h_ref, wg_ref, wu_ref, wd_ref, side_in_ref, h_out_ref, side_out_ref):
    side_out_ref[...] = side_in_ref[...].astype(side_out_ref.dtype)
    x = hn_ref[...]
    g = jnp.dot(x, wg_ref[...], preferred_element_type=jnp.float32)
    u = jnp.dot(x, wu_ref[...], preferred_element_type=jnp.float32)
    hid = (_silu(g) * u).astype(jnp.bfloat16)
    h_out_ref[...] = h_ref[...] + jnp.dot(hid, wd_ref[...], preferred_element_type=jnp.float32)


def _dense_layer(hn, h, w_gate, w_up, w_down, side):
    T, D = h.shape
    F = w_gate.shape[1]
    tm = DENSE_ROWS
    row = lambda i: (i, 0)
    resident = dict(index_map=lambda i: (0, 0), pipeline_mode=pl.Buffered(1))
    side_spec, side_shape = _side_cast_specs(side, T // tm)
    return pl.pallas_call(
        _dense_kernel,
        grid=(T // tm,),
        in_specs=[
            pl.BlockSpec((tm, D), row),
            pl.BlockSpec((tm, D), row),
            pl.BlockSpec((D, F), **resident),
            pl.BlockSpec((D, F), **resident),
            pl.BlockSpec((F, D), **resident),
            side_spec,
        ],
        out_specs=[pl.BlockSpec((tm, D), row), side_spec],
        out_shape=[jax.ShapeDtypeStruct((T, D), jnp.float32), side_shape],
        compiler_params=pltpu.CompilerParams(
            dimension_semantics=("arbitrary",), vmem_limit_bytes=VMEM_LIMIT),
        name="dense_swiglu",
    )(hn, h, w_gate.astype(jnp.bfloat16), w_up.astype(jnp.bfloat16),
      w_down.astype(jnp.bfloat16), side)


def _qkv_kernel(*refs):
    slabs = refs[:N_SLABS]
    kvg_ref, qg_ref, w_ref, side_in_ref = refs[N_SLABS:N_SLABS + 4]
    out_refs = refs[N_SLABS + 4:-1]
    side_out_ref = refs[-1]
    side_out_ref[...] = side_in_ref[...].astype(side_out_ref.dtype)
    tm = slabs[0].shape[0]
    for g, (_, dilation) in enumerate(DILATED_PATTERNS):
        n = tm // dilation
        cols = []
        for slab in slabs:
            if dilation == 1:
                cols.append(slab[...])
            else:
                cols.append(jnp.concatenate(
                    [slab[pl.ds(r, n, stride=dilation), :] for r in range(dilation)], axis=0))
        x = jnp.concatenate(cols, axis=1)
        xs = x * _rms_scale(x)
        xq = (xs * qg_ref[...]).astype(jnp.bfloat16)
        xkv = (xs * kvg_ref[...]).astype(jnp.bfloat16)
        q = jnp.dot(xq, w_ref[3 * g], preferred_element_type=jnp.float32) * Q_SCALE
        k = jnp.dot(xkv, w_ref[3 * g + 1], preferred_element_type=jnp.float32)
        v = jnp.dot(xkv, w_ref[3 * g + 2], preferred_element_type=jnp.float32)
        for out_ref, val in zip(out_refs[3 * g:3 * g + 3], (q, k, v)):
            for r in range(dilation):
                out_ref[r] = val[r * n:(r + 1) * n].astype(out_ref.dtype)


def _qkv_project(h, kv_norm, b_norm, w_q, w_kv, B, S, side):
    T, D = h.shape
    tm = QKV_ROWS
    tiles_per_seq = S // tm
    blocks = []
    for g in range(N_ATT_GROUPS):
        blocks += [w_q[:, g * D:(g + 1) * D], w_kv[:, g * D:(g + 1) * D],
                   w_kv[:, (N_ATT_GROUPS + g) * D:(N_ATT_GROUPS + g + 1) * D]]
    w = jnp.stack(blocks).astype(jnp.bfloat16)
    slab_specs = [pl.BlockSpec((tm, LANES), functools.partial(lambda i, c: (i, c), c=c))
                  for c in range(N_SLABS)]
    vec = pl.BlockSpec((1, D), lambda i: (0, 0))
    out_specs, out_shapes = [], []
    for _, d in DILATED_PATTERNS:
        spec = pl.BlockSpec((None, d, tm // d, D),
                            lambda i: (i // tiles_per_seq, 0, i % tiles_per_seq, 0))
        out_specs += [spec] * 3
        out_shapes += [jax.ShapeDtypeStruct((B, d, S // d, D), jnp.bfloat16)] * 3
    side_spec, side_shape = _side_cast_specs(side, T // tm)
    outs = pl.pallas_call(
        _qkv_kernel,
        grid=(T // tm,),
        in_specs=slab_specs + [vec, vec,
                               pl.BlockSpec(w.shape, lambda i: (0, 0, 0),
                                            pipeline_mode=pl.Buffered(1)),
                               side_spec],
        out_specs=out_specs + [side_spec],
        out_shape=out_shapes + [side_shape],
        compiler_params=pltpu.CompilerParams(
            dimension_semantics=("arbitrary",), vmem_limit_bytes=VMEM_LIMIT),
        name="qkv_proj",
    )(*([h] * N_SLABS), kv_norm.reshape(1, D), b_norm.reshape(1, D), w, side)
    return [tuple(outs[3 * g:3 * g + 3]) for g in range(N_ATT_GROUPS)], outs[-1]


def _attn_kernel(*refs, dilation, has_prev):
    if has_prev:
        (table_ref, bmap_ref, q_ref, kp_ref, kc_ref, vp_ref, vc_ref,
         o_ref, stat_ref, bias_ref, stat_scr) = refs
    else:
        table_ref, bmap_ref, q_ref, kc_ref, vc_ref, o_ref, stat_ref, bias_ref, stat_scr = refs
    b, j, r = pl.program_id(0), pl.program_id(1), pl.program_id(2)
    n_keys = bias_ref.shape[2]

    @pl.when(jnp.logical_and(jnp.logical_and(b == 0, j == 0), r == 0))
    def _():
        stat_scr[...] = jnp.zeros_like(stat_scr)
        bmap = bmap_ref[...]
        in_prev = lax.broadcasted_iota(jnp.int32, bmap.shape, 1) < n_keys - QBLK

        def head(h, carry):
            acc = jnp.full(bmap.shape, NEG_INF, jnp.float32)
            for bucket in range(N_BUCKETS):
                acc = jnp.where(bmap == bucket, table_ref[bucket, h], acc)
            rows_h = pl.ds(pl.multiple_of(h * QBLK, QBLK), QBLK)
            bias_ref[0, rows_h, :] = jnp.where(in_prev, NEG_INF, acc)
            bias_ref[1, rows_h, :] = acc
            return carry

        lax.fori_loop(0, HEADS_PER_GROUP, head, 0)

    lane = lax.broadcasted_iota(jnp.int32, (QBLK, LANES), 1)
    first_head = lane < HEAD_DIM
    bias_copy = jnp.minimum(j, 1)
    rows = pl.ds(r, QBLK, stride=dilation) if dilation > 1 else slice(None)
    for hp in range(HEADS_PER_GROUP // 2):
        sl = slice(hp * HEAD_PAIR, (hp + 1) * HEAD_PAIR)
        qp = q_ref[:, sl]
        zero = jnp.zeros_like(qp)
        q2 = jnp.concatenate([jnp.where(first_head, qp, zero),
                              jnp.where(first_head, zero, qp)], axis=0)
        if has_prev:
            kcat = jnp.concatenate([kp_ref[:, sl], kc_ref[:, sl]], axis=0)
            vcat = jnp.concatenate([vp_ref[:, sl], vc_ref[:, sl]], axis=0)
        else:
            kcat, vcat = kc_ref[:, sl], vc_ref[:, sl]
        s = lax.dot_general(q2, kcat, (((1,), (1,)), ((), ())),
                            preferred_element_type=jnp.float32)
        s = s + bias_ref[bias_copy, hp * 2 * QBLK:(hp + 1) * 2 * QBLK, :]
        m = jnp.max(s, axis=1, keepdims=True)
        p = jnp.exp2(s - m)
        l = jnp.sum(p, axis=1, keepdims=True)
        o2 = jnp.dot(p.astype(jnp.bfloat16), vcat, preferred_element_type=jnp.float32)
        o_ref[hp, rows, :] = jnp.where(first_head, o2[:QBLK], o2[QBLK:])
        for half, head in ((slice(0, QBLK), 2 * hp), (slice(QBLK, 2 * QBLK), 2 * hp + 1)):
            stat_scr[:, head:head + 1] = m[half]
            stat_scr[:, HEADS_PER_GROUP + head:HEADS_PER_GROUP + head + 1] = l[half]
    stat_ref[rows, :] = stat_scr[...]


def _bucket_map(dilation, window, n_keys):
    W = window // dilation
    a = jnp.arange(QBLK, dtype=jnp.int32)[:, None]
    c = jnp.arange(n_keys, dtype=jnp.int32)[None, :]
    m = a + (n_keys - QBLK) - c
    band = (m >= 0) & (m <= W)
    n = jnp.maximum(m, 0) * dilation
    nf = jnp.maximum(n, 1).astype(jnp.float32)
    large = MAX_EXACT + (jnp.log(nf / MAX_EXACT) / math.log(MAX_DISTANCE / MAX_EXACT)
                         * (N_BUCKETS - MAX_EXACT)).astype(jnp.int32)
    large = jnp.minimum(large, N_BUCKETS - 1)
    bucket = jnp.where(n < MAX_EXACT, n, large)
    return jnp.where(band, bucket, -1)


def _attention_group(q, k, v, rel_bias, g, B, S):
    window, d = DILATED_PATTERNS[g]
    L = S // d
    n = L // QBLK
    D = D_MODEL
    has_prev = n > 1
    n_keys = 2 * QBLK if has_prev else QBLK
    table = rel_bias[:, g * HEADS_PER_GROUP:(g + 1) * HEADS_PER_GROUP].astype(jnp.float32) * LOG2E
    bmap = _bucket_map(d, window, n_keys)
    blk = (None, None, QBLK, D)
    cur = lambda b, j, r: (b, r, j, 0)
    prev = lambda b, j, r: (b, r, jnp.maximum(j - 1, 0), 0)
    if has_prev:
        kv_specs = [pl.BlockSpec(blk, prev), pl.BlockSpec(blk, cur),
                    pl.BlockSpec(blk, prev), pl.BlockSpec(blk, cur)]
        kv_args = (k, k, v, v)
    else:
        kv_specs = [pl.BlockSpec(blk, cur), pl.BlockSpec(blk, cur)]
        kv_args = (k, v)
    span = QBLK * d
    return pl.pallas_call(
        functools.partial(_attn_kernel, dilation=d, has_prev=has_prev),
        grid=(B, n, d),
        in_specs=[pl.BlockSpec(memory_space=pltpu.SMEM),
                  pl.BlockSpec(bmap.shape, lambda b, j, r: (0, 0)),
                  pl.BlockSpec(blk, cur)] + kv_specs,
        out_specs=[pl.BlockSpec((N_SLABS, span, LANES), lambda b, j, r: (0, b * n + j, 0)),
                   pl.BlockSpec((span, LANES), lambda b, j, r: (b * n + j, 0))],
        out_shape=[jax.ShapeDtypeStruct((N_SLABS, B * S, LANES), jnp.float32),
                   jax.ShapeDtypeStruct((B * S, LANES), jnp.float32)],
        scratch_shapes=[pltpu.VMEM((2, HEADS_PER_GROUP * QBLK, n_keys), jnp.float32),
                        pltpu.VMEM((QBLK, LANES), jnp.float32)],
        compiler_params=pltpu.CompilerParams(
            dimension_semantics=("arbitrary",) * 3, vmem_limit_bytes=VMEM_LIMIT),
        name=f"dilated_attn_g{g}",
    )(table, bmap, q, *kv_args)


def _merge_kernel(o0_ref, o1_ref, o2_ref, s0_ref, s1_ref, s2_ref, h_ref, wo_ref, fn_ref,
                  r2_ref, expand_ref,
                  h_out_ref, hn_ref, route_ref, counts_ref, carry_ref):
    i = pl.program_id(0)
    tm = h_ref.shape[0]

    @pl.when(i == 0)
    def _():
        carry_ref[...] = jnp.zeros_like(carry_ref)

    lane = lax.broadcasted_iota(jnp.int32, (tm, LANES), 1)
    head_lane = lane < HEADS_PER_GROUP
    stats = [s0_ref[...], s1_ref[...], s2_ref[...]]
    dens = [pltpu.roll(st, LANES - HEADS_PER_GROUP, axis=1) for st in stats]
    mx = jnp.maximum(jnp.maximum(stats[0], stats[1]), stats[2])
    es = [jnp.exp2(st - mx) for st in stats]
    inv = 1.0 / (es[0] * dens[0] + es[1] * dens[1] + es[2] * dens[2])
    merged = None
    for e, o_ref in zip(es, (o0_ref, o1_ref, o2_ref)):
        w = jnp.where(head_lane, e * inv, 0.0)
        hi = w.astype(jnp.bfloat16).astype(jnp.float32)
        packed = (hi + pltpu.roll(w - hi, HEADS_PER_GROUP, axis=1)).astype(jnp.bfloat16)
        wide = jnp.dot(packed, expand_ref[...], preferred_element_type=jnp.float32)
        o = jnp.concatenate([o_ref[c] for c in range(N_SLABS)], axis=1)
        term = wide * o
        merged = term if merged is None else merged + term
    h = h_ref[...] + jnp.dot(merged.astype(jnp.bfloat16), wo_ref[...],
                             preferred_element_type=jnp.float32)
    h_out_ref[...] = h
    hn = h * _rms_scale(h) * fn_ref[...]
    hn_ref[...] = hn

    hi, lo = _split_bf16(hn)
    both = jnp.dot(hi, r2_ref[...], preferred_element_type=jnp.float32)
    logits = (both[:, :LANES] + both[:, LANES:]
              + jnp.dot(lo, r2_ref[:, :LANES], preferred_element_type=jnp.float32))
    logits = jnp.where(lane < N_EXPERTS, logits, -jnp.inf)
    v0 = jnp.max(logits, axis=1, keepdims=True)
    e0 = jnp.min(jnp.where(logits == v0, lane, LANES), axis=1, keepdims=True)
    rest = jnp.where(lane == e0, -jnp.inf, logits)
    v1 = jnp.max(rest, axis=1, keepdims=True)
    e1 = jnp.min(jnp.where(rest == v1, lane, LANES), axis=1, keepdims=True)
    t = jnp.exp(v1 - v0)
    g0 = 1.0 / (1.0 + t)
    g1 = t / (1.0 + t)

    hit0 = lane == e0
    hit1 = lane == e1
    onehot = jnp.where(jnp.logical_or(hit0, hit1), 1.0, 0.0)
    r_i = lax.broadcasted_iota(jnp.int32, (tm, tm), 0)
    c_i = lax.broadcasted_iota(jnp.int32, (tm, tm), 1)
    lower = jnp.where(c_i < r_i, 1.0, 0.0).astype(jnp.bfloat16)
    before = jnp.dot(lower, onehot.astype(jnp.bfloat16),
                     preferred_element_type=jnp.float32) + carry_ref[0:1, :]
    rank0 = jnp.sum(jnp.where(hit0, before, 0.0), axis=1, keepdims=True)
    rank1 = jnp.sum(jnp.where(hit1, before, 0.0), axis=1, keepdims=True)
    total = carry_ref[0:1, :] + jnp.sum(onehot, axis=0, keepdims=True)
    carry_ref[...] = jnp.broadcast_to(total, carry_ref.shape)
    counts_ref[...] = jnp.broadcast_to(total, counts_ref.shape)

    route = jnp.zeros((tm, LANES), jnp.float32)
    for ln, val in ((R_E0, e0.astype(jnp.float32)), (R_E1, e1.astype(jnp.float32)),
                    (R_RANK0, rank0), (R_RANK1, rank1), (R_G0, g0), (R_G1, g1)):
        route = jnp.where(lane == ln, val, route)
    route_ref[...] = route


def _merge_layer(outs, lses, h, w_o, ffn_norm, w_router):
    T, D = h.shape
    tm = MERGE_ROWS
    r_pad = jnp.zeros((D, LANES), jnp.float32).at[:, :N_EXPERTS].set(w_router)
    r2 = jnp.concatenate(_split_bf16(r_pad), axis=1)
    head_of_col = jnp.arange(D, dtype=jnp.int32) // HEAD_DIM
    lane_id = jnp.arange(LANES, dtype=jnp.int32)[:, None]
    expand = ((lane_id < 2 * HEADS_PER_GROUP)
              & (lane_id % HEADS_PER_GROUP == head_of_col[None, :])).astype(jnp.bfloat16)
    row = lambda i: (i, 0)
    const = lambda i: (0, 0)
    wide = pl.BlockSpec((tm, D), row)
    narrow = pl.BlockSpec((tm, LANES), row)
    slabs = pl.BlockSpec((N_SLABS, tm, LANES), lambda i: (0, i, 0))
    return pl.pallas_call(
        _merge_kernel,
        grid=(T // tm,),
        in_specs=[slabs, slabs, slabs, narrow, narrow, narrow, wide,
                  pl.BlockSpec((D, D), const), pl.BlockSpec((1, D), const),
                  pl.BlockSpec((D, 2 * LANES), const), pl.BlockSpec((LANES, D), const)],
        out_specs=[wide, wide, narrow, pl.BlockSpec((8, LANES), const)],
        out_shape=[jax.ShapeDtypeStruct((T, D), jnp.float32),
                   jax.ShapeDtypeStruct((T, D), jnp.float32),
                   jax.ShapeDtypeStruct((T, LANES), jnp.float32),
                   jax.ShapeDtypeStruct((8, LANES), jnp.float32)],
        scratch_shapes=[pltpu.VMEM((8, LANES), jnp.float32)],
        compiler_params=pltpu.CompilerParams(
            dimension_semantics=("arbitrary",), vmem_limit_bytes=VMEM_LIMIT),
        name="merge_outproj_router",
    )(*outs, *lses, h, w_o.astype(jnp.bfloat16), ffn_norm.reshape(1, D), r2, expand)


def _dispatch_kernel(dest_ref, pad_start_ref, pad_len_ref, used_ref, hn_ref, side_in_ref,
                     xs_ref, side_out_ref, zbuf, sem, zsem):
    side_out_ref[...] = side_in_ref[...].astype(side_out_ref.dtype)
    i = pl.program_id(0)
    tile = hn_ref.shape[0]
    base = i * tile
    n_tiles = xs_ref.shape[0] // MOE_ROWS

    def zero_fill(act):
        for e in range(N_EXPERTS):
            run_start = pad_start_ref[e]
            head = (-run_start) & (SUBLANES - 1)
            for row in range(SUBLANES - 1):
                @pl.when(row < head)
                def _():
                    act(pltpu.make_async_copy(zbuf.at[pl.ds(0, 1), :],
                                              xs_ref.at[pl.ds(run_start + row, 1), :], zsem))
            body_start = run_start + head
            body_len = pad_len_ref[e] - head
            bit = MOE_ROWS // 2
            while bit >= SUBLANES:
                off = pl.multiple_of(body_start + (body_len & -(2 * bit)), SUBLANES)

                @pl.when((body_len & bit) != 0)
                def _():
                    act(pltpu.make_async_copy(zbuf.at[pl.ds(0, bit), :],
                                              xs_ref.at[pl.ds(off, bit), :], zsem))
                bit //= 2
        for t in range(n_tiles):
            @pl.when(t >= used_ref[0])
            def _():
                act(pltpu.make_async_copy(zbuf, xs_ref.at[pl.ds(t * MOE_ROWS, MOE_ROWS), :], zsem))

    @pl.when(i == 0)
    def _():
        zbuf[...] = jnp.zeros_like(zbuf)
        zero_fill(lambda c: c.start())

    def start(t, c):
        for k in range(TOP_K):
            pltpu.make_async_copy(hn_ref.at[pl.ds(t, 1), :],
                                  xs_ref.at[pl.ds(dest_ref[TOP_K * (base + t) + k], 1), :],
                                  sem).start()
        return c

    lax.fori_loop(0, tile, start, 0, unroll=DMA_UNROLL)
    for k in range(TOP_K):
        pltpu.make_async_copy(hn_ref, xs_ref.at[pl.ds(0, tile), :], sem).wait()

    @pl.when(i == 0)
    def _():
        zero_fill(lambda c: c.wait())


def _dispatch(dest_flat, pad_start, pad_len, used, hn, n_slots, side):
    T, D = hn.shape
    side_spec, side_shape = _side_cast_specs(side, T // DISPATCH_ROWS)
    return pl.pallas_call(
        _dispatch_kernel,
        grid_spec=pltpu.PrefetchScalarGridSpec(
            num_scalar_prefetch=4,
            grid=(T // DISPATCH_ROWS,),
            in_specs=[pl.BlockSpec((DISPATCH_ROWS, D), lambda i, *_: (i, 0)), side_spec],
            out_specs=[pl.BlockSpec(memory_space=pl.ANY), side_spec],
            scratch_shapes=[pltpu.VMEM((MOE_ROWS, D), hn.dtype),
                            pltpu.SemaphoreType.DMA(()), pltpu.SemaphoreType.DMA(())],
        ),
        out_shape=[jax.ShapeDtypeStruct((n_slots, D), hn.dtype), side_shape],
        compiler_params=pltpu.CompilerParams(
            dimension_semantics=("arbitrary",), vmem_limit_bytes=VMEM_LIMIT),
        name="moe_dispatch",
    )(dest_flat, pad_start, pad_len, used, hn, side)


def _expert_kernel(tile_e_ref, used_ref, x_ref, wg_ref, wu_ref, wd_ref, y_ref, xb_ref, hid_ref):
    del tile_e_ref
    i = pl.program_id(0)
    s = pl.program_id(1)
    n_f, _, tf = hid_ref.shape
    active = i < used_ref[0]

    @pl.when(jnp.logical_and(active, s == 0))
    def _():
        xb_ref[...] = x_ref[...].astype(xb_ref.dtype)

    @pl.when(jnp.logical_and(active, s < n_f))
    def _():
        x = xb_ref[...]
        for c0 in range(0, tf, MOE_CHUNK):
            cols = slice(c0, min(c0 + MOE_CHUNK, tf))
            g = jnp.dot(x, wg_ref[:, cols], preferred_element_type=jnp.float32)
            u = jnp.dot(x, wu_ref[:, cols], preferred_element_type=jnp.float32)
            hid_ref[s, :, cols] = (_silu(g) * u).astype(hid_ref.dtype)

    @pl.when(jnp.logical_and(active, s >= n_f))
    def _():
        hid = jnp.concatenate([hid_ref[c] for c in range(n_f)], axis=1)
        y_ref[...] = jnp.dot(hid, wd_ref[...], preferred_element_type=jnp.float32)

    @pl.when(jnp.logical_and(jnp.logical_not(active), s >= n_f))
    def _():
        y_ref[...] = jnp.zeros_like(y_ref)


def _expert_ffn(tile_e, used, xs, w_gate, w_up, w_down):
    n_slots, D = xs.shape
    F = w_gate.shape[2]
    tm, tf, tn = MOE_ROWS, MOE_FT, MOE_NT
    n_f, n_n = F // tf, D // tn
    n_tiles = n_slots // tm
    last = n_f + n_n - 1

    def eff(i, s, used_ref):
        idle = i >= used_ref[0]
        return jnp.where(idle, used_ref[0] - 1, i), jnp.where(idle, last, s)

    def x_map(i, s, te, us):
        ie, se = eff(i, s, us)
        return jnp.minimum(ie + jnp.minimum(se, 1), us[0] - 1), 0

    def gate_map(i, s, te, us):
        ie, se = eff(i, s, us)
        return te[ie], 0, jnp.minimum(se, n_f - 1)

    def down_map(i, s, te, us):
        ie, se = eff(i, s, us)
        tile = jnp.where(se == 0, jnp.maximum(ie - 1, 0), ie)
        return te[tile], 0, jnp.where(se == 0, n_n - 1, jnp.clip(se - n_f, 0, n_n - 1))

    def out_map(i, s, te, us):
        return i, jnp.clip(s - n_f, 0, n_n - 1)

    return pl.pallas_call(
        _expert_kernel,
        grid_spec=pltpu.PrefetchScalarGridSpec(
            num_scalar_prefetch=2,
            grid=(n_tiles, n_f + n_n),
            in_specs=[
                pl.BlockSpec((tm, D), x_map),
                pl.BlockSpec((None, D, tf), gate_map),
                pl.BlockSpec((None, D, tf), gate_map),
                pl.BlockSpec((None, F, tn), down_map),
            ],
            out_specs=pl.BlockSpec((tm, tn), out_map),
            scratch_shapes=[pltpu.VMEM((tm, D), jnp.bfloat16),
                            pltpu.VMEM((n_f, tm, tf), jnp.bfloat16)],
        ),
        out_shape=jax.ShapeDtypeStruct((n_slots, D), jnp.float32),
        compiler_params=pltpu.CompilerParams(
            dimension_semantics=("arbitrary", "arbitrary"), vmem_limit_bytes=VMEM_LIMIT),
        name="moe_experts",
    )(tile_e, used, xs, w_gate, w_up, w_down)


def _combine_kernel(dest_ref, h_ref, route_ref, fin_ref, ys_ref, o_ref, ybuf, sems):
    i = pl.program_id(0)
    n_steps = pl.num_programs(0)
    tile = h_ref.shape[0]

    def issue(step, slot):
        base = step * tile

        def start(t, c):
            for k in range(TOP_K):
                pltpu.make_async_copy(
                    ys_ref.at[pl.ds(dest_ref[TOP_K * (base + t) + k], 1), :],
                    ybuf.at[slot, k, pl.ds(t, 1), :], sems.at[slot]).start()
            return c

        lax.fori_loop(0, tile, start, 0, unroll=DMA_UNROLL)

    slot = lax.rem(i, 2)

    @pl.when(i == 0)
    def _():
        issue(0, 0)

    @pl.when(i + 1 < n_steps)
    def _():
        issue(i + 1, 1 - slot)

    for k in range(TOP_K):
        pltpu.make_async_copy(ys_ref.at[pl.ds(0, tile), :], ybuf.at[slot, k], sems.at[slot]).wait()
    route = route_ref[...]
    g0 = route[:, R_G0:R_G0 + 1]
    g1 = route[:, R_G1:R_G1 + 1]
    h = h_ref[...] + (g0 * ybuf[slot, 0] + g1 * ybuf[slot, 1])
    o_ref[...] = h * _rms_scale(h) * fin_ref[...]


def _combine(dest_flat, h, route, final_norm, ys):
    T, D = h.shape
    tc = COMBINE_ROWS
    return pl.pallas_call(
        _combine_kernel,
        grid_spec=pltpu.PrefetchScalarGridSpec(
            num_scalar_prefetch=1,
            grid=(T // tc,),
            in_specs=[pl.BlockSpec((tc, D), lambda i, d: (i, 0)),
                      pl.BlockSpec((tc, LANES), lambda i, d: (i, 0)),
                      pl.BlockSpec((1, D), lambda i, d: (0, 0)),
                      pl.BlockSpec(memory_space=pl.ANY)],
            out_specs=pl.BlockSpec((tc, D), lambda i, d: (i, 0)),
            scratch_shapes=[pltpu.VMEM((2, TOP_K, tc, D), jnp.float32),
                            pltpu.SemaphoreType.DMA((2,))],
        ),
        out_shape=jax.ShapeDtypeStruct((T, D), jnp.float32),
        compiler_params=pltpu.CompilerParams(
            dimension_semantics=("arbitrary",), vmem_limit_bytes=VMEM_LIMIT),
        name="moe_combine_final_norm",
    )(dest_flat, h, route, final_norm.reshape(1, D), ys)


def _routing_tables(route, counts_row):
    T = route.shape[0]
    counts = counts_row[0, :N_EXPERTS].astype(jnp.int32)
    padded = ((counts + MOE_ROWS - 1) // MOE_ROWS) * MOE_ROWS
    ends = jnp.cumsum(padded)
    starts = ends - padded
    expert = route[:, R_E0:R_E1 + 1].astype(jnp.int32)
    rank = route[:, R_RANK0:R_RANK1 + 1].astype(jnp.int32)
    onehot = expert[..., None] == jnp.arange(N_EXPERTS, dtype=jnp.int32)
    dest = jnp.sum(jnp.where(onehot, starts, 0), axis=-1) + rank
    n_tiles = (TOP_K * T) // MOE_ROWS + N_EXPERTS
    tile_start = jnp.arange(n_tiles, dtype=jnp.int32) * MOE_ROWS
    tile_e = jnp.minimum(jnp.sum(ends[None, :] <= tile_start[:, None], axis=1),
                         N_EXPERTS - 1).astype(jnp.int32)
    used = (ends[-1:] // MOE_ROWS).astype(jnp.int32)
    pad_start = (starts + counts).astype(jnp.int32)
    pad_len = (padded - counts).astype(jnp.int32)
    return dest.reshape(-1), tile_e, used, pad_start, pad_len, n_tiles * MOE_ROWS


def kernel(x, a_norm, a_proj, a_scale, kv_norm, w_kv, b_norm, w_q, w_o, rel_bias, ffn_norm,
           dense_w_gate, dense_w_up, dense_w_down, moe_router, moe_w_gate, moe_w_up,
           moe_w_down, final_norm):
    B, S, D = x.shape
    T = B * S
    h1, hn1 = _pool_layer(x, a_norm[0], a_proj[0], a_scale[0], ffn_norm[0])
    E, _, F = moe_w_gate[0].shape
    h2, wg16 = _dense_layer(hn1.reshape(T, D), h1.reshape(T, D), dense_w_gate[0],
                            dense_w_up[0], dense_w_down[0], moe_w_gate[0].reshape(E * D, F))
    qkv, wu16 = _qkv_project(h2, kv_norm, b_norm[0], w_q[0], w_kv, B, S,
                             moe_w_up[0].reshape(E * D, F))
    outs, stats = [], []
    for g, (q, k, v) in enumerate(qkv):
        o, st = _attention_group(q, k, v, rel_bias, g, B, S)
        outs.append(o)
        stats.append(st)
    h3, hn3, route, counts = _merge_layer(outs, stats, h2, w_o[0], ffn_norm[1], moe_router[0])
    dest, tile_e, used, pad_start, pad_len, n_slots = _routing_tables(route, counts)
    xs, wd16 = _dispatch(dest, pad_start, pad_len, used, hn3, n_slots,
                         moe_w_down[0].reshape(E * F, D))
    ys = _expert_ffn(tile_e, used, xs, wg16.reshape(E, D, F), wu16.reshape(E, D, F),
                     wd16.reshape(E, F, D))
    out = _combine(dest, h3, route, final_norm, ys)
    return out.reshape(B, S, D)
```

```python
import functools
import math

import jax
import jax.numpy as jnp
from jax import lax
from jax.experimental import pallas as pl
from jax.experimental.pallas import tpu as pltpu

D_MODEL = 1024
EPS = 1e-6
POOL_WINDOWS = (2, 4, 8, 16)
POOL_GROUP_DIM = D_MODEL // len(POOL_WINDOWS)
MAX_POOL_WINDOW = max(POOL_WINDOWS)
HEAD_DIM = 64
HEADS_PER_GROUP = D_MODEL // HEAD_DIM
DILATED_PATTERNS = ((128, 1), (512, 4), (2048, 16))
N_ATT_GROUPS = len(DILATED_PATTERNS)
QBLK = 128
NEG_INF = -1e30
LOG2E = 1.4426950408889634
Q_SCALE = HEAD_DIM ** -0.5 * LOG2E
N_BUCKETS = 32
MAX_EXACT = N_BUCKETS // 2
MAX_DISTANCE = 2048
N_EXPERTS = 8
TOP_K = 2
SEQ_LEN = 2048

LANES = 128
SUBLANES = 8
N_SLABS = D_MODEL // LANES
HEAD_PAIR = 2 * HEAD_DIM
assert HEAD_PAIR == LANES

LAYER0_ROWS = 512
QKV_ROWS = 512
MERGE_ROWS = 512
MOE_ROWS = 1024
MOE_FT = 1792
MOE_CHUNK = 512
MOE_NT = 1024
DISPATCH_ROWS = 1024
COMBINE_ROWS = 512
DMA_UNROLL = 8
VMEM_LIMIT = 60 * 1024 * 1024

R_E0, R_E1, R_RANK0, R_RANK1, R_G0, R_G1 = 0, 1, 2, 3, 4, 5


def _rms_scale(x):
    return lax.rsqrt(jnp.mean(x * x, axis=-1, keepdims=True) + EPS)


def _silu(g):
    return g * (1.0 / (1.0 + jnp.exp(-g)))


def _split_bf16(x):
    hi = x.astype(jnp.bfloat16)
    lo = (x - hi.astype(jnp.float32)).astype(jnp.bfloat16)
    return hi, lo


def _side_cast_specs(side, n_steps):
    rows = side.shape[0] // n_steps
    spec = pl.BlockSpec((rows, side.shape[1]), lambda i, *_: (i, 0))
    return spec, jax.ShapeDtypeStruct(side.shape, jnp.bfloat16)


def _pool_mix(x, halo, seq_tile, gain, proj_ref, scale):
    xn = x * _rms_scale(x) * gain
    hn_halo = halo * _rms_scale(halo) * gain
    hn_halo = jnp.where(seq_tile > 0, hn_halo, 0.0)
    full = jnp.concatenate([hn_halo, xn], axis=0)
    ts = x.shape[0]
    pos = seq_tile * ts + lax.broadcasted_iota(jnp.int32, (ts, 1), 0)
    outs = []
    for g, w in enumerate(POOL_WINDOWS):
        c0 = g * POOL_GROUP_DIM
        s = full[:, c0:c0 + POOL_GROUP_DIM]
        span = 1
        while span < w:
            s = s + pltpu.roll(s, span, axis=0)
            span *= 2
        s = s[MAX_POOL_WINDOW:, :]
        cnt = jnp.minimum(pos + 1, w).astype(jnp.float32)
        pooled = s / cnt - xn[:, c0:c0 + POOL_GROUP_DIM]
        outs.append(jnp.dot(pooled.astype(jnp.bfloat16), proj_ref[g],
                            preferred_element_type=jnp.float32))
    return x + jnp.concatenate(outs, axis=1) * scale


def _layer0_kernel(*refs, tiles_per_seq, n_side):
    (x_ref, halo_ref, an_ref, proj_ref, asc_ref, fn_ref, wg_ref, wu_ref, wd_ref) = refs[:9]
    side_in = refs[9:9 + n_side]
    h_out_ref = refs[9 + n_side]
    side_out = refs[10 + n_side:]
    for src, dst in zip(side_in, side_out):
        dst[...] = src[...].astype(dst.dtype)
    seq_tile = lax.rem(pl.program_id(0), tiles_per_seq)
    h = _pool_mix(x_ref[...], halo_ref[...], seq_tile, an_ref[...], proj_ref, asc_ref[...])
    hn = (h * _rms_scale(h) * fn_ref[...]).astype(jnp.bfloat16)
    g = jnp.dot(hn, wg_ref[...], preferred_element_type=jnp.float32)
    u = jnp.dot(hn, wu_ref[...], preferred_element_type=jnp.float32)
    hid = (_silu(g) * u).astype(jnp.bfloat16)
    h_out_ref[...] = h + jnp.dot(hid, wd_ref[...], preferred_element_type=jnp.float32)


def _layer0(x, a_norm, a_proj, a_scale, ffn_norm, w_gate, w_up, w_down, sides):
    T, D = x.shape
    F = w_gate.shape[1]
    ts = LAYER0_ROWS
    n_steps = T // ts
    halo_blocks = ts // MAX_POOL_WINDOW
    row = lambda i: (i, 0)
    vec = pl.BlockSpec((1, D), lambda i: (0, 0))
    resident = dict(index_map=lambda i: (0, 0), pipeline_mode=pl.Buffered(1))
    side_specs, side_shapes = zip(*[_side_cast_specs(a, n_steps) for a in sides])
    outs = pl.pallas_call(
        functools.partial(_layer0_kernel, tiles_per_seq=SEQ_LEN // ts, n_side=len(sides)),
        grid=(n_steps,),
        in_specs=[
            pl.BlockSpec((ts, D), row),
            pl.BlockSpec((MAX_POOL_WINDOW, D), lambda i: (jnp.maximum(i * halo_blocks - 1, 0), 0)),
            vec,
            pl.BlockSpec(a_proj.shape, lambda i: (0, 0, 0), pipeline_mode=pl.Buffered(1)),
            vec, vec,
            pl.BlockSpec((D, F), **resident),
            pl.BlockSpec((D, F), **resident),
            pl.BlockSpec((F, D), **resident),
            *side_specs,
        ],
        out_specs=[pl.BlockSpec((ts, D), row), *side_specs],
        out_shape=[jax.ShapeDtypeStruct((T, D), jnp.float32), *side_shapes],
        compiler_params=pltpu.CompilerParams(
            dimension_semantics=("arbitrary",), vmem_limit_bytes=VMEM_LIMIT),
        name="pool_mixer_dense_swiglu",
    )(x, x, a_norm.reshape(1, D), a_proj.astype(jnp.bfloat16), a_scale.reshape(1, D),
      ffn_norm.reshape(1, D), w_gate.astype(jnp.bfloat16), w_up.astype(jnp.bfloat16),
      w_down.astype(jnp.bfloat16), *sides)
    return outs[0], outs[1:]


def _qkv_kernel(*refs):
    slabs = refs[:N_SLABS]
    kvg_ref, qg_ref, wq_ref, wkv_ref, side_in_ref = refs[N_SLABS:N_SLABS + 5]
    out_refs = refs[N_SLABS + 5:-1]
    side_out_ref = refs[-1]
    side_out_ref[...] = side_in_ref[...].astype(side_out_ref.dtype)
    tm = slabs[0].shape[0]
    for g, (_, dilation) in enumerate(DILATED_PATTERNS):
        n = tm // dilation
        cols = []
        for slab in slabs:
            if dilation == 1:
                cols.append(slab[...])
            else:
                cols.append(jnp.concatenate(
                    [slab[pl.ds(r, n, stride=dilation), :] for r in range(dilation)], axis=0))
        x = jnp.concatenate(cols, axis=1)
        xs = x * _rms_scale(x)
        xq = (xs * qg_ref[...]).astype(jnp.bfloat16)
        xkv = (xs * kvg_ref[...]).astype(jnp.bfloat16)
        D = D_MODEL
        kcols = slice(g * D, (g + 1) * D)
        vcols = slice((N_ATT_GROUPS + g) * D, (N_ATT_GROUPS + g + 1) * D)
        q = jnp.dot(xq, wq_ref[:, kcols], preferred_element_type=jnp.float32) * Q_SCALE
        k = jnp.dot(xkv, wkv_ref[:, kcols], preferred_element_type=jnp.float32)
        v = jnp.dot(xkv, wkv_ref[:, vcols], preferred_element_type=jnp.float32)
        for out_ref, val in zip(out_refs[3 * g:3 * g + 3], (q, k, v)):
            for r in range(dilation):
                out_ref[r] = val[r * n:(r + 1) * n].astype(out_ref.dtype)


def _qkv_project(h, kv_norm, b_norm, w_q, w_kv, B, S, side):
    T, D = h.shape
    tm = QKV_ROWS
    tiles_per_seq = S // tm
    resident = dict(index_map=lambda i: (0, 0), pipeline_mode=pl.Buffered(1))
    slab_specs = [pl.BlockSpec((tm, LANES), functools.partial(lambda i, c: (i, c), c=c))
                  for c in range(N_SLABS)]
    vec = pl.BlockSpec((1, D), lambda i: (0, 0))
    out_specs, out_shapes = [], []
    for _, d in DILATED_PATTERNS:
        spec = pl.BlockSpec((None, d, tm // d, D),
                            lambda i: (i // tiles_per_seq, 0, i % tiles_per_seq, 0))
        out_specs += [spec] * 3
        out_shapes += [jax.ShapeDtypeStruct((B, d, S // d, D), jnp.bfloat16)] * 3
    side_spec, side_shape = _side_cast_specs(side, T // tm)
    outs = pl.pallas_call(
        _qkv_kernel,
        grid=(T // tm,),
        in_specs=slab_specs + [vec, vec, pl.BlockSpec(w_q.shape, **resident),
                               pl.BlockSpec(w_kv.shape, **resident), side_spec],
        out_specs=out_specs + [side_spec],
        out_shape=out_shapes + [side_shape],
        compiler_params=pltpu.CompilerParams(
            dimension_semantics=("arbitrary",), vmem_limit_bytes=VMEM_LIMIT),
        name="qkv_proj",
    )(*([h] * N_SLABS), kv_norm.reshape(1, D), b_norm.reshape(1, D), w_q, w_kv, side)
    return [tuple(outs[3 * g:3 * g + 3]) for g in range(N_ATT_GROUPS)], outs[-1]


def _attn_kernel(*refs, dilation, has_prev):
    if has_prev:
        (table_ref, bmap_ref, q_ref, kp_ref, kc_ref, vp_ref, vc_ref,
         o_ref, stat_ref, bias_ref, stat_scr) = refs
    else:
        table_ref, bmap_ref, q_ref, kc_ref, vc_ref, o_ref, stat_ref, bias_ref, stat_scr = refs
    b, j, r = pl.program_id(0), pl.program_id(1), pl.program_id(2)
    n_keys = bias_ref.shape[2]

    @pl.when(jnp.logical_and(jnp.logical_and(b == 0, j == 0), r == 0))
    def _():
        stat_scr[...] = jnp.zeros_like(stat_scr)
        bmap = bmap_ref[...]
        in_prev = lax.broadcasted_iota(jnp.int32, bmap.shape, 1) < n_keys - QBLK

        def head(h, carry):
            acc = jnp.full(bmap.shape, NEG_INF, jnp.float32)
            for bucket in range(N_BUCKETS):
                acc = jnp.where(bmap == bucket, table_ref[bucket, h], acc)
            rows_h = pl.ds(pl.multiple_of(h * QBLK, QBLK), QBLK)
            bias_ref[0, rows_h, :] = jnp.where(in_prev, NEG_INF, acc)
            bias_ref[1, rows_h, :] = acc
            return carry

        lax.fori_loop(0, HEADS_PER_GROUP, head, 0)

    lane = lax.broadcasted_iota(jnp.int32, (QBLK, LANES), 1)
    first_head = lane < HEAD_DIM
    bias_copy = jnp.minimum(j, 1)
    rows = pl.ds(r, QBLK, stride=dilation) if dilation > 1 else slice(None)
    for hp in range(HEADS_PER_GROUP // 2):
        sl = slice(hp * HEAD_PAIR, (hp + 1) * HEAD_PAIR)
        qp = q_ref[:, sl]
        zero = jnp.zeros_like(qp)
        q2 = jnp.concatenate([jnp.where(first_head, qp, zero),
                              jnp.where(first_head, zero, qp)], axis=0)
        if has_prev:
            kcat = jnp.concatenate([kp_ref[:, sl], kc_ref[:, sl]], axis=0)
            vcat = jnp.concatenate([vp_ref[:, sl], vc_ref[:, sl]], axis=0)
        else:
            kcat, vcat = kc_ref[:, sl], vc_ref[:, sl]
        s = lax.dot_general(q2, kcat, (((1,), (1,)), ((), ())),
                            preferred_element_type=jnp.float32)
        s = s + bias_ref[bias_copy, hp * 2 * QBLK:(hp + 1) * 2 * QBLK, :]
        m = jnp.max(s, axis=1, keepdims=True)
        p = jnp.exp2(s - m)
        l = jnp.sum(p, axis=1, keepdims=True)
        o2 = jnp.dot(p.astype(jnp.bfloat16), vcat, preferred_element_type=jnp.float32)
        o_ref[hp, rows, :] = jnp.where(first_head, o2[:QBLK], o2[QBLK:])
        for half, head in ((slice(0, QBLK), 2 * hp), (slice(QBLK, 2 * QBLK), 2 * hp + 1)):
            stat_scr[:, head:head + 1] = m[half]
            stat_scr[:, HEADS_PER_GROUP + head:HEADS_PER_GROUP + head + 1] = l[half]
    stat_ref[rows, :] = stat_scr[...]


def _bucket_map(dilation, window, n_keys):
    W = window // dilation
    a = jnp.arange(QBLK, dtype=jnp.int32)[:, None]
    c = jnp.arange(n_keys, dtype=jnp.int32)[None, :]
    m = a + (n_keys - QBLK) - c
    band = (m >= 0) & (m <= W)
    n = jnp.maximum(m, 0) * dilation
    nf = jnp.maximum(n, 1).astype(jnp.float32)
    large = MAX_EXACT + (jnp.log(nf / MAX_EXACT) / math.log(MAX_DISTANCE / MAX_EXACT)
                         * (N_BUCKETS - MAX_EXACT)).astype(jnp.int32)
    large = jnp.minimum(large, N_BUCKETS - 1)
    bucket = jnp.where(n < MAX_EXACT, n, large)
    return jnp.where(band, bucket, -1)


def _attention_group(q, k, v, rel_bias, g, B, S):
    window, d = DILATED_PATTERNS[g]
    L = S // d
    n = L // QBLK
    D = D_MODEL
    has_prev = n > 1
    n_keys = 2 * QBLK if has_prev else QBLK
    table = rel_bias[:, g * HEADS_PER_GROUP:(g + 1) * HEADS_PER_GROUP].astype(jnp.float32) * LOG2E
    bmap = _bucket_map(d, window, n_keys)
    blk = (None, None, QBLK, D)
    cur = lambda b, j, r: (b, r, j, 0)
    prev = lambda b, j, r: (b, r, jnp.maximum(j - 1, 0), 0)
    if has_prev:
        kv_specs = [pl.BlockSpec(blk, prev), pl.BlockSpec(blk, cur),
                    pl.BlockSpec(blk, prev), pl.BlockSpec(blk, cur)]
        kv_args = (k, k, v, v)
    else:
        kv_specs = [pl.BlockSpec(blk, cur), pl.BlockSpec(blk, cur)]
        kv_args = (k, v)
    span = QBLK * d
    return pl.pallas_call(
        functools.partial(_attn_kernel, dilation=d, has_prev=has_prev),
        grid=(B, n, d),
        in_specs=[pl.BlockSpec(memory_space=pltpu.SMEM),
                  pl.BlockSpec(bmap.shape, lambda b, j, r: (0, 0)),
                  pl.BlockSpec(blk, cur)] + kv_specs,
        out_specs=[pl.BlockSpec((N_SLABS, span, LANES), lambda b, j, r: (0, b * n + j, 0)),
                   pl.BlockSpec((span, LANES), lambda b, j, r: (b * n + j, 0))],
        out_shape=[jax.ShapeDtypeStruct((N_SLABS, B * S, LANES), jnp.float32),
                   jax.ShapeDtypeStruct((B * S, LANES), jnp.float32)],
        scratch_shapes=[pltpu.VMEM((2, HEADS_PER_GROUP * QBLK, n_keys), jnp.float32),
                        pltpu.VMEM((QBLK, LANES), jnp.float32)],
        compiler_params=pltpu.CompilerParams(
            dimension_semantics=("arbitrary",) * 3, vmem_limit_bytes=VMEM_LIMIT),
        name=f"dilated_attn_g{g}",
    )(table, bmap, q, *kv_args)


def _merge_kernel(o0_ref, o1_ref, o2_ref, s0_ref, s1_ref, s2_ref, h_ref, wo_ref, fn_ref,
                  r2_ref, expand_ref,
                  h_out_ref, hn_ref, route_ref, counts_ref, carry_ref):
    i = pl.program_id(0)
    tm = h_ref.shape[0]

    @pl.when(i == 0)
    def _():
        carry_ref[...] = jnp.zeros_like(carry_ref)

    lane = lax.broadcasted_iota(jnp.int32, (tm, LANES), 1)
    head_lane = lane < HEADS_PER_GROUP
    stats = [s0_ref[...], s1_ref[...], s2_ref[...]]
    dens = [pltpu.roll(st, LANES - HEADS_PER_GROUP, axis=1) for st in stats]
    mx = jnp.maximum(jnp.maximum(stats[0], stats[1]), stats[2])
    es = [jnp.exp2(st - mx) for st in stats]
    inv = 1.0 / (es[0] * dens[0] + es[1] * dens[1] + es[2] * dens[2])
    merged = None
    for e, o_ref in zip(es, (o0_ref, o1_ref, o2_ref)):
        w = jnp.where(head_lane, e * inv, 0.0)
        hi = w.astype(jnp.bfloat16).astype(jnp.float32)
        packed = (hi + pltpu.roll(w - hi, HEADS_PER_GROUP, axis=1)).astype(jnp.bfloat16)
        wide = jnp.dot(packed, expand_ref[...], preferred_element_type=jnp.float32)
        o = jnp.concatenate([o_ref[c] for c in range(N_SLABS)], axis=1)
        term = wide * o
        merged = term if merged is None else merged + term
    h = h_ref[...] + jnp.dot(merged.astype(jnp.bfloat16), wo_ref[...],
                             preferred_element_type=jnp.float32)
    h_out_ref[...] = h
    hn = h * _rms_scale(h) * fn_ref[...]
    hn_ref[...] = hn

    hi, lo = _split_bf16(hn)
    both = jnp.dot(hi, r2_ref[...], preferred_element_type=jnp.float32)
    logits = (both[:, :LANES] + both[:, LANES:]
              + jnp.dot(lo, r2_ref[:, :LANES], preferred_element_type=jnp.float32))
    logits = jnp.where(lane < N_EXPERTS, logits, -jnp.inf)
    v0 = jnp.max(logits, axis=1, keepdims=True)
    e0 = jnp.min(jnp.where(logits == v0, lane, LANES), axis=1, keepdims=True)
    rest = jnp.where(lane == e0, -jnp.inf, logits)
    v1 = jnp.max(rest, axis=1, keepdims=True)
    e1 = jnp.min(jnp.where(rest == v1, lane, LANES), axis=1, keepdims=True)
    t = jnp.exp(v1 - v0)
    g0 = 1.0 / (1.0 + t)
    g1 = t / (1.0 + t)

    hit0 = lane == e0
    hit1 = lane == e1
    onehot = jnp.where(jnp.logical_or(hit0, hit1), 1.0, 0.0)
    r_i = lax.broadcasted_iota(jnp.int32, (tm, tm), 0)
    c_i = lax.broadcasted_iota(jnp.int32, (tm, tm), 1)
    lower = jnp.where(c_i < r_i, 1.0, 0.0).astype(jnp.bfloat16)
    before = jnp.dot(lower, onehot.astype(jnp.bfloat16),
                     preferred_element_type=jnp.float32) + carry_ref[0:1, :]
    rank0 = jnp.sum(jnp.where(hit0, before, 0.0), axis=1, keepdims=True)
    rank1 = jnp.sum(jnp.where(hit1, before, 0.0), axis=1, keepdims=True)
    total = carry_ref[0:1, :] + jnp.sum(onehot, axis=0, keepdims=True)
    carry_ref[...] = jnp.broadcast_to(total, carry_ref.shape)
    counts_ref[...] = jnp.broadcast_to(total, counts_ref.shape)

    route = jnp.zeros((tm, LANES), jnp.float32)
    for ln, val in ((R_E0, e0.astype(jnp.float32)), (R_E1, e1.astype(jnp.float32)),
                    (R_RANK0, rank0), (R_RANK1, rank1), (R_G0, g0), (R_G1, g1)):
        route = jnp.where(lane == ln, val, route)
    route_ref[...] = route


def _merge_layer(outs, lses, h, w_o, ffn_norm, w_router):
    T, D = h.shape
    tm = MERGE_ROWS
    r_pad = jnp.zeros((D, LANES), jnp.float32).at[:, :N_EXPERTS].set(w_router)
    r2 = jnp.concatenate(_split_bf16(r_pad), axis=1)
    head_of_col = jnp.arange(D, dtype=jnp.int32) // HEAD_DIM
    lane_id = jnp.arange(LANES, dtype=jnp.int32)[:, None]
    expand = ((lane_id < 2 * HEADS_PER_GROUP)
              & (lane_id % HEADS_PER_GROUP == head_of_col[None, :])).astype(jnp.bfloat16)
    row = lambda i: (i, 0)
    const = lambda i: (0, 0)
    wide = pl.BlockSpec((tm, D), row)
    narrow = pl.BlockSpec((tm, LANES), row)
    slabs = pl.BlockSpec((N_SLABS, tm, LANES), lambda i: (0, i, 0))
    return pl.pallas_call(
        _merge_kernel,
        grid=(T // tm,),
        in_specs=[slabs, slabs, slabs, narrow, narrow, narrow, wide,
                  pl.BlockSpec((D, D), const), pl.BlockSpec((1, D), const),
                  pl.BlockSpec((D, 2 * LANES), const), pl.BlockSpec((LANES, D), const)],
        out_specs=[wide, wide, narrow, pl.BlockSpec((8, LANES), const)],
        out_shape=[jax.ShapeDtypeStruct((T, D), jnp.float32),
                   jax.ShapeDtypeStruct((T, D), jnp.float32),
                   jax.ShapeDtypeStruct((T, LANES), jnp.float32),
                   jax.ShapeDtypeStruct((8, LANES), jnp.float32)],
        scratch_shapes=[pltpu.VMEM((8, LANES), jnp.float32)],
        compiler_params=pltpu.CompilerParams(
            dimension_semantics=("arbitrary",), vmem_limit_bytes=VMEM_LIMIT),
        name="merge_outproj_router",
    )(*outs, *lses, h, w_o, ffn_norm.reshape(1, D), r2, expand)


def _dispatch_kernel(dest_ref, pad_start_ref, pad_len_ref, used_ref, hn_ref, side_in_ref,
                     xs_ref, side_out_ref, zbuf, sem, zsem):
    side_out_ref[...] = side_in_ref[...].astype(side_out_ref.dtype)
    i = pl.program_id(0)
    tile = hn_ref.shape[0]
    base = i * tile
    n_tiles = xs_ref.shape[0] // MOE_ROWS

    def zero_fill(act):
        for e in range(N_EXPERTS):
            run_start = pad_start_ref[e]
            head = (-run_start) & (SUBLANES - 1)
            for row in range(SUBLANES - 1):
                @pl.when(row < head)
                def _():
                    act(pltpu.make_async_copy(zbuf.at[pl.ds(0, 1), :],
                                              xs_ref.at[pl.ds(run_start + row, 1), :], zsem))
            body_start = run_start + head
            body_len = pad_len_ref[e] - head
            bit = MOE_ROWS // 2
            while bit >= SUBLANES:
                off = pl.multiple_of(body_start + (body_len & -(2 * bit)), SUBLANES)

                @pl.when((body_len & bit) != 0)
                def _():
                    act(pltpu.make_async_copy(zbuf.at[pl.ds(0, bit), :],
                                              xs_ref.at[pl.ds(off, bit), :], zsem))
                bit //= 2
        for t in range(n_tiles):
            @pl.when(t >= used_ref[0])
            def _():
                act(pltpu.make_async_copy(zbuf, xs_ref.at[pl.ds(t * MOE_ROWS, MOE_ROWS), :], zsem))

    @pl.when(i == 0)
    def _():
        zbuf[...] = jnp.zeros_like(zbuf)
        zero_fill(lambda c: c.start())

    def start(t, c):
        for k in range(TOP_K):
            pltpu.make_async_copy(hn_ref.at[pl.ds(t, 1), :],
                                  xs_ref.at[pl.ds(dest_ref[TOP_K * (base + t) + k], 1), :],
                                  sem).start()
        return c

    lax.fori_loop(0, tile, start, 0, unroll=DMA_UNROLL)
    for k in range(TOP_K):
        pltpu.make_async_copy(hn_ref, xs_ref.at[pl.ds(0, tile), :], sem).wait()

    @pl.when(i == 0)
    def _():
        zero_fill(lambda c: c.wait())


def _dispatch(dest_flat, pad_start, pad_len, used, hn, n_slots, side):
    T, D = hn.shape
    side_spec, side_shape = _side_cast_specs(side, T // DISPATCH_ROWS)
    return pl.pallas_call(
        _dispatch_kernel,
        grid_spec=pltpu.PrefetchScalarGridSpec(
            num_scalar_prefetch=4,
            grid=(T // DISPATCH_ROWS,),
            in_specs=[pl.BlockSpec((DISPATCH_ROWS, D), lambda i, *_: (i, 0)), side_spec],
            out_specs=[pl.BlockSpec(memory_space=pl.ANY), side_spec],
            scratch_shapes=[pltpu.VMEM((MOE_ROWS, D), hn.dtype),
                            pltpu.SemaphoreType.DMA(()), pltpu.SemaphoreType.DMA(())],
        ),
        out_shape=[jax.ShapeDtypeStruct((n_slots, D), hn.dtype), side_shape],
        compiler_params=pltpu.CompilerParams(
            dimension_semantics=("arbitrary",), vmem_limit_bytes=VMEM_LIMIT),
        name="moe_dispatch",
    )(dest_flat, pad_start, pad_len, used, hn, side)


def _expert_kernel(tile_e_ref, used_ref, x_ref, wg_ref, wu_ref, wd_ref, y_ref, xb_ref, hid_ref):
    del tile_e_ref
    i = pl.program_id(0)
    s = pl.program_id(1)
    n_f, _, tf = hid_ref.shape
    active = i < used_ref[0]

    @pl.when(jnp.logical_and(active, s == 0))
    def _():
        xb_ref[...] = x_ref[...].astype(xb_ref.dtype)

    @pl.when(jnp.logical_and(active, s < n_f))
    def _():
        x = xb_ref[...]
        for c0 in range(0, tf, MOE_CHUNK):
            cols = slice(c0, min(c0 + MOE_CHUNK, tf))
            g = jnp.dot(x, wg_ref[:, cols], preferred_element_type=jnp.float32)
            u = jnp.dot(x, wu_ref[:, cols], preferred_element_type=jnp.float32)
            hid_ref[s, :, cols] = (_silu(g) * u).astype(hid_ref.dtype)

    @pl.when(jnp.logical_and(active, s >= n_f))
    def _():
        hid = jnp.concatenate([hid_ref[c] for c in range(n_f)], axis=1)
        y_ref[...] = jnp.dot(hid, wd_ref[...], preferred_element_type=jnp.float32)

    @pl.when(jnp.logical_and(jnp.logical_not(active), s >= n_f))
    def _():
        y_ref[...] = jnp.zeros_like(y_ref)


def _expert_ffn(tile_e, used, xs, w_gate, w_up, w_down):
    n_slots, D = xs.shape
    F = w_gate.shape[2]
    tm, tf, tn = MOE_ROWS, MOE_FT, MOE_NT
    n_f, n_n = F // tf, D // tn
    n_tiles = n_slots // tm
    last = n_f + n_n - 1

    def eff(i, s, used_ref):
        idle = i >= used_ref[0]
        return jnp.where(idle, used_ref[0] - 1, i), jnp.where(idle, last, s)

    def x_map(i, s, te, us):
        ie, se = eff(i, s, us)
        return jnp.minimum(ie + jnp.minimum(se, 1), us[0] - 1), 0

    def gate_map(i, s, te, us):
        ie, se = eff(i, s, us)
        return te[ie], 0, jnp.minimum(se, n_f - 1)

    def down_map(i, s, te, us):
        ie, se = eff(i, s, us)
        tile = jnp.where(se == 0, jnp.maximum(ie - 1, 0), ie)
        return te[tile], 0, jnp.where(se == 0, n_n - 1, jnp.clip(se - n_f, 0, n_n - 1))

    def out_map(i, s, te, us):
        return i, jnp.clip(s - n_f, 0, n_n - 1)

    return pl.pallas_call(
        _expert_kernel,
        grid_spec=pltpu.PrefetchScalarGridSpec(
            num_scalar_prefetch=2,
            grid=(n_tiles, n_f + n_n),
            in_specs=[
                pl.BlockSpec((tm, D), x_map),
                pl.BlockSpec((None, D, tf), gate_map),
                pl.BlockSpec((None, D, tf), gate_map),
                pl.BlockSpec((None, F, tn), down_map),
            ],
            out_specs=pl.BlockSpec((tm, tn), out_map),
            scratch_shapes=[pltpu.VMEM((tm, D), jnp.bfloat16),
                            pltpu.VMEM((n_f, tm, tf), jnp.bfloat16)],
        ),
        out_shape=jax.ShapeDtypeStruct((n_slots, D), jnp.float32),
        compiler_params=pltpu.CompilerParams(
            dimension_semantics=("arbitrary", "arbitrary"), vmem_limit_bytes=VMEM_LIMIT),
        name="moe_experts",
    )(tile_e, used, xs, w_gate, w_up, w_down)


def _combine_kernel(dest_ref, h_ref, route_ref, fin_ref, ys_ref, o_ref, ybuf, sems):
    i = pl.program_id(0)
    n_steps = pl.num_programs(0)
    tile = h_ref.shape[0]

    def issue(step, slot):
        base = step * tile

        def start(t, c):
            for k in range(TOP_K):
                pltpu.make_async_copy(
                    ys_ref.at[pl.ds(dest_ref[TOP_K * (base + t) + k], 1), :],
                    ybuf.at[slot, k, pl.ds(t, 1), :], sems.at[slot]).start()
            return c

        lax.fori_loop(0, tile, start, 0, unroll=DMA_UNROLL)

    slot = lax.rem(i, 2)

    @pl.when(i == 0)
    def _():
        issue(0, 0)

    @pl.when(i + 1 < n_steps)
    def _():
        issue(i + 1, 1 - slot)

    for k in range(TOP_K):
        pltpu.make_async_copy(ys_ref.at[pl.ds(0, tile), :], ybuf.at[slot, k], sems.at[slot]).wait()
    route = route_ref[...]
    g0 = route[:, R_G0:R_G0 + 1]
    g1 = route[:, R_G1:R_G1 + 1]
    h = h_ref[...] + (g0 * ybuf[slot, 0] + g1 * ybuf[slot, 1])
    o_ref[...] = h * _rms_scale(h) * fin_ref[...]


def _combine(dest_flat, h, route, final_norm, ys):
    T, D = h.shape
    tc = COMBINE_ROWS
    return pl.pallas_call(
        _combine_kernel,
        grid_spec=pltpu.PrefetchScalarGridSpec(
            num_scalar_prefetch=1,
            grid=(T // tc,),
            in_specs=[pl.BlockSpec((tc, D), lambda i, d: (i, 0)),
                      pl.BlockSpec((tc, LANES), lambda i, d: (i, 0)),
                      pl.BlockSpec((1, D), lambda i, d: (0, 0)),
                      pl.BlockSpec(memory_space=pl.ANY)],
            out_specs=pl.BlockSpec((tc, D), lambda i, d: (i, 0)),
            scratch_shapes=[pltpu.VMEM((2, TOP_K, tc, D), jnp.float32),
                            pltpu.SemaphoreType.DMA((2,))],
        ),
        out_shape=jax.ShapeDtypeStruct((T, D), jnp.float32),
        compiler_params=pltpu.CompilerParams(
            dimension_semantics=("arbitrary",), vmem_limit_bytes=VMEM_LIMIT),
        name="moe_combine_final_norm",
    )(dest_flat, h, route, final_norm.reshape(1, D), ys)


def _routing_tables(route, counts_row):
    T = route.shape[0]
    counts = counts_row[0, :N_EXPERTS].astype(jnp.int32)
    padded = ((counts + MOE_ROWS - 1) // MOE_ROWS) * MOE_ROWS
    ends = jnp.cumsum(padded)
    starts = ends - padded
    expert = route[:, R_E0:R_E1 + 1].astype(jnp.int32)
    rank = route[:, R_RANK0:R_RANK1 + 1].astype(jnp.int32)
    onehot = expert[..., None] == jnp.arange(N_EXPERTS, dtype=jnp.int32)
    dest = jnp.sum(jnp.where(onehot, starts, 0), axis=-1) + rank
    n_tiles = (TOP_K * T) // MOE_ROWS + N_EXPERTS
    tile_start = jnp.arange(n_tiles, dtype=jnp.int32) * MOE_ROWS
    tile_e = jnp.minimum(jnp.sum(ends[None, :] <= tile_start[:, None], axis=1),
                         N_EXPERTS - 1).astype(jnp.int32)
    used = (ends[-1:] // MOE_ROWS).astype(jnp.int32)
    pad_start = (starts + counts).astype(jnp.int32)
    pad_len = (padded - counts).astype(jnp.int32)
    return dest.reshape(-1), tile_e, used, pad_start, pad_len, n_tiles * MOE_ROWS


def kernel(x, a_norm, a_proj, a_scale, kv_norm, w_kv, b_norm, w_q, w_o, rel_bias, ffn_norm,
           dense_w_gate, dense_w_up, dense_w_down, moe_router, moe_w_gate, moe_w_up,
           moe_w_down, final_norm):
    B, S, D = x.shape
    T = B * S
    E, _, F = moe_w_gate[0].shape
    h2, (wg16, wq16, wkv16, wo16) = _layer0(
        x.reshape(T, D), a_norm[0], a_proj[0], a_scale[0], ffn_norm[0],
        dense_w_gate[0], dense_w_up[0], dense_w_down[0],
        (moe_w_gate[0].reshape(E * D, F), w_q[0], w_kv, w_o[0]))
    qkv, wu16 = _qkv_project(h2, kv_norm, b_norm[0], wq16, wkv16, B, S,
                             moe_w_up[0].reshape(E * D, F))
    outs, stats = [], []
    for g, (q, k, v) in enumerate(qkv):
        o, st = _attention_group(q, k, v, rel_bias, g, B, S)
        outs.append(o)
        stats.append(st)
    h3, hn3, route, counts = _merge_layer(outs, stats, h2, wo16, ffn_norm[1], moe_router[0])
    dest, tile_e, used, pad_start, pad_len, n_slots = _routing_tables(route, counts)
    xs, wd16 = _dispatch(dest, pad_start, pad_len, used, hn3, n_slots,
                         moe_w_down[0].reshape(E * F, D))
    ys = _expert_ffn(tile_e, used, xs, wg16.reshape(E, D, F), wu16.reshape(E, D, F),
                     wd16.reshape(E, F, D))
    out = _combine(dest, h3, route, final_norm, ys)
    return out.reshape(B, S, D)
```

```python
import functools
import math

import jax
import jax.numpy as jnp
from jax import lax
from jax.experimental import pallas as pl
from jax.experimental.pallas import tpu as pltpu

D_MODEL = 1024
EPS = 1e-6
POOL_WINDOWS = (2, 4, 8, 16)
POOL_GROUP_DIM = D_MODEL // len(POOL_WINDOWS)
MAX_POOL_WINDOW = max(POOL_WINDOWS)
HEAD_DIM = 64
HEADS_PER_GROUP = D_MODEL // HEAD_DIM
DILATED_PATTERNS = ((128, 1), (512, 4), (2048, 16))
N_ATT_GROUPS = len(DILATED_PATTERNS)
QBLK = 128
NEG_INF = -1e30
LOG2E = 1.4426950408889634
Q_SCALE = HEAD_DIM ** -0.5 * LOG2E
N_BUCKETS = 32
MAX_EXACT = N_BUCKETS // 2
MAX_DISTANCE = 2048
N_EXPERTS = 8
TOP_K = 2
SEQ_LEN = 2048

LANES = 128
SUBLANES = 8
N_SLABS = D_MODEL // LANES
HEAD_PAIR = 2 * HEAD_DIM
assert HEAD_PAIR == LANES

LAYER0_ROWS = 512
QKV_ROWS = 512
MERGE_ROWS = 512
MOE_ROWS = 1024
MOE_FT = 1792
MOE_CHUNK = 512
MOE_SUBROWS = 256
MOE_NT = 1024
DISPATCH_ROWS = 1024
COMBINE_ROWS = 512
DMA_UNROLL = 8
VMEM_LIMIT = 60 * 1024 * 1024

R_E0, R_E1, R_RANK0, R_RANK1, R_G0, R_G1 = 0, 1, 2, 3, 4, 5


def _rms_scale(x):
    return lax.rsqrt(jnp.mean(x * x, axis=-1, keepdims=True) + EPS)


def _silu(g):
    return g * (1.0 / (1.0 + jnp.exp(-g)))


def _split_bf16(x):
    hi = x.astype(jnp.bfloat16)
    lo = (x - hi.astype(jnp.float32)).astype(jnp.bfloat16)
    return hi, lo


def _side_cast_specs(side, n_steps):
    rows = side.shape[0] // n_steps
    spec = pl.BlockSpec((rows, side.shape[1]), lambda i, *_: (i, 0))
    return spec, jax.ShapeDtypeStruct(side.shape, jnp.bfloat16)


def _pool_mix(x, halo, seq_tile, gain, proj_ref, scale):
    xn = x * _rms_scale(x) * gain
    hn_halo = halo * _rms_scale(halo) * gain
    hn_halo = jnp.where(seq_tile > 0, hn_halo, 0.0)
    full = jnp.concatenate([hn_halo, xn], axis=0)
    ts = x.shape[0]
    pos = seq_tile * ts + lax.broadcasted_iota(jnp.int32, (ts, 1), 0)
    outs = []
    for g, w in enumerate(POOL_WINDOWS):
        c0 = g * POOL_GROUP_DIM
        s = full[:, c0:c0 + POOL_GROUP_DIM]
        span = 1
        while span < w:
            s = s + pltpu.roll(s, span, axis=0)
            span *= 2
        s = s[MAX_POOL_WINDOW:, :]
        cnt = jnp.minimum(pos + 1, w).astype(jnp.float32)
        pooled = s / cnt - xn[:, c0:c0 + POOL_GROUP_DIM]
        outs.append(jnp.dot(pooled.astype(jnp.bfloat16), proj_ref[g],
                            preferred_element_type=jnp.float32))
    return x + jnp.concatenate(outs, axis=1) * scale


def _layer0_kernel(*refs, tiles_per_seq, n_side):
    (x_ref, halo_ref, an_ref, proj_ref, asc_ref, fn_ref, wg_ref, wu_ref, wd_ref) = refs[:9]
    side_in = refs[9:9 + n_side]
    h_out_ref = refs[9 + n_side]
    side_out = refs[10 + n_side:]
    for src, dst in zip(side_in, side_out):
        dst[...] = src[...].astype(dst.dtype)
    seq_tile = lax.rem(pl.program_id(0), tiles_per_seq)
    h = _pool_mix(x_ref[...], halo_ref[...], seq_tile, an_ref[...], proj_ref, asc_ref[...])
    hn = (h * _rms_scale(h) * fn_ref[...]).astype(jnp.bfloat16)
    g = jnp.dot(hn, wg_ref[...], preferred_element_type=jnp.float32)
    u = jnp.dot(hn, wu_ref[...], preferred_element_type=jnp.float32)
    hid = (_silu(g) * u).astype(jnp.bfloat16)
    h_out_ref[...] = h + jnp.dot(hid, wd_ref[...], preferred_element_type=jnp.float32)


def _layer0(x, a_norm, a_proj, a_scale, ffn_norm, w_gate, w_up, w_down, sides):
    T, D = x.shape
    F = w_gate.shape[1]
    ts = LAYER0_ROWS
    n_steps = T // ts
    halo_blocks = ts // MAX_POOL_WINDOW
    row = lambda i: (i, 0)
    vec = pl.BlockSpec((1, D), lambda i: (0, 0))
    resident = dict(index_map=lambda i: (0, 0), pipeline_mode=pl.Buffered(1))
    side_specs, side_shapes = zip(*[_side_cast_specs(a, n_steps) for a in sides])
    outs = pl.pallas_call(
        functools.partial(_layer0_kernel, tiles_per_seq=SEQ_LEN // ts, n_side=len(sides)),
        grid=(n_steps,),
        in_specs=[
            pl.BlockSpec((ts, D), row),
            pl.BlockSpec((MAX_POOL_WINDOW, D), lambda i: (jnp.maximum(i * halo_blocks - 1, 0), 0)),
            vec,
            pl.BlockSpec(a_proj.shape, lambda i: (0, 0, 0), pipeline_mode=pl.Buffered(1)),
            vec, vec,
            pl.BlockSpec((D, F), **resident),
            pl.BlockSpec((D, F), **resident),
            pl.BlockSpec((F, D), **resident),
            *side_specs,
        ],
        out_specs=[pl.BlockSpec((ts, D), row), *side_specs],
        out_shape=[jax.ShapeDtypeStruct((T, D), jnp.float32), *side_shapes],
        compiler_params=pltpu.CompilerParams(
            dimension_semantics=("arbitrary",), vmem_limit_bytes=VMEM_LIMIT),
        name="pool_mixer_dense_swiglu",
    )(x, x, a_norm.reshape(1, D), a_proj.astype(jnp.bfloat16), a_scale.reshape(1, D),
      ffn_norm.reshape(1, D), w_gate.astype(jnp.bfloat16), w_up.astype(jnp.bfloat16),
      w_down.astype(jnp.bfloat16), *sides)
    return outs[0], outs[1:]


def _qkv_kernel(*refs):
    slabs = refs[:N_SLABS]
    kvg_ref, qg_ref, wq_ref, wkv_ref, side_in_ref = refs[N_SLABS:N_SLABS + 5]
    out_refs = refs[N_SLABS + 5:-1]
    side_out_ref = refs[-1]
    side_out_ref[...] = side_in_ref[...].astype(side_out_ref.dtype)
    tm = slabs[0].shape[0]
    for g, (_, dilation) in enumerate(DILATED_PATTERNS):
        n = tm // dilation
        cols = []
        for slab in slabs:
            if dilation == 1:
                cols.append(slab[...])
            else:
                cols.append(jnp.concatenate(
                    [slab[pl.ds(r, n, stride=dilation), :] for r in range(dilation)], axis=0))
        x = jnp.concatenate(cols, axis=1)
        xs = x * _rms_scale(x)
        xq = (xs * qg_ref[...]).astype(jnp.bfloat16)
        xkv = (xs * kvg_ref[...]).astype(jnp.bfloat16)
        D = D_MODEL
        kcols = slice(g * D, (g + 1) * D)
        vcols = slice((N_ATT_GROUPS + g) * D, (N_ATT_GROUPS + g + 1) * D)
        q = jnp.dot(xq, wq_ref[:, kcols], preferred_element_type=jnp.float32) * Q_SCALE
        k = jnp.dot(xkv, wkv_ref[:, kcols], preferred_element_type=jnp.float32)
        v = jnp.dot(xkv, wkv_ref[:, vcols], preferred_element_type=jnp.float32)
        for out_ref, val in zip(out_refs[3 * g:3 * g + 3], (q, k, v)):
            for r in range(dilation):
                out_ref[r] = val[r * n:(r + 1) * n].astype(out_ref.dtype)


def _qkv_project(h, kv_norm, b_norm, w_q, w_kv, B, S, side):
    T, D = h.shape
    tm = QKV_ROWS
    tiles_per_seq = S // tm
    resident = dict(index_map=lambda i: (0, 0), pipeline_mode=pl.Buffered(1))
    slab_specs = [pl.BlockSpec((tm, LANES), functools.partial(lambda i, c: (i, c), c=c))
                  for c in range(N_SLABS)]
    vec = pl.BlockSpec((1, D), lambda i: (0, 0))
    out_specs, out_shapes = [], []
    for _, d in DILATED_PATTERNS:
        spec = pl.BlockSpec((None, d, tm // d, D),
                            lambda i: (i // tiles_per_seq, 0, i % tiles_per_seq, 0))
        out_specs += [spec] * 3
        out_shapes += [jax.ShapeDtypeStruct((B, d, S // d, D), jnp.bfloat16)] * 3
    side_spec, side_shape = _side_cast_specs(side, T // tm)
    outs = pl.pallas_call(
        _qkv_kernel,
        grid=(T // tm,),
        in_specs=slab_specs + [vec, vec, pl.BlockSpec(w_q.shape, **resident),
                               pl.BlockSpec(w_kv.shape, **resident), side_spec],
        out_specs=out_specs + [side_spec],
        out_shape=out_shapes + [side_shape],
        compiler_params=pltpu.CompilerParams(
            dimension_semantics=("arbitrary",), vmem_limit_bytes=VMEM_LIMIT),
        name="qkv_proj",
    )(*([h] * N_SLABS), kv_norm.reshape(1, D), b_norm.reshape(1, D), w_q, w_kv, side)
    return [tuple(outs[3 * g:3 * g + 3]) for g in range(N_ATT_GROUPS)], outs[-1]


def _attn_kernel(*refs, dilation, has_prev):
    if has_prev:
        (table_ref, bmap_ref, q_ref, kp_ref, kc_ref, vp_ref, vc_ref,
         o_ref, stat_ref, bias_ref, stat_scr) = refs
    else:
        table_ref, bmap_ref, q_ref, kc_ref, vc_ref, o_ref, stat_ref, bias_ref, stat_scr = refs
    b, j, r = pl.program_id(0), pl.program_id(1), pl.program_id(2)
    n_keys = bias_ref.shape[2]

    @pl.when(jnp.logical_and(jnp.logical_and(b == 0, j == 0), r == 0))
    def _():
        stat_scr[...] = jnp.zeros_like(stat_scr)
        bmap = bmap_ref[...]
        in_prev = lax.broadcasted_iota(jnp.int32, bmap.shape, 1) < n_keys - QBLK

        def head(h, carry):
            acc = jnp.full(bmap.shape, NEG_INF, jnp.float32)
            for bucket in range(N_BUCKETS):
                acc = jnp.where(bmap == bucket, table_ref[bucket, h], acc)
            rows_h = pl.ds(pl.multiple_of(h * QBLK, QBLK), QBLK)
            bias_ref[0, rows_h, :] = jnp.where(in_prev, NEG_INF, acc)
            bias_ref[1, rows_h, :] = acc
            return carry

        lax.fori_loop(0, HEADS_PER_GROUP, head, 0)

    lane = lax.broadcasted_iota(jnp.int32, (QBLK, LANES), 1)
    first_head = lane < HEAD_DIM
    bias_copy = jnp.minimum(j, 1)
    rows = pl.ds(r, QBLK, stride=dilation) if dilation > 1 else slice(None)
    for hp in range(HEADS_PER_GROUP // 2):
        sl = slice(hp * HEAD_PAIR, (hp + 1) * HEAD_PAIR)
        qp = q_ref[:, sl]
        zero = jnp.zeros_like(qp)
        q2 = jnp.concatenate([jnp.where(first_head, qp, zero),
                              jnp.where(first_head, zero, qp)], axis=0)
        if has_prev:
            kcat = jnp.concatenate([kp_ref[:, sl], kc_ref[:, sl]], axis=0)
            vcat = jnp.concatenate([vp_ref[:, sl], vc_ref[:, sl]], axis=0)
        else:
            kcat, vcat = kc_ref[:, sl], vc_ref[:, sl]
        s = lax.dot_general(q2, kcat, (((1,), (1,)), ((), ())),
                            preferred_element_type=jnp.float32)
        s = s + bias_ref[bias_copy, hp * 2 * QBLK:(hp + 1) * 2 * QBLK, :]
        m = jnp.max(s, axis=1, keepdims=True)
        p = jnp.exp2(s - m)
        l = jnp.sum(p, axis=1, keepdims=True)
        o2 = jnp.dot(p.astype(jnp.bfloat16), vcat, preferred_element_type=jnp.float32)
        o_ref[hp, rows, :] = jnp.where(first_head, o2[:QBLK], o2[QBLK:])
        for half, head in ((slice(0, QBLK), 2 * hp), (slice(QBLK, 2 * QBLK), 2 * hp + 1)):
            stat_scr[:, head:head + 1] = m[half]
            stat_scr[:, HEADS_PER_GROUP + head:HEADS_PER_GROUP + head + 1] = l[half]
    stat_ref[rows, :] = stat_scr[...]


def _bucket_map(dilation, window, n_keys):
    W = window // dilation
    a = jnp.arange(QBLK, dtype=jnp.int32)[:, None]
    c = jnp.arange(n_keys, dtype=jnp.int32)[None, :]
    m = a + (n_keys - QBLK) - c
    band = (m >= 0) & (m <= W)
    n = jnp.maximum(m, 0) * dilation
    nf = jnp.maximum(n, 1).astype(jnp.float32)
    large = MAX_EXACT + (jnp.log(nf / MAX_EXACT) / math.log(MAX_DISTANCE / MAX_EXACT)
                         * (N_BUCKETS - MAX_EXACT)).astype(jnp.int32)
    large = jnp.minimum(large, N_BUCKETS - 1)
    bucket = jnp.where(n < MAX_EXACT, n, large)
    return jnp.where(band, bucket, -1)


def _attention_group(q, k, v, rel_bias, g, B, S):
    window, d = DILATED_PATTERNS[g]
    L = S // d
    n = L // QBLK
    D = D_MODEL
    has_prev = n > 1
    n_keys = 2 * QBLK if has_prev else QBLK
    table = rel_bias[:, g * HEADS_PER_GROUP:(g + 1) * HEADS_PER_GROUP].astype(jnp.float32) * LOG2E
    bmap = _bucket_map(d, window, n_keys)
    blk = (None, None, QBLK, D)
    cur = lambda b, j, r: (b, r, j, 0)
    prev = lambda b, j, r: (b, r, jnp.maximum(j - 1, 0), 0)
    if has_prev:
        kv_specs = [pl.BlockSpec(blk, prev), pl.BlockSpec(blk, cur),
                    pl.BlockSpec(blk, prev), pl.BlockSpec(blk, cur)]
        kv_args = (k, k, v, v)
    else:
        kv_specs = [pl.BlockSpec(blk, cur), pl.BlockSpec(blk, cur)]
        kv_args = (k, v)
    span = QBLK * d
    return pl.pallas_call(
        functools.partial(_attn_kernel, dilation=d, has_prev=has_prev),
        grid=(B, n, d),
        in_specs=[pl.BlockSpec(memory_space=pltpu.SMEM),
                  pl.BlockSpec(bmap.shape, lambda b, j, r: (0, 0)),
                  pl.BlockSpec(blk, cur)] + kv_specs,
        out_specs=[pl.BlockSpec((N_SLABS, span, LANES), lambda b, j, r: (0, b * n + j, 0)),
                   pl.BlockSpec((span, LANES), lambda b, j, r: (b * n + j, 0))],
        out_shape=[jax.ShapeDtypeStruct((N_SLABS, B * S, LANES), jnp.float32),
                   jax.ShapeDtypeStruct((B * S, LANES), jnp.float32)],
        scratch_shapes=[pltpu.VMEM((2, HEADS_PER_GROUP * QBLK, n_keys), jnp.float32),
                        pltpu.VMEM((QBLK, LANES), jnp.float32)],
        compiler_params=pltpu.CompilerParams(
            dimension_semantics=("arbitrary",) * 3, vmem_limit_bytes=VMEM_LIMIT),
        name=f"dilated_attn_g{g}",
    )(table, bmap, q, *kv_args)


def _merge_kernel(o0_ref, o1_ref, o2_ref, s0_ref, s1_ref, s2_ref, h_ref, wo_ref, fn_ref,
                  r2_ref, expand_ref,
                  h_out_ref, hn_ref, route_ref, counts_ref, carry_ref):
    i = pl.program_id(0)
    tm = h_ref.shape[0]

    @pl.when(i == 0)
    def _():
        carry_ref[...] = jnp.zeros_like(carry_ref)

    lane = lax.broadcasted_iota(jnp.int32, (tm, LANES), 1)
    head_lane = lane < HEADS_PER_GROUP
    stats = [s0_ref[...], s1_ref[...], s2_ref[...]]
    dens = [pltpu.roll(st, LANES - HEADS_PER_GROUP, axis=1) for st in stats]
    mx = jnp.maximum(jnp.maximum(stats[0], stats[1]), stats[2])
    es = [jnp.exp2(st - mx) for st in stats]
    inv = 1.0 / (es[0] * dens[0] + es[1] * dens[1] + es[2] * dens[2])
    merged = None
    for e, o_ref in zip(es, (o0_ref, o1_ref, o2_ref)):
        w = jnp.where(head_lane, e * inv, 0.0)
        hi = w.astype(jnp.bfloat16).astype(jnp.float32)
        packed = (hi + pltpu.roll(w - hi, HEADS_PER_GROUP, axis=1)).astype(jnp.bfloat16)
        wide = jnp.dot(packed, expand_ref[...], preferred_element_type=jnp.float32)
        o = jnp.concatenate([o_ref[c] for c in range(N_SLABS)], axis=1)
        term = wide * o
        merged = term if merged is None else merged + term
    h = h_ref[...] + jnp.dot(merged.astype(jnp.bfloat16), wo_ref[...],
                             preferred_element_type=jnp.float32)
    h_out_ref[...] = h
    hn = h * _rms_scale(h) * fn_ref[...]
    hn_ref[...] = hn

    hi, lo = _split_bf16(hn)
    both = jnp.dot(hi, r2_ref[...], preferred_element_type=jnp.float32)
    logits = (both[:, :LANES] + both[:, LANES:]
              + jnp.dot(lo, r2_ref[:, :LANES], preferred_element_type=jnp.float32))
    logits = jnp.where(lane < N_EXPERTS, logits, -jnp.inf)
    v0 = jnp.max(logits, axis=1, keepdims=True)
    e0 = jnp.min(jnp.where(logits == v0, lane, LANES), axis=1, keepdims=True)
    rest = jnp.where(lane == e0, -jnp.inf, logits)
    v1 = jnp.max(rest, axis=1, keepdims=True)
    e1 = jnp.min(jnp.where(rest == v1, lane, LANES), axis=1, keepdims=True)
    t = jnp.exp(v1 - v0)
    g0 = 1.0 / (1.0 + t)
    g1 = t / (1.0 + t)

    hit0 = lane == e0
    hit1 = lane == e1
    onehot = jnp.where(jnp.logical_or(hit0, hit1), 1.0, 0.0)
    r_i = lax.broadcasted_iota(jnp.int32, (tm, tm), 0)
    c_i = lax.broadcasted_iota(jnp.int32, (tm, tm), 1)
    lower = jnp.where(c_i < r_i, 1.0, 0.0).astype(jnp.bfloat16)
    before = jnp.dot(lower, onehot.astype(jnp.bfloat16),
                     preferred_element_type=jnp.float32) + carry_ref[0:1, :]
    rank0 = jnp.sum(jnp.where(hit0, before, 0.0), axis=1, keepdims=True)
    rank1 = jnp.sum(jnp.where(hit1, before, 0.0), axis=1, keepdims=True)
    total = carry_ref[0:1, :] + jnp.sum(onehot, axis=0, keepdims=True)
    carry_ref[...] = jnp.broadcast_to(total, carry_ref.shape)
    counts_ref[...] = jnp.broadcast_to(total, counts_ref.shape)

    route = jnp.zeros((tm, LANES), jnp.float32)
    for ln, val in ((R_E0, e0.astype(jnp.float32)), (R_E1, e1.astype(jnp.float32)),
                    (R_RANK0, rank0), (R_RANK1, rank1), (R_G0, g0), (R_G1, g1)):
        route = jnp.where(lane == ln, val, route)
    route_ref[...] = route


def _merge_layer(outs, lses, h, w_o, ffn_norm, w_router):
    T, D = h.shape
    tm = MERGE_ROWS
    r_pad = jnp.zeros((D, LANES), jnp.float32).at[:, :N_EXPERTS].set(w_router)
    r2 = jnp.concatenate(_split_bf16(r_pad), axis=1)
    head_of_col = jnp.arange(D, dtype=jnp.int32) // HEAD_DIM
    lane_id = jnp.arange(LANES, dtype=jnp.int32)[:, None]
    expand = ((lane_id < 2 * HEADS_PER_GROUP)
              & (lane_id % HEADS_PER_GROUP == head_of_col[None, :])).astype(jnp.bfloat16)
    row = lambda i: (i, 0)
    const = lambda i: (0, 0)
    wide = pl.BlockSpec((tm, D), row)
    narrow = pl.BlockSpec((tm, LANES), row)
    slabs = pl.BlockSpec((N_SLABS, tm, LANES), lambda i: (0, i, 0))
    return pl.pallas_call(
        _merge_kernel,
        grid=(T // tm,),
        in_specs=[slabs, slabs, slabs, narrow, narrow, narrow, wide,
                  pl.BlockSpec((D, D), const), pl.BlockSpec((1, D), const),
                  pl.BlockSpec((D, 2 * LANES), const), pl.BlockSpec((LANES, D), const)],
        out_specs=[wide, wide, narrow, pl.BlockSpec((8, LANES), const)],
        out_shape=[jax.ShapeDtypeStruct((T, D), jnp.float32),
                   jax.ShapeDtypeStruct((T, D), jnp.float32),
                   jax.ShapeDtypeStruct((T, LANES), jnp.float32),
                   jax.ShapeDtypeStruct((8, LANES), jnp.float32)],
        scratch_shapes=[pltpu.VMEM((8, LANES), jnp.float32)],
        compiler_params=pltpu.CompilerParams(
            dimension_semantics=("arbitrary",), vmem_limit_bytes=VMEM_LIMIT),
        name="merge_outproj_router",
    )(*outs, *lses, h, w_o, ffn_norm.reshape(1, D), r2, expand)


def _dispatch_kernel(dest_ref, pad_start_ref, pad_len_ref, used_ref, hn_ref, side_in_ref,
                     xs_ref, side_out_ref, zbuf, sem, zsem):
    side_out_ref[...] = side_in_ref[...].astype(side_out_ref.dtype)
    i = pl.program_id(0)
    tile = hn_ref.shape[0]
    base = i * tile
    n_tiles = xs_ref.shape[0] // MOE_ROWS

    def zero_fill(act):
        for e in range(N_EXPERTS):
            run_start = pad_start_ref[e]
            head = (-run_start) & (SUBLANES - 1)
            for row in range(SUBLANES - 1):
                @pl.when(row < head)
                def _():
                    act(pltpu.make_async_copy(zbuf.at[pl.ds(0, 1), :],
                                              xs_ref.at[pl.ds(run_start + row, 1), :], zsem))
            body_start = run_start + head
            body_len = pad_len_ref[e] - head
            bit = MOE_ROWS // 2
            while bit >= SUBLANES:
                off = pl.multiple_of(body_start + (body_len & -(2 * bit)), SUBLANES)

                @pl.when((body_len & bit) != 0)
                def _():
                    act(pltpu.make_async_copy(zbuf.at[pl.ds(0, bit), :],
                                              xs_ref.at[pl.ds(off, bit), :], zsem))
                bit //= 2
        for t in range(n_tiles):
            @pl.when(t >= used_ref[0])
            def _():
                act(pltpu.make_async_copy(zbuf, xs_ref.at[pl.ds(t * MOE_ROWS, MOE_ROWS), :], zsem))

    @pl.when(i == 0)
    def _():
        zbuf[...] = jnp.zeros_like(zbuf)
        zero_fill(lambda c: c.start())

    def start(t, c):
        for k in range(TOP_K):
            pltpu.make_async_copy(hn_ref.at[pl.ds(t, 1), :],
                                  xs_ref.at[pl.ds(dest_ref[TOP_K * (base + t) + k], 1), :],
                                  sem).start()
        return c

    lax.fori_loop(0, tile, start, 0, unroll=DMA_UNROLL)
    for k in range(TOP_K):
        pltpu.make_async_copy(hn_ref, xs_ref.at[pl.ds(0, tile), :], sem).wait()

    @pl.when(i == 0)
    def _():
        zero_fill(lambda c: c.wait())


def _dispatch(dest_flat, pad_start, pad_len, used, hn, n_slots, side):
    T, D = hn.shape
    side_spec, side_shape = _side_cast_specs(side, T // DISPATCH_ROWS)
    return pl.pallas_call(
        _dispatch_kernel,
        grid_spec=pltpu.PrefetchScalarGridSpec(
            num_scalar_prefetch=4,
            grid=(T // DISPATCH_ROWS,),
            in_specs=[pl.BlockSpec((DISPATCH_ROWS, D), lambda i, *_: (i, 0)), side_spec],
            out_specs=[pl.BlockSpec(memory_space=pl.ANY), side_spec],
            scratch_shapes=[pltpu.VMEM((MOE_ROWS, D), hn.dtype),
                            pltpu.SemaphoreType.DMA(()), pltpu.SemaphoreType.DMA(())],
        ),
        out_shape=[jax.ShapeDtypeStruct((n_slots, D), hn.dtype), side_shape],
        compiler_params=pltpu.CompilerParams(
            dimension_semantics=("arbitrary",), vmem_limit_bytes=VMEM_LIMIT),
        name="moe_dispatch",
    )(dest_flat, pad_start, pad_len, used, hn, side)


def _expert_kernel(tile_e_ref, used_ref, valid_ref, x_ref, wg_ref, wu_ref, wd_ref, y_ref,
                   xb_ref, hid_ref):
    del tile_e_ref
    i = pl.program_id(0)
    s = pl.program_id(1)
    n_f, tm, tf = hid_ref.shape
    active = i < used_ref[0]
    n_valid = valid_ref[jnp.minimum(i, used_ref[0] - 1)]
    full = n_valid == tm
    gate_step = jnp.logical_and(active, s < n_f)
    down_step = jnp.logical_and(active, s >= n_f)

    @pl.when(jnp.logical_and(active, s == 0))
    def _():
        xb_ref[...] = x_ref[...].astype(xb_ref.dtype)

    def gate_up(rows):
        x = xb_ref[rows, :]
        for c0 in range(0, tf, MOE_CHUNK):
            cols = slice(c0, min(c0 + MOE_CHUNK, tf))
            g = jnp.dot(x, wg_ref[:, cols], preferred_element_type=jnp.float32)
            u = jnp.dot(x, wu_ref[:, cols], preferred_element_type=jnp.float32)
            hid_ref[s, rows, cols] = (_silu(g) * u).astype(hid_ref.dtype)

    def down(rows):
        hid = jnp.concatenate([hid_ref[c, rows, :] for c in range(n_f)], axis=1)
        y_ref[rows, :] = jnp.dot(hid, wd_ref[...], preferred_element_type=jnp.float32)

    @pl.when(jnp.logical_and(gate_step, full))
    def _():
        gate_up(slice(None))

    @pl.when(jnp.logical_and(down_step, full))
    def _():
        down(slice(None))

    for r0 in range(0, tm, MOE_SUBROWS):
        rows = slice(r0, r0 + MOE_SUBROWS)
        occupied = jnp.logical_and(jnp.logical_not(full), r0 < n_valid)

        @pl.when(jnp.logical_and(gate_step, occupied))
        def _():
            gate_up(rows)

        @pl.when(jnp.logical_and(down_step, occupied))
        def _():
            down(rows)

        @pl.when(jnp.logical_and(down_step, r0 >= n_valid))
        def _():
            y_ref[rows, :] = jnp.zeros((MOE_SUBROWS, y_ref.shape[1]), y_ref.dtype)

    @pl.when(jnp.logical_and(jnp.logical_not(active), s >= n_f))
    def _():
        y_ref[...] = jnp.zeros_like(y_ref)


def _expert_ffn(tile_e, used, tile_valid, xs, w_gate, w_up, w_down):
    n_slots, D = xs.shape
    F = w_gate.shape[2]
    tm, tf, tn = MOE_ROWS, MOE_FT, MOE_NT
    n_f, n_n = F // tf, D // tn
    n_tiles = n_slots // tm
    last = n_f + n_n - 1

    def eff(i, s, used_ref):
        idle = i >= used_ref[0]
        return jnp.where(idle, used_ref[0] - 1, i), jnp.where(idle, last, s)

    def x_map(i, s, te, us, nv):
        ie, se = eff(i, s, us)
        return jnp.minimum(ie + jnp.minimum(se, 1), us[0] - 1), 0

    def gate_map(i, s, te, us, nv):
        ie, se = eff(i, s, us)
        return te[ie], 0, jnp.minimum(se, n_f - 1)

    def down_map(i, s, te, us, nv):
        ie, se = eff(i, s, us)
        tile = jnp.where(se == 0, jnp.maximum(ie - 1, 0), ie)
        return te[tile], 0, jnp.where(se == 0, n_n - 1, jnp.clip(se - n_f, 0, n_n - 1))

    def out_map(i, s, te, us, nv):
        return i, jnp.clip(s - n_f, 0, n_n - 1)

    return pl.pallas_call(
        _expert_kernel,
        grid_spec=pltpu.PrefetchScalarGridSpec(
            num_scalar_prefetch=3,
            grid=(n_tiles, n_f + n_n),
            in_specs=[
                pl.BlockSpec((tm, D), x_map),
                pl.BlockSpec((None, D, tf), gate_map),
                pl.BlockSpec((None, D, tf), gate_map),
                pl.BlockSpec((None, F, tn), down_map),
            ],
            out_specs=pl.BlockSpec((tm, tn), out_map),
            scratch_shapes=[pltpu.VMEM((tm, D), jnp.bfloat16),
                            pltpu.VMEM((n_f, tm, tf), jnp.bfloat16)],
        ),
        out_shape=jax.ShapeDtypeStruct((n_slots, D), jnp.float32),
        compiler_params=pltpu.CompilerParams(
            dimension_semantics=("arbitrary", "arbitrary"), vmem_limit_bytes=VMEM_LIMIT),
        name="moe_experts",
    )(tile_e, used, tile_valid, xs, w_gate, w_up, w_down)


def _combine_kernel(dest_ref, h_ref, route_ref, fin_ref, ys_ref, o_ref, ybuf, sems):
    i = pl.program_id(0)
    n_steps = pl.num_programs(0)
    tile = h_ref.shape[0]

    def issue(step, slot):
        base = step * tile

        def start(t, c):
            for k in range(TOP_K):
                pltpu.make_async_copy(
                    ys_ref.at[pl.ds(dest_ref[TOP_K * (base + t) + k], 1), :],
                    ybuf.at[slot, k, pl.ds(t, 1), :], sems.at[slot]).start()
            return c

        lax.fori_loop(0, tile, start, 0, unroll=DMA_UNROLL)

    slot = lax.rem(i, 2)

    @pl.when(i == 0)
    def _():
        issue(0, 0)

    @pl.when(i + 1 < n_steps)
    def _():
        issue(i + 1, 1 - slot)

    for k in range(TOP_K):
        pltpu.make_async_copy(ys_ref.at[pl.ds(0, tile), :], ybuf.at[slot, k], sems.at[slot]).wait()
    route = route_ref[...]
    g0 = route[:, R_G0:R_G0 + 1]
    g1 = route[:, R_G1:R_G1 + 1]
    h = h_ref[...] + (g0 * ybuf[slot, 0] + g1 * ybuf[slot, 1])
    o_ref[...] = h * _rms_scale(h) * fin_ref[...]


def _combine(dest_flat, h, route, final_norm, ys):
    T, D = h.shape
    tc = COMBINE_ROWS
    return pl.pallas_call(
        _combine_kernel,
        grid_spec=pltpu.PrefetchScalarGridSpec(
            num_scalar_prefetch=1,
            grid=(T // tc,),
            in_specs=[pl.BlockSpec((tc, D), lambda i, d: (i, 0)),
                      pl.BlockSpec((tc, LANES), lambda i, d: (i, 0)),
                      pl.BlockSpec((1, D), lambda i, d: (0, 0)),
                      pl.BlockSpec(memory_space=pl.ANY)],
            out_specs=pl.BlockSpec((tc, D), lambda i, d: (i, 0)),
            scratch_shapes=[pltpu.VMEM((2, TOP_K, tc, D), jnp.float32),
                            pltpu.SemaphoreType.DMA((2,))],
        ),
        out_shape=jax.ShapeDtypeStruct((T, D), jnp.float32),
        compiler_params=pltpu.CompilerParams(
            dimension_semantics=("arbitrary",), vmem_limit_bytes=VMEM_LIMIT),
        name="moe_combine_final_norm",
    )(dest_flat, h, route, final_norm.reshape(1, D), ys)


def _routing_tables(route, counts_row):
    T = route.shape[0]
    counts = counts_row[0, :N_EXPERTS].astype(jnp.int32)
    padded = ((counts + MOE_ROWS - 1) // MOE_ROWS) * MOE_ROWS
    ends = jnp.cumsum(padded)
    starts = ends - padded
    expert = route[:, R_E0:R_E1 + 1].astype(jnp.int32)
    rank = route[:, R_RANK0:R_RANK1 + 1].astype(jnp.int32)
    onehot = expert[..., None] == jnp.arange(N_EXPERTS, dtype=jnp.int32)
    dest = jnp.sum(jnp.where(onehot, starts, 0), axis=-1) + rank
    n_tiles = (TOP_K * T) // MOE_ROWS + N_EXPERTS
    tile_start = jnp.arange(n_tiles, dtype=jnp.int32) * MOE_ROWS
    tile_e = jnp.minimum(jnp.sum(ends[None, :] <= tile_start[:, None], axis=1),
                         N_EXPERTS - 1).astype(jnp.int32)
    used = (ends[-1:] // MOE_ROWS).astype(jnp.int32)
    tile_valid = jnp.clip(counts[tile_e] - (tile_start - starts[tile_e]), 0, MOE_ROWS).astype(jnp.int32)
    pad_start = (starts + counts).astype(jnp.int32)
    pad_len = (padded - counts).astype(jnp.int32)
    return dest.reshape(-1), tile_e, used, tile_valid, pad_start, pad_len, n_tiles * MOE_ROWS


def kernel(x, a_norm, a_proj, a_scale, kv_norm, w_kv, b_norm, w_q, w_o, rel_bias, ffn_norm,
           dense_w_gate, dense_w_up, dense_w_down, moe_router, moe_w_gate, moe_w_up,
           moe_w_down, final_norm):
    B, S, D = x.shape
    T = B * S
    E, _, F = moe_w_gate[0].shape
    h2, (wg16, wq16, wkv16, wo16) = _layer0(
        x.reshape(T, D), a_norm[0], a_proj[0], a_scale[0], ffn_norm[0],
        dense_w_gate[0], dense_w_up[0], dense_w_down[0],
        (moe_w_gate[0].reshape(E * D, F), w_q[0], w_kv, w_o[0]))
    qkv, wu16 = _qkv_project(h2, kv_norm, b_norm[0], wq16, wkv16, B, S,
                             moe_w_up[0].reshape(E * D, F))
    outs, stats = [], []
    for g, (q, k, v) in enumerate(qkv):
        o, st = _attention_group(q, k, v, rel_bias, g, B, S)
        outs.append(o)
        stats.append(st)
    h3, hn3, route, counts = _merge_layer(outs, stats, h2, wo16, ffn_norm[1], moe_router[0])
    dest, tile_e, used, tile_valid, pad_start, pad_len, n_slots = _routing_tables(route, counts)
    xs, wd16 = _dispatch(dest, pad_start, pad_len, used, hn3, n_slots,
                         moe_w_down[0].reshape(E * F, D))
    ys = _expert_ffn(tile_e, used, tile_valid, xs, wg16.reshape(E, D, F), wu16.reshape(E, D, F),
                     wd16.reshape(E, F, D))
    out = _combine(dest, h3, route, final_norm, ys)
    return out.reshape(B, S, D)
```

```python
import functools
import math

import jax
import jax.numpy as jnp
from jax import lax
from jax.experimental import pallas as pl
from jax.experimental.pallas import tpu as pltpu

D_MODEL = 1024
EPS = 1e-6
POOL_WINDOWS = (2, 4, 8, 16)
POOL_GROUP_DIM = D_MODEL // len(POOL_WINDOWS)
MAX_POOL_WINDOW = max(POOL_WINDOWS)
HEAD_DIM = 64
HEADS_PER_GROUP = D_MODEL // HEAD_DIM
DILATED_PATTERNS = ((128, 1), (512, 4), (2048, 16))
N_ATT_GROUPS = len(DILATED_PATTERNS)
QBLK = 128
NEG_INF = -1e30
LOG2E = 1.4426950408889634
Q_SCALE = HEAD_DIM ** -0.5 * LOG2E
N_BUCKETS = 32
MAX_EXACT = N_BUCKETS // 2
MAX_DISTANCE = 2048
N_EXPERTS = 8
TOP_K = 2
SEQ_LEN = 2048

LANES = 128
SUBLANES = 8
N_SLABS = D_MODEL // LANES
HEAD_PAIR = 2 * HEAD_DIM
assert HEAD_PAIR == LANES

LAYER0_ROWS = 512
QKV_ROWS = 512
ATTN_BLOCKS_PER_STEP = 4
MERGE_ROWS = 512
MOE_ROWS = 1024
MOE_FT = 1792
MOE_CHUNK = 512
MOE_SUBROWS = 256
MOE_NT = 1024
DISPATCH_ROWS = 1024
COMBINE_ROWS = 512
DMA_UNROLL = 8
VMEM_LIMIT = 60 * 1024 * 1024

R_E0, R_E1, R_RANK0, R_RANK1, R_G0, R_G1 = 0, 1, 2, 3, 4, 5


def _rms_scale(x):
    return lax.rsqrt(jnp.mean(x * x, axis=-1, keepdims=True) + EPS)


def _silu(g):
    return g * (1.0 / (1.0 + jnp.exp(-g)))


def _split_bf16(x):
    hi = x.astype(jnp.bfloat16)
    lo = (x - hi.astype(jnp.float32)).astype(jnp.bfloat16)
    return hi, lo


def _side_cast_specs(side, n_steps):
    rows = side.shape[0] // n_steps
    spec = pl.BlockSpec((rows, side.shape[1]), lambda i, *_: (i, 0))
    return spec, jax.ShapeDtypeStruct(side.shape, jnp.bfloat16)


def _pool_mix(x, halo, seq_tile, gain, proj_ref, scale):
    xn = x * _rms_scale(x) * gain
    hn_halo = halo * _rms_scale(halo) * gain
    hn_halo = jnp.where(seq_tile > 0, hn_halo, 0.0)
    full = jnp.concatenate([hn_halo, xn], axis=0)
    ts = x.shape[0]
    pos = seq_tile * ts + lax.broadcasted_iota(jnp.int32, (ts, 1), 0)
    outs = []
    for g, w in enumerate(POOL_WINDOWS):
        c0 = g * POOL_GROUP_DIM
        s = full[:, c0:c0 + POOL_GROUP_DIM]
        span = 1
        while span < w:
            s = s + pltpu.roll(s, span, axis=0)
            span *= 2
        s = s[MAX_POOL_WINDOW:, :]
        cnt = jnp.minimum(pos + 1, w).astype(jnp.float32)
        pooled = s / cnt - xn[:, c0:c0 + POOL_GROUP_DIM]
        outs.append(jnp.dot(pooled.astype(jnp.bfloat16), proj_ref[g],
                            preferred_element_type=jnp.float32))
    return x + jnp.concatenate(outs, axis=1) * scale


def _layer0_kernel(*refs, tiles_per_seq, n_side):
    (x_ref, halo_ref, an_ref, proj_ref, asc_ref, fn_ref, wg_ref, wu_ref, wd_ref) = refs[:9]
    side_in = refs[9:9 + n_side]
    h_out_ref = refs[9 + n_side]
    side_out = refs[10 + n_side:]
    for src, dst in zip(side_in, side_out):
        dst[...] = src[...].astype(dst.dtype)
    seq_tile = lax.rem(pl.program_id(0), tiles_per_seq)
    h = _pool_mix(x_ref[...], halo_ref[...], seq_tile, an_ref[...], proj_ref, asc_ref[...])
    hn = (h * _rms_scale(h) * fn_ref[...]).astype(jnp.bfloat16)
    g = jnp.dot(hn, wg_ref[...], preferred_element_type=jnp.float32)
    u = jnp.dot(hn, wu_ref[...], preferred_element_type=jnp.float32)
    hid = (_silu(g) * u).astype(jnp.bfloat16)
    h_out_ref[...] = h + jnp.dot(hid, wd_ref[...], preferred_element_type=jnp.float32)


def _layer0(x, a_norm, a_proj, a_scale, ffn_norm, w_gate, w_up, w_down, sides):
    T, D = x.shape
    F = w_gate.shape[1]
    ts = LAYER0_ROWS
    n_steps = T // ts
    halo_blocks = ts // MAX_POOL_WINDOW
    row = lambda i: (i, 0)
    vec = pl.BlockSpec((1, D), lambda i: (0, 0))
    resident = dict(index_map=lambda i: (0, 0), pipeline_mode=pl.Buffered(1))
    side_specs, side_shapes = zip(*[_side_cast_specs(a, n_steps) for a in sides])
    outs = pl.pallas_call(
        functools.partial(_layer0_kernel, tiles_per_seq=SEQ_LEN // ts, n_side=len(sides)),
        grid=(n_steps,),
        in_specs=[
            pl.BlockSpec((ts, D), row),
            pl.BlockSpec((MAX_POOL_WINDOW, D), lambda i: (jnp.maximum(i * halo_blocks - 1, 0), 0)),
            vec,
            pl.BlockSpec(a_proj.shape, lambda i: (0, 0, 0), pipeline_mode=pl.Buffered(1)),
            vec, vec,
            pl.BlockSpec((D, F), **resident),
            pl.BlockSpec((D, F), **resident),
            pl.BlockSpec((F, D), **resident),
            *side_specs,
        ],
        out_specs=[pl.BlockSpec((ts, D), row), *side_specs],
        out_shape=[jax.ShapeDtypeStruct((T, D), jnp.float32), *side_shapes],
        compiler_params=pltpu.CompilerParams(
            dimension_semantics=("arbitrary",), vmem_limit_bytes=VMEM_LIMIT),
        name="pool_mixer_dense_swiglu",
    )(x, x, a_norm.reshape(1, D), a_proj.astype(jnp.bfloat16), a_scale.reshape(1, D),
      ffn_norm.reshape(1, D), w_gate.astype(jnp.bfloat16), w_up.astype(jnp.bfloat16),
      w_down.astype(jnp.bfloat16), *sides)
    return outs[0], outs[1:]


def _qkv_kernel(*refs):
    slabs = refs[:N_SLABS]
    kvg_ref, qg_ref, wq_ref, wkv_ref, side_in_ref = refs[N_SLABS:N_SLABS + 5]
    out_refs = refs[N_SLABS + 5:-1]
    side_out_ref = refs[-1]
    side_out_ref[...] = side_in_ref[...].astype(side_out_ref.dtype)
    tm = slabs[0].shape[0]
    for g, (_, dilation) in enumerate(DILATED_PATTERNS):
        n = tm // dilation
        cols = []
        for slab in slabs:
            if dilation == 1:
                cols.append(slab[...])
            else:
                cols.append(jnp.concatenate(
                    [slab[pl.ds(r, n, stride=dilation), :] for r in range(dilation)], axis=0))
        x = jnp.concatenate(cols, axis=1)
        xs = x * _rms_scale(x)
        xq = (xs * qg_ref[...]).astype(jnp.bfloat16)
        xkv = (xs * kvg_ref[...]).astype(jnp.bfloat16)
        D = D_MODEL
        kcols = slice(g * D, (g + 1) * D)
        vcols = slice((N_ATT_GROUPS + g) * D, (N_ATT_GROUPS + g + 1) * D)
        q = jnp.dot(xq, wq_ref[:, kcols], preferred_element_type=jnp.float32) * Q_SCALE
        k = jnp.dot(xkv, wkv_ref[:, kcols], preferred_element_type=jnp.float32)
        v = jnp.dot(xkv, wkv_ref[:, vcols], preferred_element_type=jnp.float32)
        for out_ref, val in zip(out_refs[3 * g:3 * g + 3], (q, k, v)):
            for r in range(dilation):
                out_ref[r] = val[r * n:(r + 1) * n].astype(out_ref.dtype)


def _qkv_project(h, kv_norm, b_norm, w_q, w_kv, B, S, side):
    T, D = h.shape
    tm = QKV_ROWS
    tiles_per_seq = S // tm
    resident = dict(index_map=lambda i: (0, 0), pipeline_mode=pl.Buffered(1))
    slab_specs = [pl.BlockSpec((tm, LANES), functools.partial(lambda i, c: (i, c), c=c))
                  for c in range(N_SLABS)]
    vec = pl.BlockSpec((1, D), lambda i: (0, 0))
    out_specs, out_shapes = [], []
    for _, d in DILATED_PATTERNS:
        spec = pl.BlockSpec((None, d, tm // d, D),
                            lambda i: (i // tiles_per_seq, 0, i % tiles_per_seq, 0))
        out_specs += [spec] * 3
        out_shapes += [jax.ShapeDtypeStruct((B, d, S // d, D), jnp.bfloat16)] * 3
    side_spec, side_shape = _side_cast_specs(side, T // tm)
    outs = pl.pallas_call(
        _qkv_kernel,
        grid=(T // tm,),
        in_specs=slab_specs + [vec, vec, pl.BlockSpec(w_q.shape, **resident),
                               pl.BlockSpec(w_kv.shape, **resident), side_spec],
        out_specs=out_specs + [side_spec],
        out_shape=out_shapes + [side_shape],
        compiler_params=pltpu.CompilerParams(
            dimension_semantics=("arbitrary",), vmem_limit_bytes=VMEM_LIMIT),
        name="qkv_proj",
    )(*([h] * N_SLABS), kv_norm.reshape(1, D), b_norm.reshape(1, D), w_q, w_kv, side)
    return [tuple(outs[3 * g:3 * g + 3]) for g in range(N_ATT_GROUPS)], outs[-1]


def _attn_kernel(*refs, dilation, has_prev, jb, rb):
    if has_prev:
        (table_ref, bmap_ref, q_ref, kp_ref, kc_ref, vp_ref, vc_ref,
         o_ref, stat_ref, bias_ref, stat_scr) = refs
    else:
        table_ref, bmap_ref, q_ref, kc_ref, vc_ref, o_ref, stat_ref, bias_ref, stat_scr = refs
    b, jstep, rstep = pl.program_id(0), pl.program_id(1), pl.program_id(2)
    n_keys = bias_ref.shape[2]

    @pl.when(jnp.logical_and(jnp.logical_and(b == 0, jstep == 0), rstep == 0))
    def _():
        stat_scr[...] = jnp.zeros_like(stat_scr)
        bmap = bmap_ref[...]
        in_prev = lax.broadcasted_iota(jnp.int32, bmap.shape, 1) < n_keys - QBLK

        def head(h, carry):
            acc = jnp.full(bmap.shape, NEG_INF, jnp.float32)
            for bucket in range(N_BUCKETS):
                acc = jnp.where(bmap == bucket, table_ref[bucket, h], acc)
            rows_h = pl.ds(pl.multiple_of(h * QBLK, QBLK), QBLK)
            bias_ref[0, rows_h, :] = jnp.where(in_prev, NEG_INF, acc)
            bias_ref[1, rows_h, :] = acc
            return carry

        lax.fori_loop(0, HEADS_PER_GROUP, head, 0)

    lane = lax.broadcasted_iota(jnp.int32, (QBLK, LANES), 1)
    first_head = lane < HEAD_DIM
    for rr in range(rb):
        for jj in range(jb):
            qrows = slice(jj * QBLK, (jj + 1) * QBLK)
            bias_copy = jnp.minimum(jstep, 1) if jj == 0 else 1
            if dilation > 1:
                out_rows = pl.ds(dilation * QBLK * jj + rstep * rb + rr, QBLK, stride=dilation)
            else:
                out_rows = qrows
            for hp in range(HEADS_PER_GROUP // 2):
                sl = slice(hp * HEAD_PAIR, (hp + 1) * HEAD_PAIR)
                qp = q_ref[rr, qrows, sl]
                zero = jnp.zeros_like(qp)
                q2 = jnp.concatenate([jnp.where(first_head, qp, zero),
                                      jnp.where(first_head, zero, qp)], axis=0)
                if has_prev:
                    if jj == 0:
                        k_prev, v_prev = kp_ref[rr, :, sl], vp_ref[rr, :, sl]
                    else:
                        prows = slice((jj - 1) * QBLK, jj * QBLK)
                        k_prev, v_prev = kc_ref[rr, prows, sl], vc_ref[rr, prows, sl]
                    kcat = jnp.concatenate([k_prev, kc_ref[rr, qrows, sl]], axis=0)
                    vcat = jnp.concatenate([v_prev, vc_ref[rr, qrows, sl]], axis=0)
                else:
                    kcat, vcat = kc_ref[rr, qrows, sl], vc_ref[rr, qrows, sl]
                s = lax.dot_general(q2, kcat, (((1,), (1,)), ((), ())),
                                    preferred_element_type=jnp.float32)
                s = s + bias_ref[bias_copy, hp * 2 * QBLK:(hp + 1) * 2 * QBLK, :]
                m = jnp.max(s, axis=1, keepdims=True)
                p = jnp.exp2(s - m)
                l = jnp.sum(p, axis=1, keepdims=True)
                o2 = jnp.dot(p.astype(jnp.bfloat16), vcat, preferred_element_type=jnp.float32)
                o_ref[hp, out_rows, :] = jnp.where(first_head, o2[:QBLK], o2[QBLK:])
                for half, head in ((slice(0, QBLK), 2 * hp), (slice(QBLK, 2 * QBLK), 2 * hp + 1)):
                    stat_scr[:, head:head + 1] = m[half]
                    stat_scr[:, HEADS_PER_GROUP + head:HEADS_PER_GROUP + head + 1] = l[half]
            stat_ref[out_rows, :] = stat_scr[...]


def _bucket_map(dilation, window, n_keys):
    W = window // dilation
    a = jnp.arange(QBLK, dtype=jnp.int32)[:, None]
    c = jnp.arange(n_keys, dtype=jnp.int32)[None, :]
    m = a + (n_keys - QBLK) - c
    band = (m >= 0) & (m <= W)
    n = jnp.maximum(m, 0) * dilation
    nf = jnp.maximum(n, 1).astype(jnp.float32)
    large = MAX_EXACT + (jnp.log(nf / MAX_EXACT) / math.log(MAX_DISTANCE / MAX_EXACT)
                         * (N_BUCKETS - MAX_EXACT)).astype(jnp.int32)
    large = jnp.minimum(large, N_BUCKETS - 1)
    bucket = jnp.where(n < MAX_EXACT, n, large)
    return jnp.where(band, bucket, -1)


def _attention_group(q, k, v, rel_bias, g, B, S):
    window, d = DILATED_PATTERNS[g]
    L = S // d
    n = L // QBLK
    D = D_MODEL
    has_prev = n > 1
    n_keys = 2 * QBLK if has_prev else QBLK
    table = rel_bias[:, g * HEADS_PER_GROUP:(g + 1) * HEADS_PER_GROUP].astype(jnp.float32) * LOG2E
    bmap = _bucket_map(d, window, n_keys)
    jb = min(ATTN_BLOCKS_PER_STEP, n)
    rb = ATTN_BLOCKS_PER_STEP // jb
    cur_blk = (None, rb, jb * QBLK, D)
    prev_blk = (None, rb, QBLK, D)
    cur = lambda b, j, r: (b, r, j, 0)
    prev = lambda b, j, r: (b, r, jnp.maximum(j * jb - 1, 0), 0)
    if has_prev:
        kv_specs = [pl.BlockSpec(prev_blk, prev), pl.BlockSpec(cur_blk, cur),
                    pl.BlockSpec(prev_blk, prev), pl.BlockSpec(cur_blk, cur)]
        kv_args = (k, k, v, v)
    else:
        kv_specs = [pl.BlockSpec(cur_blk, cur), pl.BlockSpec(cur_blk, cur)]
        kv_args = (k, v)
    n_j = n // jb
    span = QBLK * jb * d
    return pl.pallas_call(
        functools.partial(_attn_kernel, dilation=d, has_prev=has_prev, jb=jb, rb=rb),
        grid=(B, n_j, d // rb),
        in_specs=[pl.BlockSpec(memory_space=pltpu.SMEM),
                  pl.BlockSpec(bmap.shape, lambda b, j, r: (0, 0)),
                  pl.BlockSpec(cur_blk, cur)] + kv_specs,
        out_specs=[pl.BlockSpec((N_SLABS, span, LANES), lambda b, j, r: (0, b * n_j + j, 0)),
                   pl.BlockSpec((span, LANES), lambda b, j, r: (b * n_j + j, 0))],
        out_shape=[jax.ShapeDtypeStruct((N_SLABS, B * S, LANES), jnp.float32),
                   jax.ShapeDtypeStruct((B * S, LANES), jnp.float32)],
        scratch_shapes=[pltpu.VMEM((2, HEADS_PER_GROUP * QBLK, n_keys), jnp.float32),
                        pltpu.VMEM((QBLK, LANES), jnp.float32)],
        compiler_params=pltpu.CompilerParams(
            dimension_semantics=("arbitrary",) * 3, vmem_limit_bytes=VMEM_LIMIT),
        name=f"dilated_attn_g{g}",
    )(table, bmap, q, *kv_args)


def _merge_kernel(o0_ref, o1_ref, o2_ref, s0_ref, s1_ref, s2_ref, h_ref, wo_ref, fn_ref,
                  r2_ref, expand_ref,
                  h_out_ref, hn_ref, route_ref, counts_ref, carry_ref):
    i = pl.program_id(0)
    tm = h_ref.shape[0]

    @pl.when(i == 0)
    def _():
        carry_ref[...] = jnp.zeros_like(carry_ref)

    lane = lax.broadcasted_iota(jnp.int32, (tm, LANES), 1)
    head_lane = lane < HEADS_PER_GROUP
    stats = [s0_ref[...], s1_ref[...], s2_ref[...]]
    dens = [pltpu.roll(st, LANES - HEADS_PER_GROUP, axis=1) for st in stats]
    mx = jnp.maximum(jnp.maximum(stats[0], stats[1]), stats[2])
    es = [jnp.exp2(st - mx) for st in stats]
    inv = 1.0 / (es[0] * dens[0] + es[1] * dens[1] + es[2] * dens[2])
    merged = None
    for e, o_ref in zip(es, (o0_ref, o1_ref, o2_ref)):
        w = jnp.where(head_lane, e * inv, 0.0)
        hi = w.astype(jnp.bfloat16).astype(jnp.float32)
        packed = (hi + pltpu.roll(w - hi, HEADS_PER_GROUP, axis=1)).astype(jnp.bfloat16)
        wide = jnp.dot(packed, expand_ref[...], preferred_element_type=jnp.float32)
        o = jnp.concatenate([o_ref[c] for c in range(N_SLABS)], axis=1)
        term = wide * o
        merged = term if merged is None else merged + term
    h = h_ref[...] + jnp.dot(merged.astype(jnp.bfloat16), wo_ref[...],
                             preferred_element_type=jnp.float32)
    h_out_ref[...] = h
    hn = h * _rms_scale(h) * fn_ref[...]
    hn_ref[...] = hn

    hi, lo = _split_bf16(hn)
    both = jnp.dot(hi, r2_ref[...], preferred_element_type=jnp.float32)
    logits = (both[:, :LANES] + both[:, LANES:]
              + jnp.dot(lo, r2_ref[:, :LANES], preferred_element_type=jnp.float32))
    logits = jnp.where(lane < N_EXPERTS, logits, -jnp.inf)
    v0 = jnp.max(logits, axis=1, keepdims=True)
    e0 = jnp.min(jnp.where(logits == v0, lane, LANES), axis=1, keepdims=True)
    rest = jnp.where(lane == e0, -jnp.inf, logits)
    v1 = jnp.max(rest, axis=1, keepdims=True)
    e1 = jnp.min(jnp.where(rest == v1, lane, LANES), axis=1, keepdims=True)
    t = jnp.exp(v1 - v0)
    g0 = 1.0 / (1.0 + t)
    g1 = t / (1.0 + t)

    hit0 = lane == e0
    hit1 = lane == e1
    onehot = jnp.where(jnp.logical_or(hit0, hit1), 1.0, 0.0)
    r_i = lax.broadcasted_iota(jnp.int32, (tm, tm), 0)
    c_i = lax.broadcasted_iota(jnp.int32, (tm, tm), 1)
    lower = jnp.where(c_i < r_i, 1.0, 0.0).astype(jnp.bfloat16)
    before = jnp.dot(lower, onehot.astype(jnp.bfloat16),
                     preferred_element_type=jnp.float32) + carry_ref[0:1, :]
    rank0 = jnp.sum(jnp.where(hit0, before, 0.0), axis=1, keepdims=True)
    rank1 = jnp.sum(jnp.where(hit1, before, 0.0), axis=1, keepdims=True)
    total = carry_ref[0:1, :] + jnp.sum(onehot, axis=0, keepdims=True)
    carry_ref[...] = jnp.broadcast_to(total, carry_ref.shape)
    counts_ref[...] = jnp.broadcast_to(total, counts_ref.shape)

    route = jnp.zeros((tm, LANES), jnp.float32)
    for ln, val in ((R_E0, e0.astype(jnp.float32)), (R_E1, e1.astype(jnp.float32)),
                    (R_RANK0, rank0), (R_RANK1, rank1), (R_G0, g0), (R_G1, g1)):
        route = jnp.where(lane == ln, val, route)
    route_ref[...] = route


def _merge_layer(outs, lses, h, w_o, ffn_norm, w_router):
    T, D = h.shape
    tm = MERGE_ROWS
    r_pad = jnp.zeros((D, LANES), jnp.float32).at[:, :N_EXPERTS].set(w_router)
    r2 = jnp.concatenate(_split_bf16(r_pad), axis=1)
    head_of_col = jnp.arange(D, dtype=jnp.int32) // HEAD_DIM
    lane_id = jnp.arange(LANES, dtype=jnp.int32)[:, None]
    expand = ((lane_id < 2 * HEADS_PER_GROUP)
              & (lane_id % HEADS_PER_GROUP == head_of_col[None, :])).astype(jnp.bfloat16)
    row = lambda i: (i, 0)
    const = lambda i: (0, 0)
    wide = pl.BlockSpec((tm, D), row)
    narrow = pl.BlockSpec((tm, LANES), row)
    slabs = pl.BlockSpec((N_SLABS, tm, LANES), lambda i: (0, i, 0))
    return pl.pallas_call(
        _merge_kernel,
        grid=(T // tm,),
        in_specs=[slabs, slabs, slabs, narrow, narrow, narrow, wide,
                  pl.BlockSpec((D, D), const), pl.BlockSpec((1, D), const),
                  pl.BlockSpec((D, 2 * LANES), const), pl.BlockSpec((LANES, D), const)],
        out_specs=[wide, wide, narrow, pl.BlockSpec((8, LANES), const)],
        out_shape=[jax.ShapeDtypeStruct((T, D), jnp.float32),
                   jax.ShapeDtypeStruct((T, D), jnp.float32),
                   jax.ShapeDtypeStruct((T, LANES), jnp.float32),
                   jax.ShapeDtypeStruct((8, LANES), jnp.float32)],
        scratch_shapes=[pltpu.VMEM((8, LANES), jnp.float32)],
        compiler_params=pltpu.CompilerParams(
            dimension_semantics=("arbitrary",), vmem_limit_bytes=VMEM_LIMIT),
        name="merge_outproj_router",
    )(*outs, *lses, h, w_o, ffn_norm.reshape(1, D), r2, expand)


def _dispatch_kernel(dest_ref, pad_start_ref, pad_len_ref, used_ref, hn_ref, side_in_ref,
                     xs_ref, side_out_ref, zbuf, sem, zsem):
    side_out_ref[...] = side_in_ref[...].astype(side_out_ref.dtype)
    i = pl.program_id(0)
    tile = hn_ref.shape[0]
    base = i * tile
    n_tiles = xs_ref.shape[0] // MOE_ROWS

    def zero_fill(act):
        for e in range(N_EXPERTS):
            run_start = pad_start_ref[e]
            head = (-run_start) & (SUBLANES - 1)
            for row in range(SUBLANES - 1):
                @pl.when(row < head)
                def _():
                    act(pltpu.make_async_copy(zbuf.at[pl.ds(0, 1), :],
                                              xs_ref.at[pl.ds(run_start + row, 1), :], zsem))
            body_start = run_start + head
            body_len = pad_len_ref[e] - head
            bit = MOE_ROWS // 2
            while bit >= SUBLANES:
                off = pl.multiple_of(body_start + (body_len & -(2 * bit)), SUBLANES)

                @pl.when((body_len & bit) != 0)
                def _():
                    act(pltpu.make_async_copy(zbuf.at[pl.ds(0, bit), :],
                                              xs_ref.at[pl.ds(off, bit), :], zsem))
                bit //= 2
        for t in range(n_tiles):
            @pl.when(t >= used_ref[0])
            def _():
                act(pltpu.make_async_copy(zbuf, xs_ref.at[pl.ds(t * MOE_ROWS, MOE_ROWS), :], zsem))

    @pl.when(i == 0)
    def _():
        zbuf[...] = jnp.zeros_like(zbuf)
        zero_fill(lambda c: c.start())

    def start(t, c):
        for k in range(TOP_K):
            pltpu.make_async_copy(hn_ref.at[pl.ds(t, 1), :],
                                  xs_ref.at[pl.ds(dest_ref[TOP_K * (base + t) + k], 1), :],
                                  sem).start()
        return c

    lax.fori_loop(0, tile, start, 0, unroll=DMA_UNROLL)
    for k in range(TOP_K):
        pltpu.make_async_copy(hn_ref, xs_ref.at[pl.ds(0, tile), :], sem).wait()

    @pl.when(i == 0)
    def _():
        zero_fill(lambda c: c.wait())


def _dispatch(dest_flat, pad_start, pad_len, used, hn, n_slots, side):
    T, D = hn.shape
    side_spec, side_shape = _side_cast_specs(side, T // DISPATCH_ROWS)
    return pl.pallas_call(
        _dispatch_kernel,
        grid_spec=pltpu.PrefetchScalarGridSpec(
            num_scalar_prefetch=4,
            grid=(T // DISPATCH_ROWS,),
            in_specs=[pl.BlockSpec((DISPATCH_ROWS, D), lambda i, *_: (i, 0)), side_spec],
            out_specs=[pl.BlockSpec(memory_space=pl.ANY), side_spec],
            scratch_shapes=[pltpu.VMEM((MOE_ROWS, D), hn.dtype),
                            pltpu.SemaphoreType.DMA(()), pltpu.SemaphoreType.DMA(())],
        ),
        out_shape=[jax.ShapeDtypeStruct((n_slots, D), hn.dtype), side_shape],
        compiler_params=pltpu.CompilerParams(
            dimension_semantics=("arbitrary",), vmem_limit_bytes=VMEM_LIMIT),
        name="moe_dispatch",
    )(dest_flat, pad_start, pad_len, used, hn, side)


def _expert_kernel(tile_e_ref, used_ref, valid_ref, x_ref, wg_ref, wu_ref, wd_ref, y_ref,
                   xb_ref, hid_ref):
    del tile_e_ref
    i = pl.program_id(0)
    s = pl.program_id(1)
    n_f, tm, tf = hid_ref.shape
    active = i < used_ref[0]
    n_valid = valid_ref[jnp.minimum(i, used_ref[0] - 1)]
    full = n_valid == tm
    gate_step = jnp.logical_and(active, s < n_f)
    down_step = jnp.logical_and(active, s >= n_f)

    @pl.when(jnp.logical_and(active, s == 0))
    def _():
        xb_ref[...] = x_ref[...].astype(xb_ref.dtype)

    def gate_up(rows):
        x = xb_ref[rows, :]
        for c0 in range(0, tf, MOE_CHUNK):
            cols = slice(c0, min(c0 + MOE_CHUNK, tf))
            g = jnp.dot(x, wg_ref[:, cols], preferred_element_type=jnp.float32)
            u = jnp.dot(x, wu_ref[:, cols], preferred_element_type=jnp.float32)
            hid_ref[s, rows, cols] = (_silu(g) * u).astype(hid_ref.dtype)

    def down(rows):
        hid = jnp.concatenate([hid_ref[c, rows, :] for c in range(n_f)], axis=1)
        y_ref[rows, :] = jnp.dot(hid, wd_ref[...], preferred_element_type=jnp.float32)

    @pl.when(jnp.logical_and(gate_step, full))
    def _():
        gate_up(slice(None))

    @pl.when(jnp.logical_and(down_step, full))
    def _():
        down(slice(None))

    for r0 in range(0, tm, MOE_SUBROWS):
        rows = slice(r0, r0 + MOE_SUBROWS)
        occupied = jnp.logical_and(jnp.logical_not(full), r0 < n_valid)

        @pl.when(jnp.logical_and(gate_step, occupied))
        def _():
            gate_up(rows)

        @pl.when(jnp.logical_and(down_step, occupied))
        def _():
            down(rows)

        @pl.when(jnp.logical_and(down_step, r0 >= n_valid))
        def _():
            y_ref[rows, :] = jnp.zeros((MOE_SUBROWS, y_ref.shape[1]), y_ref.dtype)

    @pl.when(jnp.logical_and(jnp.logical_not(active), s >= n_f))
    def _():
        y_ref[...] = jnp.zeros_like(y_ref)


def _expert_ffn(tile_e, used, tile_valid, xs, w_gate, w_up, w_down):
    n_slots, D = xs.shape
    F = w_gate.shape[2]
    tm, tf, tn = MOE_ROWS, MOE_FT, MOE_NT
    n_f, n_n = F // tf, D // tn
    n_tiles = n_slots // tm
    last = n_f + n_n - 1

    def eff(i, s, used_ref):
        idle = i >= used_ref[0]
        return jnp.where(idle, used_ref[0] - 1, i), jnp.where(idle, last, s)

    def x_map(i, s, te, us, nv):
        ie, se = eff(i, s, us)
        return jnp.minimum(ie + jnp.minimum(se, 1), us[0] - 1), 0

    def gate_map(i, s, te, us, nv):
        ie, se = eff(i, s, us)
        return te[ie], 0, jnp.minimum(se, n_f - 1)

    def down_map(i, s, te, us, nv):
        ie, se = eff(i, s, us)
        tile = jnp.where(se == 0, jnp.maximum(ie - 1, 0), ie)
        return te[tile], 0, jnp.where(se == 0, n_n - 1, jnp.clip(se - n_f, 0, n_n - 1))

    def out_map(i, s, te, us, nv):
        return i, jnp.clip(s - n_f, 0, n_n - 1)

    return pl.pallas_call(
        _expert_kernel,
        grid_spec=pltpu.PrefetchScalarGridSpec(
            num_scalar_prefetch=3,
            grid=(n_tiles, n_f + n_n),
            in_specs=[
                pl.BlockSpec((tm, D), x_map),
                pl.BlockSpec((None, D, tf), gate_map),
                pl.BlockSpec((None, D, tf), gate_map),
                pl.BlockSpec((None, F, tn), down_map),
            ],
            out_specs=pl.BlockSpec((tm, tn), out_map),
            scratch_shapes=[pltpu.VMEM((tm, D), jnp.bfloat16),
                            pltpu.VMEM((n_f, tm, tf), jnp.bfloat16)],
        ),
        out_shape=jax.ShapeDtypeStruct((n_slots, D), jnp.float32),
        compiler_params=pltpu.CompilerParams(
            dimension_semantics=("arbitrary", "arbitrary"), vmem_limit_bytes=VMEM_LIMIT),
        name="moe_experts",
    )(tile_e, used, tile_valid, xs, w_gate, w_up, w_down)


def _combine_kernel(dest_ref, h_ref, route_ref, fin_ref, ys_ref, o_ref, ybuf, sems):
    i = pl.program_id(0)
    n_steps = pl.num_programs(0)
    tile = h_ref.shape[0]

    def issue(step, slot):
        base = step * tile

        def start(t, c):
            for k in range(TOP_K):
                pltpu.make_async_copy(
                    ys_ref.at[pl.ds(dest_ref[TOP_K * (base + t) + k], 1), :],
                    ybuf.at[slot, k, pl.ds(t, 1), :], sems.at[slot]).start()
            return c

        lax.fori_loop(0, tile, start, 0, unroll=DMA_UNROLL)

    slot = lax.rem(i, 2)

    @pl.when(i == 0)
    def _():
        issue(0, 0)

    @pl.when(i + 1 < n_steps)
    def _():
        issue(i + 1, 1 - slot)

    for k in range(TOP_K):
        pltpu.make_async_copy(ys_ref.at[pl.ds(0, tile), :], ybuf.at[slot, k], sems.at[slot]).wait()
    route = route_ref[...]
    g0 = route[:, R_G0:R_G0 + 1]
    g1 = route[:, R_G1:R_G1 + 1]
    h = h_ref[...] + (g0 * ybuf[slot, 0] + g1 * ybuf[slot, 1])
    o_ref[...] = h * _rms_scale(h) * fin_ref[...]


def _combine(dest_flat, h, route, final_norm, ys):
    T, D = h.shape
    tc = COMBINE_ROWS
    return pl.pallas_call(
        _combine_kernel,
        grid_spec=pltpu.PrefetchScalarGridSpec(
            num_scalar_prefetch=1,
            grid=(T // tc,),
            in_specs=[pl.BlockSpec((tc, D), lambda i, d: (i, 0)),
                      pl.BlockSpec((tc, LANES), lambda i, d: (i, 0)),
                      pl.BlockSpec((1, D), lambda i, d: (0, 0)),
                      pl.BlockSpec(memory_space=pl.ANY)],
            out_specs=pl.BlockSpec((tc, D), lambda i, d: (i, 0)),
            scratch_shapes=[pltpu.VMEM((2, TOP_K, tc, D), jnp.float32),
                            pltpu.SemaphoreType.DMA((2,))],
        ),
        out_shape=jax.ShapeDtypeStruct((T, D), jnp.float32),
        compiler_params=pltpu.CompilerParams(
            dimension_semantics=("arbitrary",), vmem_limit_bytes=VMEM_LIMIT),
        name="moe_combine_final_norm",
    )(dest_flat, h, route, final_norm.reshape(1, D), ys)


def _routing_tables(route, counts_row):
    T = route.shape[0]
    counts = counts_row[0, :N_EXPERTS].astype(jnp.int32)
    padded = ((counts + MOE_ROWS - 1) // MOE_ROWS) * MOE_ROWS
    ends = jnp.cumsum(padded)
    starts = ends - padded
    expert = route[:, R_E0:R_E1 + 1].astype(jnp.int32)
    rank = route[:, R_RANK0:R_RANK1 + 1].astype(jnp.int32)
    onehot = expert[..., None] == jnp.arange(N_EXPERTS, dtype=jnp.int32)
    dest = jnp.sum(jnp.where(onehot, starts, 0), axis=-1) + rank
    n_tiles = (TOP_K * T) // MOE_ROWS + N_EXPERTS
    tile_start = jnp.arange(n_tiles, dtype=jnp.int32) * MOE_ROWS
    tile_e = jnp.minimum(jnp.sum(ends[None, :] <= tile_start[:, None], axis=1),
                         N_EXPERTS - 1).astype(jnp.int32)
    used = (ends[-1:] // MOE_ROWS).astype(jnp.int32)
    tile_valid = jnp.clip(counts[tile_e] - (tile_start - starts[tile_e]), 0, MOE_ROWS).astype(jnp.int32)
    pad_start = (starts + counts).astype(jnp.int32)
    pad_len = (padded - counts).astype(jnp.int32)
    return dest.reshape(-1), tile_e, used, tile_valid, pad_start, pad_len, n_tiles * MOE_ROWS


def kernel(x, a_norm, a_proj, a_scale, kv_norm, w_kv, b_norm, w_q, w_o, rel_bias, ffn_norm,
           dense_w_gate, dense_w_up, dense_w_down, moe_router, moe_w_gate, moe_w_up,
           moe_w_down, final_norm):
    B, S, D = x.shape
    T = B * S
    E, _, F = moe_w_gate[0].shape
    h2, (wg16, wq16, wkv16, wo16) = _layer0(
        x.reshape(T, D), a_norm[0], a_proj[0], a_scale[0], ffn_norm[0],
        dense_w_gate[0], dense_w_up[0], dense_w_down[0],
        (moe_w_gate[0].reshape(E * D, F), w_q[0], w_kv, w_o[0]))
    qkv, wu16 = _qkv_project(h2, kv_norm, b_norm[0], wq16, wkv16, B, S,
                             moe_w_up[0].reshape(E * D, F))
    outs, stats = [], []
    for g, (q, k, v) in enumerate(qkv):
        o, st = _attention_group(q, k, v, rel_bias, g, B, S)
        outs.append(o)
        stats.append(st)
    h3, hn3, route, counts = _merge_layer(outs, stats, h2, wo16, ffn_norm[1], moe_router[0])
    dest, tile_e, used, tile_valid, pad_start, pad_len, n_slots = _routing_tables(route, counts)
    xs, wd16 = _dispatch(dest, pad_start, pad_len, used, hn3, n_slots,
                         moe_w_down[0].reshape(E * F, D))
    ys = _expert_ffn(tile_e, used, tile_valid, xs, wg16.reshape(E, D, F), wu16.reshape(E, D, F),
                     wd16.reshape(E, F, D))
    out = _combine(dest, h3, route, final_norm, ys)
    return out.reshape(B, S, D)
```

```python
import functools
import math

import jax
import jax.numpy as jnp
from jax import lax
from jax.experimental import pallas as pl
from jax.experimental.pallas import tpu as pltpu

D_MODEL = 1024
EPS = 1e-6
POOL_WINDOWS = (2, 4, 8, 16)
POOL_GROUP_DIM = D_MODEL // len(POOL_WINDOWS)
MAX_POOL_WINDOW = max(POOL_WINDOWS)
HEAD_DIM = 64
HEADS_PER_GROUP = D_MODEL // HEAD_DIM
DILATED_PATTERNS = ((128, 1), (512, 4), (2048, 16))
N_ATT_GROUPS = len(DILATED_PATTERNS)
QBLK = 128
NEG_INF = -1e30
LOG2E = 1.4426950408889634
Q_SCALE = HEAD_DIM ** -0.5 * LOG2E
N_BUCKETS = 32
MAX_EXACT = N_BUCKETS // 2
MAX_DISTANCE = 2048
N_EXPERTS = 8
TOP_K = 2
SEQ_LEN = 2048

LANES = 128
SUBLANES = 8
N_SLABS = D_MODEL // LANES
HEAD_PAIR = 2 * HEAD_DIM
assert HEAD_PAIR == LANES

LAYER0_ROWS = 512
QKV_ROWS = 512
ATTN_BLOCKS_PER_STEP = 4
MERGE_ROWS = 512
MOE_ROWS = 1024
MOE_FT = 1792
MOE_CHUNK = 512
MOE_SUBROWS = 256
MOE_NT = 1024
STAGE_ROWS = 1152
VMEM_LIMIT = 60 * 1024 * 1024

R_E0, R_E1, R_RANK0, R_RANK1, R_G0, R_G1 = 0, 1, 2, 3, 4, 5
T_BASE, T_COUNT = 0, 1


def _rms_scale(x):
    return lax.rsqrt(jnp.mean(x * x, axis=-1, keepdims=True) + EPS)


def _silu(g):
    return g * (1.0 / (1.0 + jnp.exp(-g)))


def _split_bf16(x):
    hi = x.astype(jnp.bfloat16)
    lo = (x - hi.astype(jnp.float32)).astype(jnp.bfloat16)
    return hi, lo


def _side_cast_specs(side, n_steps):
    rows = side.shape[0] // n_steps
    spec = pl.BlockSpec((rows, side.shape[1]), lambda i, *_: (i, 0))
    return spec, jax.ShapeDtypeStruct(side.shape, jnp.bfloat16)


def _pool_mix(x, halo, seq_tile, gain, proj_ref, scale):
    xn = x * _rms_scale(x) * gain
    hn_halo = halo * _rms_scale(halo) * gain
    hn_halo = jnp.where(seq_tile > 0, hn_halo, 0.0)
    full = jnp.concatenate([hn_halo, xn], axis=0)
    ts = x.shape[0]
    pos = seq_tile * ts + lax.broadcasted_iota(jnp.int32, (ts, 1), 0)
    outs = []
    for g, w in enumerate(POOL_WINDOWS):
        c0 = g * POOL_GROUP_DIM
        s = full[:, c0:c0 + POOL_GROUP_DIM]
        span = 1
        while span < w:
            s = s + pltpu.roll(s, span, axis=0)
            span *= 2
        s = s[MAX_POOL_WINDOW:, :]
        cnt = jnp.minimum(pos + 1, w).astype(jnp.float32)
        pooled = s / cnt - xn[:, c0:c0 + POOL_GROUP_DIM]
        outs.append(jnp.dot(pooled.astype(jnp.bfloat16), proj_ref[g],
                            preferred_element_type=jnp.float32))
    return x + jnp.concatenate(outs, axis=1) * scale


def _layer0_kernel(*refs, tiles_per_seq, n_side):
    (x_ref, halo_ref, an_ref, proj_ref, asc_ref, fn_ref, wg_ref, wu_ref, wd_ref) = refs[:9]
    side_in = refs[9:9 + n_side]
    h_out_ref = refs[9 + n_side]
    side_out = refs[10 + n_side:]
    for src, dst in zip(side_in, side_out):
        dst[...] = src[...].astype(dst.dtype)
    seq_tile = lax.rem(pl.program_id(0), tiles_per_seq)
    h = _pool_mix(x_ref[...], halo_ref[...], seq_tile, an_ref[...], proj_ref, asc_ref[...])
    hn = (h * _rms_scale(h) * fn_ref[...]).astype(jnp.bfloat16)
    g = jnp.dot(hn, wg_ref[...], preferred_element_type=jnp.float32)
    u = jnp.dot(hn, wu_ref[...], preferred_element_type=jnp.float32)
    hid = (_silu(g) * u).astype(jnp.bfloat16)
    h_out_ref[...] = h + jnp.dot(hid, wd_ref[...], preferred_element_type=jnp.float32)


def _layer0(x, a_norm, a_proj, a_scale, ffn_norm, w_gate, w_up, w_down, sides):
    T, D = x.shape
    F = w_gate.shape[1]
    ts = LAYER0_ROWS
    n_steps = T // ts
    halo_blocks = ts // MAX_POOL_WINDOW
    row = lambda i: (i, 0)
    vec = pl.BlockSpec((1, D), lambda i: (0, 0))
    resident = dict(index_map=lambda i: (0, 0), pipeline_mode=pl.Buffered(1))
    side_specs, side_shapes = zip(*[_side_cast_specs(a, n_steps) for a in sides])
    outs = pl.pallas_call(
        functools.partial(_layer0_kernel, tiles_per_seq=SEQ_LEN // ts, n_side=len(sides)),
        grid=(n_steps,),
        in_specs=[
            pl.BlockSpec((ts, D), row),
            pl.BlockSpec((MAX_POOL_WINDOW, D), lambda i: (jnp.maximum(i * halo_blocks - 1, 0), 0)),
            vec,
            pl.BlockSpec(a_proj.shape, lambda i: (0, 0, 0), pipeline_mode=pl.Buffered(1)),
            vec, vec,
            pl.BlockSpec((D, F), **resident),
            pl.BlockSpec((D, F), **resident),
            pl.BlockSpec((F, D), **resident),
            *side_specs,
        ],
        out_specs=[pl.BlockSpec((ts, D), row), *side_specs],
        out_shape=[jax.ShapeDtypeStruct((T, D), jnp.float32), *side_shapes],
        compiler_params=pltpu.CompilerParams(
            dimension_semantics=("arbitrary",), vmem_limit_bytes=VMEM_LIMIT),
        name="pool_mixer_dense_swiglu",
    )(x, x, a_norm.reshape(1, D), a_proj.astype(jnp.bfloat16), a_scale.reshape(1, D),
      ffn_norm.reshape(1, D), w_gate.astype(jnp.bfloat16), w_up.astype(jnp.bfloat16),
      w_down.astype(jnp.bfloat16), *sides)
    return outs[0], outs[1:]


def _qkv_kernel(*refs):
    slabs = refs[:N_SLABS]
    kvg_ref, qg_ref, wq_ref, wkv_ref, side_in_ref = refs[N_SLABS:N_SLABS + 5]
    out_refs = refs[N_SLABS + 5:-1]
    side_out_ref = refs[-1]
    side_out_ref[...] = side_in_ref[...].astype(side_out_ref.dtype)
    tm = slabs[0].shape[0]
    for g, (_, dilation) in enumerate(DILATED_PATTERNS):
        n = tm // dilation
        cols = []
        for slab in slabs:
            if dilation == 1:
                cols.append(slab[...])
            else:
                cols.append(jnp.concatenate(
                    [slab[pl.ds(r, n, stride=dilation), :] for r in range(dilation)], axis=0))
        x = jnp.concatenate(cols, axis=1)
        xs = x * _rms_scale(x)
        xq = (xs * qg_ref[...]).astype(jnp.bfloat16)
        xkv = (xs * kvg_ref[...]).astype(jnp.bfloat16)
        D = D_MODEL
        kcols = slice(g * D, (g + 1) * D)
        vcols = slice((N_ATT_GROUPS + g) * D, (N_ATT_GROUPS + g + 1) * D)
        q = jnp.dot(xq, wq_ref[:, kcols], preferred_element_type=jnp.float32) * Q_SCALE
        k = jnp.dot(xkv, wkv_ref[:, kcols], preferred_element_type=jnp.float32)
        v = jnp.dot(xkv, wkv_ref[:, vcols], preferred_element_type=jnp.float32)
        for out_ref, val in zip(out_refs[3 * g:3 * g + 3], (q, k, v)):
            for r in range(dilation):
                out_ref[r] = val[r * n:(r + 1) * n].astype(out_ref.dtype)


def _qkv_project(h, kv_norm, b_norm, w_q, w_kv, B, S, side):
    T, D = h.shape
    tm = QKV_ROWS
    tiles_per_seq = S // tm
    resident = dict(index_map=lambda i: (0, 0), pipeline_mode=pl.Buffered(1))
    slab_specs = [pl.BlockSpec((tm, LANES), functools.partial(lambda i, c: (i, c), c=c))
                  for c in range(N_SLABS)]
    vec = pl.BlockSpec((1, D), lambda i: (0, 0))
    out_specs, out_shapes = [], []
    for _, d in DILATED_PATTERNS:
        spec = pl.BlockSpec((None, d, tm // d, D),
                            lambda i: (i // tiles_per_seq, 0, i % tiles_per_seq, 0))
        out_specs += [spec] * 3
        out_shapes += [jax.ShapeDtypeStruct((B, d, S // d, D), jnp.bfloat16)] * 3
    side_spec, side_shape = _side_cast_specs(side, T // tm)
    outs = pl.pallas_call(
        _qkv_kernel,
        grid=(T // tm,),
        in_specs=slab_specs + [vec, vec, pl.BlockSpec(w_q.shape, **resident),
                               pl.BlockSpec(w_kv.shape, **resident), side_spec],
        out_specs=out_specs + [side_spec],
        out_shape=out_shapes + [side_shape],
        compiler_params=pltpu.CompilerParams(
            dimension_semantics=("arbitrary",), vmem_limit_bytes=VMEM_LIMIT),
        name="qkv_proj",
    )(*([h] * N_SLABS), kv_norm.reshape(1, D), b_norm.reshape(1, D), w_q, w_kv, side)
    return [tuple(outs[3 * g:3 * g + 3]) for g in range(N_ATT_GROUPS)], outs[-1]


def _attn_kernel(*refs, dilation, has_prev, jb, rb):
    if has_prev:
        (table_ref, bmap_ref, q_ref, kp_ref, kc_ref, vp_ref, vc_ref,
         o_ref, stat_ref, bias_ref, stat_scr) = refs
    else:
        table_ref, bmap_ref, q_ref, kc_ref, vc_ref, o_ref, stat_ref, bias_ref, stat_scr = refs
    b, jstep, rstep = pl.program_id(0), pl.program_id(1), pl.program_id(2)
    n_keys = bias_ref.shape[2]

    @pl.when(jnp.logical_and(jnp.logical_and(b == 0, jstep == 0), rstep == 0))
    def _():
        stat_scr[...] = jnp.zeros_like(stat_scr)
        bmap = bmap_ref[...]
        in_prev = lax.broadcasted_iota(jnp.int32, bmap.shape, 1) < n_keys - QBLK

        def head(h, carry):
            acc = jnp.full(bmap.shape, NEG_INF, jnp.float32)
            for bucket in range(N_BUCKETS):
                acc = jnp.where(bmap == bucket, table_ref[bucket, h], acc)
            rows_h = pl.ds(pl.multiple_of(h * QBLK, QBLK), QBLK)
            bias_ref[0, rows_h, :] = jnp.where(in_prev, NEG_INF, acc)
            bias_ref[1, rows_h, :] = acc
            return carry

        lax.fori_loop(0, HEADS_PER_GROUP, head, 0)

    lane = lax.broadcasted_iota(jnp.int32, (QBLK, LANES), 1)
    first_head = lane < HEAD_DIM
    for rr in range(rb):
        for jj in range(jb):
            qrows = slice(jj * QBLK, (jj + 1) * QBLK)
            bias_copy = jnp.minimum(jstep, 1) if jj == 0 else 1
            if dilation > 1:
                out_rows = pl.ds(dilation * QBLK * jj + rstep * rb + rr, QBLK, stride=dilation)
            else:
                out_rows = qrows
            for hp in range(HEADS_PER_GROUP // 2):
                sl = slice(hp * HEAD_PAIR, (hp + 1) * HEAD_PAIR)
                qp = q_ref[rr, qrows, sl]
                zero = jnp.zeros_like(qp)
                q2 = jnp.concatenate([jnp.where(first_head, qp, zero),
                                      jnp.where(first_head, zero, qp)], axis=0)
                if has_prev:
                    if jj == 0:
                        k_prev, v_prev = kp_ref[rr, :, sl], vp_ref[rr, :, sl]
                    else:
                        prows = slice((jj - 1) * QBLK, jj * QBLK)
                        k_prev, v_prev = kc_ref[rr, prows, sl], vc_ref[rr, prows, sl]
                    kcat = jnp.concatenate([k_prev, kc_ref[rr, qrows, sl]], axis=0)
                    vcat = jnp.concatenate([v_prev, vc_ref[rr, qrows, sl]], axis=0)
                else:
                    kcat, vcat = kc_ref[rr, qrows, sl], vc_ref[rr, qrows, sl]
                s = lax.dot_general(q2, kcat, (((1,), (1,)), ((), ())),
                                    preferred_element_type=jnp.float32)
                s = s + bias_ref[bias_copy, hp * 2 * QBLK:(hp + 1) * 2 * QBLK, :]
                m = jnp.max(s, axis=1, keepdims=True)
                p = jnp.exp2(s - m)
                l = jnp.sum(p, axis=1, keepdims=True)
                o2 = jnp.dot(p.astype(jnp.bfloat16), vcat, preferred_element_type=jnp.float32)
                o_ref[hp, out_rows, :] = jnp.where(first_head, o2[:QBLK], o2[QBLK:])
                for half, head in ((slice(0, QBLK), 2 * hp), (slice(QBLK, 2 * QBLK), 2 * hp + 1)):
                    stat_scr[:, head:head + 1] = m[half]
                    stat_scr[:, HEADS_PER_GROUP + head:HEADS_PER_GROUP + head + 1] = l[half]
            stat_ref[out_rows, :] = stat_scr[...]


def _bucket_map(dilation, window, n_keys):
    W = window // dilation
    a = jnp.arange(QBLK, dtype=jnp.int32)[:, None]
    c = jnp.arange(n_keys, dtype=jnp.int32)[None, :]
    m = a + (n_keys - QBLK) - c
    band = (m >= 0) & (m <= W)
    n = jnp.maximum(m, 0) * dilation
    nf = jnp.maximum(n, 1).astype(jnp.float32)
    large = MAX_EXACT + (jnp.log(nf / MAX_EXACT) / math.log(MAX_DISTANCE / MAX_EXACT)
                         * (N_BUCKETS - MAX_EXACT)).astype(jnp.int32)
    large = jnp.minimum(large, N_BUCKETS - 1)
    bucket = jnp.where(n < MAX_EXACT, n, large)
    return jnp.where(band, bucket, -1)


def _attention_group(q, k, v, rel_bias, g, B, S):
    window, d = DILATED_PATTERNS[g]
    L = S // d
    n = L // QBLK
    D = D_MODEL
    has_prev = n > 1
    n_keys = 2 * QBLK if has_prev else QBLK
    table = rel_bias[:, g * HEADS_PER_GROUP:(g + 1) * HEADS_PER_GROUP].astype(jnp.float32) * LOG2E
    bmap = _bucket_map(d, window, n_keys)
    jb = min(ATTN_BLOCKS_PER_STEP, n)
    rb = ATTN_BLOCKS_PER_STEP // jb
    cur_blk = (None, rb, jb * QBLK, D)
    prev_blk = (None, rb, QBLK, D)
    cur = lambda b, j, r: (b, r, j, 0)
    prev = lambda b, j, r: (b, r, jnp.maximum(j * jb - 1, 0), 0)
    if has_prev:
        kv_specs = [pl.BlockSpec(prev_blk, prev), pl.BlockSpec(cur_blk, cur),
                    pl.BlockSpec(prev_blk, prev), pl.BlockSpec(cur_blk, cur)]
        kv_args = (k, k, v, v)
    else:
        kv_specs = [pl.BlockSpec(cur_blk, cur), pl.BlockSpec(cur_blk, cur)]
        kv_args = (k, v)
    n_j = n // jb
    span = QBLK * jb * d
    return pl.pallas_call(
        functools.partial(_attn_kernel, dilation=d, has_prev=has_prev, jb=jb, rb=rb),
        grid=(B, n_j, d // rb),
        in_specs=[pl.BlockSpec(memory_space=pltpu.SMEM),
                  pl.BlockSpec(bmap.shape, lambda b, j, r: (0, 0)),
                  pl.BlockSpec(cur_blk, cur)] + kv_specs,
        out_specs=[pl.BlockSpec((N_SLABS, span, LANES), lambda b, j, r: (0, b * n_j + j, 0)),
                   pl.BlockSpec((span, LANES), lambda b, j, r: (b * n_j + j, 0))],
        out_shape=[jax.ShapeDtypeStruct((N_SLABS, B * S, LANES), jnp.float32),
                   jax.ShapeDtypeStruct((B * S, LANES), jnp.float32)],
        scratch_shapes=[pltpu.VMEM((2, HEADS_PER_GROUP * QBLK, n_keys), jnp.float32),
                        pltpu.VMEM((QBLK, LANES), jnp.float32)],
        compiler_params=pltpu.CompilerParams(
            dimension_semantics=("arbitrary",) * 3, vmem_limit_bytes=VMEM_LIMIT),
        name=f"dilated_attn_g{g}",
    )(table, bmap, q, *kv_args)


def _merge_kernel(o0_ref, o1_ref, o2_ref, s0_ref, s1_ref, s2_ref, h_ref, wo_ref, fn_ref,
                  r2_ref, expand_ref,
                  h_out_ref, hn_ref, route_ref, route_t_ref, tinfo_ref, counts_ref, carry_ref):
    i = pl.program_id(0)
    tm = h_ref.shape[0]

    @pl.when(i == 0)
    def _():
        carry_ref[...] = jnp.zeros_like(carry_ref)

    lane = lax.broadcasted_iota(jnp.int32, (tm, LANES), 1)
    head_lane = lane < HEADS_PER_GROUP
    stats = [s0_ref[...], s1_ref[...], s2_ref[...]]
    dens = [pltpu.roll(st, LANES - HEADS_PER_GROUP, axis=1) for st in stats]
    mx = jnp.maximum(jnp.maximum(stats[0], stats[1]), stats[2])
    es = [jnp.exp2(st - mx) for st in stats]
    inv = 1.0 / (es[0] * dens[0] + es[1] * dens[1] + es[2] * dens[2])
    merged = None
    for e, o_ref in zip(es, (o0_ref, o1_ref, o2_ref)):
        w = jnp.where(head_lane, e * inv, 0.0)
        hi = w.astype(jnp.bfloat16).astype(jnp.float32)
        packed = (hi + pltpu.roll(w - hi, HEADS_PER_GROUP, axis=1)).astype(jnp.bfloat16)
        wide = jnp.dot(packed, expand_ref[...], preferred_element_type=jnp.float32)
        o = jnp.concatenate([o_ref[c] for c in range(N_SLABS)], axis=1)
        term = wide * o
        merged = term if merged is None else merged + term
    h = h_ref[...] + jnp.dot(merged.astype(jnp.bfloat16), wo_ref[...],
                             preferred_element_type=jnp.float32)
    h_out_ref[...] = h
    hn = h * _rms_scale(h) * fn_ref[...]
    hn_ref[...] = hn.astype(hn_ref.dtype)

    hi, lo = _split_bf16(hn)
    both = jnp.dot(hi, r2_ref[...], preferred_element_type=jnp.float32)
    logits = (both[:, :LANES] + both[:, LANES:]
              + jnp.dot(lo, r2_ref[:, :LANES], preferred_element_type=jnp.float32))
    logits = jnp.where(lane < N_EXPERTS, logits, -jnp.inf)
    v0 = jnp.max(logits, axis=1, keepdims=True)
    e0 = jnp.min(jnp.where(logits == v0, lane, LANES), axis=1, keepdims=True)
    rest = jnp.where(lane == e0, -jnp.inf, logits)
    v1 = jnp.max(rest, axis=1, keepdims=True)
    e1 = jnp.min(jnp.where(rest == v1, lane, LANES), axis=1, keepdims=True)
    t = jnp.exp(v1 - v0)
    g0 = 1.0 / (1.0 + t)
    g1 = t / (1.0 + t)

    hit0 = lane == e0
    hit1 = lane == e1
    onehot = jnp.where(jnp.logical_or(hit0, hit1), 1.0, 0.0)
    r_i = lax.broadcasted_iota(jnp.int32, (tm, tm), 0)
    c_i = lax.broadcasted_iota(jnp.int32, (tm, tm), 1)
    lower = jnp.where(c_i < r_i, 1.0, 0.0).astype(jnp.bfloat16)
    before = jnp.dot(lower, onehot.astype(jnp.bfloat16), preferred_element_type=jnp.float32)
    rank0 = jnp.sum(jnp.where(hit0, before, 0.0), axis=1, keepdims=True)
    rank1 = jnp.sum(jnp.where(hit1, before, 0.0), axis=1, keepdims=True)
    tile_counts = jnp.sum(onehot, axis=0, keepdims=True)
    base = carry_ref[0:1, :]
    total = base + jnp.floor((tile_counts + (SUBLANES - 1)) * (1.0 / SUBLANES)) * SUBLANES
    carry_ref[...] = jnp.broadcast_to(total, carry_ref.shape)
    counts_ref[...] = jnp.broadcast_to(total, counts_ref.shape)
    row_id = lax.broadcasted_iota(jnp.int32, tinfo_ref.shape, 0)
    tinfo_ref[...] = jnp.where(row_id == T_BASE, base, jnp.where(row_id == T_COUNT, tile_counts, 0.0))

    route = jnp.zeros((tm, LANES), jnp.float32)
    for ln, val in ((R_E0, e0.astype(jnp.float32)), (R_E1, e1.astype(jnp.float32)),
                    (R_RANK0, rank0), (R_RANK1, rank1), (R_G0, g0), (R_G1, g1)):
        route = jnp.where(lane == ln, val, route)
    route_ref[...] = route
    route_t_ref[...] = route.T[:SUBLANES, :]


def _merge_layer(outs, lses, h, w_o, ffn_norm, w_router):
    T, D = h.shape
    tm = MERGE_ROWS
    r_pad = jnp.zeros((D, LANES), jnp.float32).at[:, :N_EXPERTS].set(w_router)
    r2 = jnp.concatenate(_split_bf16(r_pad), axis=1)
    head_of_col = jnp.arange(D, dtype=jnp.int32) // HEAD_DIM
    lane_id = jnp.arange(LANES, dtype=jnp.int32)[:, None]
    expand = ((lane_id < 2 * HEADS_PER_GROUP)
              & (lane_id % HEADS_PER_GROUP == head_of_col[None, :])).astype(jnp.bfloat16)
    row = lambda i: (i, 0)
    const = lambda i: (0, 0)
    wide = pl.BlockSpec((tm, D), row)
    narrow = pl.BlockSpec((tm, LANES), row)
    slabs = pl.BlockSpec((N_SLABS, tm, LANES), lambda i: (0, i, 0))
    return pl.pallas_call(
        _merge_kernel,
        grid=(T // tm,),
        in_specs=[slabs, slabs, slabs, narrow, narrow, narrow, wide,
                  pl.BlockSpec((D, D), const), pl.BlockSpec((1, D), const),
                  pl.BlockSpec((D, 2 * LANES), const), pl.BlockSpec((LANES, D), const)],
        out_specs=[wide, wide, narrow,
                   pl.BlockSpec((None, SUBLANES, tm), lambda i: (i, 0, 0)),
                   pl.BlockSpec((None, SUBLANES, LANES), lambda i: (i, 0, 0)),
                   pl.BlockSpec((8, LANES), const)],
        out_shape=[jax.ShapeDtypeStruct((T, D), jnp.float32),
                   jax.ShapeDtypeStruct((T, D), jnp.bfloat16),
                   jax.ShapeDtypeStruct((T, LANES), jnp.float32),
                   jax.ShapeDtypeStruct((T // tm, SUBLANES, tm), jnp.float32),
                   jax.ShapeDtypeStruct((T // tm, SUBLANES, LANES), jnp.float32),
                   jax.ShapeDtypeStruct((8, LANES), jnp.float32)],
        scratch_shapes=[pltpu.VMEM((8, LANES), jnp.float32)],
        compiler_params=pltpu.CompilerParams(
            dimension_semantics=("arbitrary",), vmem_limit_bytes=VMEM_LIMIT),
        name="merge_outproj_router",
    )(*outs, *lses, h, w_o, ffn_norm.reshape(1, D), r2, expand)


def _run_offsets(cnt_ref, tile):
    sizes, offs, off = [], [], 0
    for e in range(N_EXPERTS):
        size = ((cnt_ref[tile * N_EXPERTS + e] + (SUBLANES - 1)) // SUBLANES) * SUBLANES
        sizes.append(size)
        offs.append(off)
        off = off + size
    return sizes, offs


def _for_each_run_piece(start_ref, cnt_ref, tile, fn):
    sizes, offs = _run_offsets(cnt_ref, tile)
    for e in range(N_EXPERTS):
        slot0 = start_ref[tile * N_EXPERTS + e]
        bit = MERGE_ROWS
        while bit >= SUBLANES:
            done = sizes[e] & -(2 * bit)

            @pl.when((sizes[e] & bit) != 0)
            def _():
                fn(pl.multiple_of(offs[e] + done, SUBLANES),
                   pl.multiple_of(slot0 + done, SUBLANES), bit)
            bit //= 2


def _staging_columns(expert, rank, offs):
    col = rank
    for e in range(N_EXPERTS):
        col = col + jnp.where(expert == e, offs[e], 0)
    return col


def _dispatch_kernel(start_ref, cnt_ref, pad_start_ref, pad_len_ref, used_ref,
                     hn_ref, route_t_ref, side_in_ref, xs_ref, side_out_ref,
                     xbuf, zbuf, sems, zsem):
    side_out_ref[...] = side_in_ref[...].astype(side_out_ref.dtype)
    i = pl.program_id(0)
    n_steps = pl.num_programs(0)
    slot = lax.rem(i, 2)
    n_tiles = xs_ref.shape[0] // MOE_ROWS

    def zero_fill(act):
        for e in range(N_EXPERTS):
            run_start = pl.multiple_of(pad_start_ref[e], SUBLANES)
            run_len = pad_len_ref[e]
            bit = MOE_ROWS // 2
            while bit >= SUBLANES:
                off = pl.multiple_of(run_start + (run_len & -(2 * bit)), SUBLANES)

                @pl.when((run_len & bit) != 0)
                def _():
                    act(pltpu.make_async_copy(zbuf.at[pl.ds(0, bit), :],
                                              xs_ref.at[pl.ds(off, bit), :], zsem))
                bit //= 2
        for t in range(n_tiles):
            @pl.when(t >= used_ref[0])
            def _():
                act(pltpu.make_async_copy(zbuf, xs_ref.at[pl.ds(t * MOE_ROWS, MOE_ROWS), :], zsem))

    def run_copies(tile, buf_slot, act):
        def piece(buf_row, slot_row, rows):
            act(pltpu.make_async_copy(xbuf.at[buf_slot, pl.ds(buf_row, rows), :],
                                      xs_ref.at[pl.ds(slot_row, rows), :], sems.at[buf_slot]))
        _for_each_run_piece(start_ref, cnt_ref, tile, piece)

    @pl.when(i == 0)
    def _():
        zbuf[...] = jnp.zeros_like(zbuf)
        zero_fill(lambda c: c.start())

    @pl.when(i >= 2)
    def _():
        run_copies(i - 2, slot, lambda c: c.wait())

    _, offs = _run_offsets(cnt_ref, i)
    route_t = route_t_ref[...]
    cols = [_staging_columns(route_t[R_E0 + k:R_E0 + k + 1, :].astype(jnp.int32),
                             route_t[R_RANK0 + k:R_RANK0 + k + 1, :].astype(jnp.int32), offs)
            for k in range(TOP_K)]
    buf_row = lax.broadcasted_iota(jnp.int32, (STAGE_ROWS, hn_ref.shape[0]), 0)
    onehot = jnp.where(jnp.logical_or(buf_row == cols[0], buf_row == cols[1]), 1.0, 0.0)
    xbuf[slot] = jnp.dot(onehot.astype(jnp.bfloat16), hn_ref[...],
                         preferred_element_type=jnp.float32)
    run_copies(i, slot, lambda c: c.start())

    @pl.when(i == n_steps - 1)
    def _():
        @pl.when(i >= 1)
        def _():
            run_copies(i - 1, 1 - slot, lambda c: c.wait())
        run_copies(i, slot, lambda c: c.wait())
        zero_fill(lambda c: c.wait())


def _dispatch(run_start, run_cnt, pad_start, pad_len, used, hn, route_t, n_slots, side):
    T, D = hn.shape
    tm = MERGE_ROWS
    side_spec, side_shape = _side_cast_specs(side, T // tm)
    return pl.pallas_call(
        _dispatch_kernel,
        grid_spec=pltpu.PrefetchScalarGridSpec(
            num_scalar_prefetch=5,
            grid=(T // tm,),
            in_specs=[pl.BlockSpec((tm, D), lambda i, *_: (i, 0)),
                      pl.BlockSpec((None, SUBLANES, tm), lambda i, *_: (i, 0, 0)),
                      side_spec],
            out_specs=[pl.BlockSpec(memory_space=pl.ANY), side_spec],
            scratch_shapes=[pltpu.VMEM((2, STAGE_ROWS, D), jnp.float32),
                            pltpu.VMEM((MOE_ROWS, D), jnp.float32),
                            pltpu.SemaphoreType.DMA((2,)), pltpu.SemaphoreType.DMA(())],
        ),
        out_shape=[jax.ShapeDtypeStruct((n_slots, D), jnp.float32), side_shape],
        compiler_params=pltpu.CompilerParams(
            dimension_semantics=("arbitrary",), vmem_limit_bytes=VMEM_LIMIT),
        name="moe_dispatch",
    )(run_start, run_cnt, pad_start, pad_len, used, hn, route_t, side)


def _expert_kernel(tile_e_ref, used_ref, valid_ref, x_ref, wg_ref, wu_ref, wd_ref, y_ref,
                   xb_ref, hid_ref):
    del tile_e_ref
    i = pl.program_id(0)
    s = pl.program_id(1)
    n_f, tm, tf = hid_ref.shape
    active = i < used_ref[0]
    n_valid = valid_ref[jnp.minimum(i, used_ref[0] - 1)]
    full = n_valid == tm
    gate_step = jnp.logical_and(active, s < n_f)
    down_step = jnp.logical_and(active, s >= n_f)

    @pl.when(jnp.logical_and(active, s == 0))
    def _():
        xb_ref[...] = x_ref[...].astype(xb_ref.dtype)

    def gate_up(rows):
        x = xb_ref[rows, :]
        for c0 in range(0, tf, MOE_CHUNK):
            cols = slice(c0, min(c0 + MOE_CHUNK, tf))
            g = jnp.dot(x, wg_ref[:, cols], preferred_element_type=jnp.float32)
            u = jnp.dot(x, wu_ref[:, cols], preferred_element_type=jnp.float32)
            hid_ref[s, rows, cols] = (_silu(g) * u).astype(hid_ref.dtype)

    def down(rows):
        hid = jnp.concatenate([hid_ref[c, rows, :] for c in range(n_f)], axis=1)
        y_ref[rows, :] = jnp.dot(hid, wd_ref[...], preferred_element_type=jnp.float32)

    @pl.when(jnp.logical_and(gate_step, full))
    def _():
        gate_up(slice(None))

    @pl.when(jnp.logical_and(down_step, full))
    def _():
        down(slice(None))

    for r0 in range(0, tm, MOE_SUBROWS):
        rows = slice(r0, r0 + MOE_SUBROWS)
        occupied = jnp.logical_and(jnp.logical_not(full), r0 < n_valid)

        @pl.when(jnp.logical_and(gate_step, occupied))
        def _():
            gate_up(rows)

        @pl.when(jnp.logical_and(down_step, occupied))
        def _():
            down(rows)

        @pl.when(jnp.logical_and(down_step, r0 >= n_valid))
        def _():
            y_ref[rows, :] = jnp.zeros((MOE_SUBROWS, y_ref.shape[1]), y_ref.dtype)

    @pl.when(jnp.logical_and(jnp.logical_not(active), s >= n_f))
    def _():
        y_ref[...] = jnp.zeros_like(y_ref)


def _expert_ffn(tile_e, used, tile_valid, xs, w_gate, w_up, w_down):
    n_slots, D = xs.shape
    F = w_gate.shape[2]
    tm, tf, tn = MOE_ROWS, MOE_FT, MOE_NT
    n_f, n_n = F // tf, D // tn
    n_tiles = n_slots // tm
    last = n_f + n_n - 1

    def eff(i, s, used_ref):
        idle = i >= used_ref[0]
        return jnp.where(idle, used_ref[0] - 1, i), jnp.where(idle, last, s)

    def x_map(i, s, te, us, nv):
        ie, se = eff(i, s, us)
        return jnp.minimum(ie + jnp.minimum(se, 1), us[0] - 1), 0

    def gate_map(i, s, te, us, nv):
        ie, se = eff(i, s, us)
        return te[ie], 0, jnp.minimum(se, n_f - 1)

    def down_map(i, s, te, us, nv):
        ie, se = eff(i, s, us)
        tile = jnp.where(se == 0, jnp.maximum(ie - 1, 0), ie)
        return te[tile], 0, jnp.where(se == 0, n_n - 1, jnp.clip(se - n_f, 0, n_n - 1))

    def out_map(i, s, te, us, nv):
        return i, jnp.clip(s - n_f, 0, n_n - 1)

    return pl.pallas_call(
        _expert_kernel,
        grid_spec=pltpu.PrefetchScalarGridSpec(
            num_scalar_prefetch=3,
            grid=(n_tiles, n_f + n_n),
            in_specs=[
                pl.BlockSpec((tm, D), x_map),
                pl.BlockSpec((None, D, tf), gate_map),
                pl.BlockSpec((None, D, tf), gate_map),
                pl.BlockSpec((None, F, tn), down_map),
            ],
            out_specs=pl.BlockSpec((tm, tn), out_map),
            scratch_shapes=[pltpu.VMEM((tm, D), jnp.bfloat16),
                            pltpu.VMEM((n_f, tm, tf), jnp.bfloat16)],
        ),
        out_shape=jax.ShapeDtypeStruct((n_slots, D), jnp.float32),
        compiler_params=pltpu.CompilerParams(
            dimension_semantics=("arbitrary", "arbitrary"), vmem_limit_bytes=VMEM_LIMIT),
        name="moe_experts",
    )(tile_e, used, tile_valid, xs, w_gate, w_up, w_down)


def _combine_kernel(start_ref, cnt_ref, h_ref, route_ref, fin_ref, ys_ref, o_ref, ybuf, sems):
    i = pl.program_id(0)
    n_steps = pl.num_programs(0)
    tile = h_ref.shape[0]
    slot = lax.rem(i, 2)

    def run_copies(step, buf_slot, act):
        def piece(buf_row, slot_row, rows):
            act(pltpu.make_async_copy(ys_ref.at[pl.ds(slot_row, rows), :],
                                      ybuf.at[buf_slot, pl.ds(buf_row, rows), :],
                                      sems.at[buf_slot]))
        _for_each_run_piece(start_ref, cnt_ref, step, piece)

    @pl.when(i == 0)
    def _():
        ybuf[...] = jnp.zeros_like(ybuf)
        run_copies(0, 0, lambda c: c.start())

    @pl.when(i + 1 < n_steps)
    def _():
        run_copies(i + 1, 1 - slot, lambda c: c.start())

    run_copies(i, slot, lambda c: c.wait())
    _, offs = _run_offsets(cnt_ref, i)
    route = route_ref[...]
    staged = ybuf[slot].astype(jnp.bfloat16)
    buf_col = lax.broadcasted_iota(jnp.int32, (tile, STAGE_ROWS), 1)
    ys = []
    for k in range(TOP_K):
        col = _staging_columns(route[:, R_E0 + k:R_E0 + k + 1].astype(jnp.int32),
                               route[:, R_RANK0 + k:R_RANK0 + k + 1].astype(jnp.int32), offs)
        onehot = jnp.where(buf_col == col, 1.0, 0.0).astype(jnp.bfloat16)
        ys.append(jnp.dot(onehot, staged, preferred_element_type=jnp.float32))
    g0 = route[:, R_G0:R_G0 + 1]
    g1 = route[:, R_G1:R_G1 + 1]
    h = h_ref[...] + (g0 * ys[0] + g1 * ys[1])
    o_ref[...] = h * _rms_scale(h) * fin_ref[...]


def _combine(run_start, run_cnt, h, route, final_norm, ys):
    T, D = h.shape
    tc = MERGE_ROWS
    return pl.pallas_call(
        _combine_kernel,
        grid_spec=pltpu.PrefetchScalarGridSpec(
            num_scalar_prefetch=2,
            grid=(T // tc,),
            in_specs=[pl.BlockSpec((tc, D), lambda i, *_: (i, 0)),
                      pl.BlockSpec((tc, LANES), lambda i, *_: (i, 0)),
                      pl.BlockSpec((1, D), lambda i, *_: (0, 0)),
                      pl.BlockSpec(memory_space=pl.ANY)],
            out_specs=pl.BlockSpec((tc, D), lambda i, *_: (i, 0)),
            scratch_shapes=[pltpu.VMEM((2, STAGE_ROWS, D), jnp.float32),
                            pltpu.SemaphoreType.DMA((2,))],
        ),
        out_shape=jax.ShapeDtypeStruct((T, D), jnp.float32),
        compiler_params=pltpu.CompilerParams(
            dimension_semantics=("arbitrary",), vmem_limit_bytes=VMEM_LIMIT),
        name="moe_combine_final_norm",
    )(run_start, run_cnt, h, route, final_norm.reshape(1, D), ys)


def _routing_tables(tinfo, counts_row, n_tokens):
    counts = counts_row[0, :N_EXPERTS].astype(jnp.int32)
    padded = ((counts + MOE_ROWS - 1) // MOE_ROWS) * MOE_ROWS
    ends = jnp.cumsum(padded)
    starts = ends - padded
    run_start = starts[None, :] + tinfo[:, T_BASE, :N_EXPERTS].astype(jnp.int32)
    run_cnt = tinfo[:, T_COUNT, :N_EXPERTS].astype(jnp.int32)
    run_padding = (n_tokens // MERGE_ROWS) * N_EXPERTS * (SUBLANES - 1)
    n_tiles = (TOP_K * n_tokens + run_padding + MOE_ROWS - 1) // MOE_ROWS + N_EXPERTS
    tile_start = jnp.arange(n_tiles, dtype=jnp.int32) * MOE_ROWS
    tile_e = jnp.minimum(jnp.sum(ends[None, :] <= tile_start[:, None], axis=1),
                         N_EXPERTS - 1).astype(jnp.int32)
    used = (ends[-1:] // MOE_ROWS).astype(jnp.int32)
    tile_valid = jnp.clip(counts[tile_e] - (tile_start - starts[tile_e]), 0, MOE_ROWS).astype(jnp.int32)
    pad_start = (starts + counts).astype(jnp.int32)
    pad_len = (padded - counts).astype(jnp.int32)
    return (run_start.reshape(-1), run_cnt.reshape(-1), tile_e, used, tile_valid, pad_start,
            pad_len, n_tiles * MOE_ROWS)


def kernel(x, a_norm, a_proj, a_scale, kv_norm, w_kv, b_norm, w_q, w_o, rel_bias, ffn_norm,
           dense_w_gate, dense_w_up, dense_w_down, moe_router, moe_w_gate, moe_w_up,
           moe_w_down, final_norm):
    B, S, D = x.shape
    T = B * S
    E, _, F = moe_w_gate[0].shape
    h2, (wg16, wq16, wkv16, wo16) = _layer0(
        x.reshape(T, D), a_norm[0], a_proj[0], a_scale[0], ffn_norm[0],
        dense_w_gate[0], dense_w_up[0], dense_w_down[0],
        (moe_w_gate[0].reshape(E * D, F), w_q[0], w_kv, w_o[0]))
    qkv, wu16 = _qkv_project(h2, kv_norm, b_norm[0], wq16, wkv16, B, S,
                             moe_w_up[0].reshape(E * D, F))
    outs, stats = [], []
    for g, (q, k, v) in enumerate(qkv):
        o, st = _attention_group(q, k, v, rel_bias, g, B, S)
        outs.append(o)
        stats.append(st)
    h3, hn3, route, route_t, tinfo, counts = _merge_layer(outs, stats, h2, wo16, ffn_norm[1],
                                                          moe_router[0])
    (run_start, run_cnt, tile_e, used, tile_valid, pad_start, pad_len,
     n_slots) = _routing_tables(tinfo, counts, T)
    xs, wd16 = _dispatch(run_start, run_cnt, pad_start, pad_len, used, hn3, route_t, n_slots,
                         moe_w_down[0].reshape(E * F, D))
    ys = _expert_ffn(tile_e, used, tile_valid, xs, wg16.reshape(E, D, F), wu16.reshape(E, D, F),
                     wd16.reshape(E, F, D))
    out = _combine(run_start, run_cnt, h3, route, final_norm, ys)
    return out.reshape(B, S, D)
```

```python
import functools
import math

import jax
import jax.numpy as jnp
from jax import lax
from jax.experimental import pallas as pl
from jax.experimental.pallas import tpu as pltpu

D_MODEL = 1024
EPS = 1e-6
POOL_WINDOWS = (2, 4, 8, 16)
POOL_GROUP_DIM = D_MODEL // len(POOL_WINDOWS)
MAX_POOL_WINDOW = max(POOL_WINDOWS)
HEAD_DIM = 64
HEADS_PER_GROUP = D_MODEL // HEAD_DIM
DILATED_PATTERNS = ((128, 1), (512, 4), (2048, 16))
N_ATT_GROUPS = len(DILATED_PATTERNS)
QBLK = 128
NEG_INF = -1e30
LOG2E = 1.4426950408889634
Q_SCALE = HEAD_DIM ** -0.5 * LOG2E
N_BUCKETS = 32
MAX_EXACT = N_BUCKETS // 2
MAX_DISTANCE = 2048
N_EXPERTS = 8
TOP_K = 2
SEQ_LEN = 2048

LANES = 128
SUBLANES = 8
N_SLABS = D_MODEL // LANES
HEAD_PAIR = 2 * HEAD_DIM
assert HEAD_PAIR == LANES

LAYER0_ROWS = 512
QKV_ROWS = 512
ATTN_BLOCKS_PER_STEP = 8
MERGE_ROWS = 512
MOE_ROWS = 1024
MOE_FT = 1792
MOE_CHUNK = 512
MOE_SUBROWS = 256
MOE_NT = 1024
STAGE_ROWS = 1152
VMEM_LIMIT = 60 * 1024 * 1024

R_E0, R_E1, R_RANK0, R_RANK1, R_G0, R_G1 = 0, 1, 2, 3, 4, 5
T_BASE, T_COUNT = 0, 1


def _rms_scale(x):
    return lax.rsqrt(jnp.mean(x * x, axis=-1, keepdims=True) + EPS)


def _silu(g):
    return g * (1.0 / (1.0 + jnp.exp(-g)))


def _split_bf16(x):
    hi = x.astype(jnp.bfloat16)
    lo = (x - hi.astype(jnp.float32)).astype(jnp.bfloat16)
    return hi, lo


def _side_cast_specs(side, n_steps):
    rows = side.shape[0] // n_steps
    spec = pl.BlockSpec((rows, side.shape[1]), lambda i, *_: (i, 0))
    return spec, jax.ShapeDtypeStruct(side.shape, jnp.bfloat16)


def _pool_mix(x, halo, seq_tile, gain, proj_ref, scale):
    xn = x * _rms_scale(x) * gain
    hn_halo = halo * _rms_scale(halo) * gain
    hn_halo = jnp.where(seq_tile > 0, hn_halo, 0.0)
    full = jnp.concatenate([hn_halo, xn], axis=0)
    ts = x.shape[0]
    pos = seq_tile * ts + lax.broadcasted_iota(jnp.int32, (ts, 1), 0)
    outs = []
    for g, w in enumerate(POOL_WINDOWS):
        c0 = g * POOL_GROUP_DIM
        s = full[:, c0:c0 + POOL_GROUP_DIM]
        span = 1
        while span < w:
            s = s + pltpu.roll(s, span, axis=0)
            span *= 2
        s = s[MAX_POOL_WINDOW:, :]
        cnt = jnp.minimum(pos + 1, w).astype(jnp.float32)
        pooled = s / cnt - xn[:, c0:c0 + POOL_GROUP_DIM]
        outs.append(jnp.dot(pooled.astype(jnp.bfloat16), proj_ref[g],
                            preferred_element_type=jnp.float32))
    return x + jnp.concatenate(outs, axis=1) * scale


def _layer0_kernel(*refs, tiles_per_seq, n_side):
    (x_ref, halo_ref, an_ref, proj_ref, asc_ref, fn_ref, wg_ref, wu_ref, wd_ref) = refs[:9]
    side_in = refs[9:9 + n_side]
    h_out_ref = refs[9 + n_side]
    side_out = refs[10 + n_side:]
    for src, dst in zip(side_in, side_out):
        dst[...] = src[...].astype(dst.dtype)
    seq_tile = lax.rem(pl.program_id(0), tiles_per_seq)
    h = _pool_mix(x_ref[...], halo_ref[...], seq_tile, an_ref[...], proj_ref, asc_ref[...])
    hn = (h * _rms_scale(h) * fn_ref[...]).astype(jnp.bfloat16)
    g = jnp.dot(hn, wg_ref[...], preferred_element_type=jnp.float32)
    u = jnp.dot(hn, wu_ref[...], preferred_element_type=jnp.float32)
    hid = (_silu(g) * u).astype(jnp.bfloat16)
    h_out_ref[...] = h + jnp.dot(hid, wd_ref[...], preferred_element_type=jnp.float32)


def _layer0(x, a_norm, a_proj, a_scale, ffn_norm, w_gate, w_up, w_down, sides):
    T, D = x.shape
    F = w_gate.shape[1]
    ts = LAYER0_ROWS
    n_steps = T // ts
    halo_blocks = ts // MAX_POOL_WINDOW
    row = lambda i: (i, 0)
    vec = pl.BlockSpec((1, D), lambda i: (0, 0))
    resident = dict(index_map=lambda i: (0, 0), pipeline_mode=pl.Buffered(1))
    side_specs, side_shapes = zip(*[_side_cast_specs(a, n_steps) for a in sides])
    outs = pl.pallas_call(
        functools.partial(_layer0_kernel, tiles_per_seq=SEQ_LEN // ts, n_side=len(sides)),
        grid=(n_steps,),
        in_specs=[
            pl.BlockSpec((ts, D), row),
            pl.BlockSpec((MAX_POOL_WINDOW, D), lambda i: (jnp.maximum(i * halo_blocks - 1, 0), 0)),
            vec,
            pl.BlockSpec(a_proj.shape, lambda i: (0, 0, 0), pipeline_mode=pl.Buffered(1)),
            vec, vec,
            pl.BlockSpec((D, F), **resident),
            pl.BlockSpec((D, F), **resident),
            pl.BlockSpec((F, D), **resident),
            *side_specs,
        ],
        out_specs=[pl.BlockSpec((ts, D), row), *side_specs],
        out_shape=[jax.ShapeDtypeStruct((T, D), jnp.float32), *side_shapes],
        compiler_params=pltpu.CompilerParams(
            dimension_semantics=("arbitrary",), vmem_limit_bytes=VMEM_LIMIT),
        name="pool_mixer_dense_swiglu",
    )(x, x, a_norm.reshape(1, D), a_proj.astype(jnp.bfloat16), a_scale.reshape(1, D),
      ffn_norm.reshape(1, D), w_gate.astype(jnp.bfloat16), w_up.astype(jnp.bfloat16),
      w_down.astype(jnp.bfloat16), *sides)
    return outs[0], outs[1:]


def _qkv_kernel(*refs):
    slabs = refs[:N_SLABS]
    kvg_ref, qg_ref, wq_ref, wkv_ref, side_in_ref = refs[N_SLABS:N_SLABS + 5]
    out_refs = refs[N_SLABS + 5:-1]
    side_out_ref = refs[-1]
    side_out_ref[...] = side_in_ref[...].astype(side_out_ref.dtype)
    tm = slabs[0].shape[0]
    for g, (_, dilation) in enumerate(DILATED_PATTERNS):
        n = tm // dilation
        cols = []
        for slab in slabs:
            if dilation == 1:
                cols.append(slab[...])
            else:
                cols.append(jnp.concatenate(
                    [slab[pl.ds(r, n, stride=dilation), :] for r in range(dilation)], axis=0))
        x = jnp.concatenate(cols, axis=1)
        xs = x * _rms_scale(x)
        xq = (xs * qg_ref[...]).astype(jnp.bfloat16)
        xkv = (xs * kvg_ref[...]).astype(jnp.bfloat16)
        D = D_MODEL
        kcols = slice(g * D, (g + 1) * D)
        vcols = slice((N_ATT_GROUPS + g) * D, (N_ATT_GROUPS + g + 1) * D)
        q = jnp.dot(xq, wq_ref[:, kcols], preferred_element_type=jnp.float32) * Q_SCALE
        k = jnp.dot(xkv, wkv_ref[:, kcols], preferred_element_type=jnp.float32)
        v = jnp.dot(xkv, wkv_ref[:, vcols], preferred_element_type=jnp.float32)
        for out_ref, val in zip(out_refs[3 * g:3 * g + 3], (q, k, v)):
            for r in range(dilation):
                out_ref[r] = val[r * n:(r + 1) * n].astype(out_ref.dtype)


def _qkv_project(h, kv_norm, b_norm, w_q, w_kv, B, S, side):
    T, D = h.shape
    tm = QKV_ROWS
    tiles_per_seq = S // tm
    resident = dict(index_map=lambda i: (0, 0), pipeline_mode=pl.Buffered(1))
    slab_specs = [pl.BlockSpec((tm, LANES), functools.partial(lambda i, c: (i, c), c=c))
                  for c in range(N_SLABS)]
    vec = pl.BlockSpec((1, D), lambda i: (0, 0))
    out_specs, out_shapes = [], []
    for _, d in DILATED_PATTERNS:
        spec = pl.BlockSpec((None, d, tm // d, D),
                            lambda i: (i // tiles_per_seq, 0, i % tiles_per_seq, 0))
        out_specs += [spec] * 3
        out_shapes += [jax.ShapeDtypeStruct((B, d, S // d, D), jnp.bfloat16)] * 3
    side_spec, side_shape = _side_cast_specs(side, T // tm)
    outs = pl.pallas_call(
        _qkv_kernel,
        grid=(T // tm,),
        in_specs=slab_specs + [vec, vec, pl.BlockSpec(w_q.shape, **resident),
                               pl.BlockSpec(w_kv.shape, **resident), side_spec],
        out_specs=out_specs + [side_spec],
        out_shape=out_shapes + [side_shape],
        compiler_params=pltpu.CompilerParams(
            dimension_semantics=("arbitrary",), vmem_limit_bytes=VMEM_LIMIT),
        name="qkv_proj",
    )(*([h] * N_SLABS), kv_norm.reshape(1, D), b_norm.reshape(1, D), w_q, w_kv, side)
    return [tuple(outs[3 * g:3 * g + 3]) for g in range(N_ATT_GROUPS)], outs[-1]


def _attn_kernel(*refs, dilation, has_prev, jb, rb):
    if has_prev:
        (table_ref, bmap_ref, q_ref, kp_ref, kc_ref, vp_ref, vc_ref,
         o_ref, stat_ref, bias_ref, stat_scr) = refs
    else:
        table_ref, bmap_ref, q_ref, kc_ref, vc_ref, o_ref, stat_ref, bias_ref, stat_scr = refs
    b, jstep, rstep = pl.program_id(0), pl.program_id(1), pl.program_id(2)
    n_keys = bias_ref.shape[2]

    @pl.when(jnp.logical_and(jnp.logical_and(b == 0, jstep == 0), rstep == 0))
    def _():
        stat_scr[...] = jnp.zeros_like(stat_scr)
        bmap = bmap_ref[...]
        in_prev = lax.broadcasted_iota(jnp.int32, bmap.shape, 1) < n_keys - QBLK

        def head(h, carry):
            acc = jnp.full(bmap.shape, NEG_INF, jnp.float32)
            for bucket in range(N_BUCKETS):
                acc = jnp.where(bmap == bucket, table_ref[bucket, h], acc)
            rows_h = pl.ds(pl.multiple_of(h * QBLK, QBLK), QBLK)
            bias_ref[0, rows_h, :] = jnp.where(in_prev, NEG_INF, acc)
            bias_ref[1, rows_h, :] = acc
            return carry

        lax.fori_loop(0, HEADS_PER_GROUP, head, 0)

    lane = lax.broadcasted_iota(jnp.int32, (QBLK, LANES), 1)
    first_head = lane < HEAD_DIM
    for rr in range(rb):
        for jj in range(jb):
            qrows = slice(jj * QBLK, (jj + 1) * QBLK)
            bias_copy = jnp.minimum(jstep, 1) if jj == 0 else 1
            if dilation > 1:
                out_rows = pl.ds(dilation * QBLK * jj + rstep * rb + rr, QBLK, stride=dilation)
            else:
                out_rows = qrows
            for hp in range(HEADS_PER_GROUP // 2):
                sl = slice(hp * HEAD_PAIR, (hp + 1) * HEAD_PAIR)
                qp = q_ref[rr, qrows, sl]
                zero = jnp.zeros_like(qp)
                q2 = jnp.concatenate([jnp.where(first_head, qp, zero),
                                      jnp.where(first_head, zero, qp)], axis=0)
                if has_prev:
                    if jj == 0:
                        k_prev, v_prev = kp_ref[rr, :, sl], vp_ref[rr, :, sl]
                    else:
                        prows = slice((jj - 1) * QBLK, jj * QBLK)
                        k_prev, v_prev = kc_ref[rr, prows, sl], vc_ref[rr, prows, sl]
                    kcat = jnp.concatenate([k_prev, kc_ref[rr, qrows, sl]], axis=0)
                    vcat = jnp.concatenate([v_prev, vc_ref[rr, qrows, sl]], axis=0)
                else:
                    kcat, vcat = kc_ref[rr, qrows, sl], vc_ref[rr, qrows, sl]
                s = lax.dot_general(q2, kcat, (((1,), (1,)), ((), ())),
                                    preferred_element_type=jnp.float32)
                s = s + bias_ref[bias_copy, hp * 2 * QBLK:(hp + 1) * 2 * QBLK, :]
                m = jnp.max(s, axis=1, keepdims=True)
                p = jnp.exp2(s - m)
                l = jnp.sum(p, axis=1, keepdims=True)
                o2 = jnp.dot(p.astype(jnp.bfloat16), vcat, preferred_element_type=jnp.float32)
                o_ref[hp, out_rows, :] = jnp.where(first_head, o2[:QBLK], o2[QBLK:])
                for half, head in ((slice(0, QBLK), 2 * hp), (slice(QBLK, 2 * QBLK), 2 * hp + 1)):
                    stat_scr[:, head:head + 1] = m[half]
                    stat_scr[:, HEADS_PER_GROUP + head:HEADS_PER_GROUP + head + 1] = l[half]
            stat_ref[out_rows, :] = stat_scr[...]


def _bucket_map(dilation, window, n_keys):
    W = window // dilation
    a = jnp.arange(QBLK, dtype=jnp.int32)[:, None]
    c = jnp.arange(n_keys, dtype=jnp.int32)[None, :]
    m = a + (n_keys - QBLK) - c
    band = (m >= 0) & (m <= W)
    n = jnp.maximum(m, 0) * dilation
    nf = jnp.maximum(n, 1).astype(jnp.float32)
    large = MAX_EXACT + (jnp.log(nf / MAX_EXACT) / math.log(MAX_DISTANCE / MAX_EXACT)
                         * (N_BUCKETS - MAX_EXACT)).astype(jnp.int32)
    large = jnp.minimum(large, N_BUCKETS - 1)
    bucket = jnp.where(n < MAX_EXACT, n, large)
    return jnp.where(band, bucket, -1)


def _attention_group(q, k, v, rel_bias, g, B, S):
    window, d = DILATED_PATTERNS[g]
    L = S // d
    n = L // QBLK
    D = D_MODEL
    has_prev = n > 1
    n_keys = 2 * QBLK if has_prev else QBLK
    table = rel_bias[:, g * HEADS_PER_GROUP:(g + 1) * HEADS_PER_GROUP].astype(jnp.float32) * LOG2E
    bmap = _bucket_map(d, window, n_keys)
    jb = min(ATTN_BLOCKS_PER_STEP, n)
    rb = ATTN_BLOCKS_PER_STEP // jb
    cur_blk = (None, rb, jb * QBLK, D)
    prev_blk = (None, rb, QBLK, D)
    cur = lambda b, j, r: (b, r, j, 0)
    prev = lambda b, j, r: (b, r, jnp.maximum(j * jb - 1, 0), 0)
    if has_prev:
        kv_specs = [pl.BlockSpec(prev_blk, prev), pl.BlockSpec(cur_blk, cur),
                    pl.BlockSpec(prev_blk, prev), pl.BlockSpec(cur_blk, cur)]
        kv_args = (k, k, v, v)
    else:
        kv_specs = [pl.BlockSpec(cur_blk, cur), pl.BlockSpec(cur_blk, cur)]
        kv_args = (k, v)
    n_j = n // jb
    span = QBLK * jb * d
    return pl.pallas_call(
        functools.partial(_attn_kernel, dilation=d, has_prev=has_prev, jb=jb, rb=rb),
        grid=(B, n_j, d // rb),
        in_specs=[pl.BlockSpec(memory_space=pltpu.SMEM),
                  pl.BlockSpec(bmap.shape, lambda b, j, r: (0, 0)),
                  pl.BlockSpec(cur_blk, cur)] + kv_specs,
        out_specs=[pl.BlockSpec((N_SLABS, span, LANES), lambda b, j, r: (0, b * n_j + j, 0)),
                   pl.BlockSpec((span, LANES), lambda b, j, r: (b * n_j + j, 0))],
        out_shape=[jax.ShapeDtypeStruct((N_SLABS, B * S, LANES), jnp.float32),
                   jax.ShapeDtypeStruct((B * S, LANES), jnp.float32)],
        scratch_shapes=[pltpu.VMEM((2, HEADS_PER_GROUP * QBLK, n_keys), jnp.float32),
                        pltpu.VMEM((QBLK, LANES), jnp.float32)],
        compiler_params=pltpu.CompilerParams(
            dimension_semantics=("arbitrary",) * 3, vmem_limit_bytes=VMEM_LIMIT),
        name=f"dilated_attn_g{g}",
    )(table, bmap, q, *kv_args)


def _merge_kernel(o0_ref, o1_ref, o2_ref, s0_ref, s1_ref, s2_ref, h_ref, wo_ref, fn_ref,
                  r2_ref, expand_ref,
                  h_out_ref, hn_ref, route_ref, route_t_ref, tinfo_ref, counts_ref, carry_ref):
    i = pl.program_id(0)
    tm = h_ref.shape[0]

    @pl.when(i == 0)
    def _():
        carry_ref[...] = jnp.zeros_like(carry_ref)

    lane = lax.broadcasted_iota(jnp.int32, (tm, LANES), 1)
    head_lane = lane < HEADS_PER_GROUP
    stats = [s0_ref[...], s1_ref[...], s2_ref[...]]
    dens = [pltpu.roll(st, LANES - HEADS_PER_GROUP, axis=1) for st in stats]
    mx = jnp.maximum(jnp.maximum(stats[0], stats[1]), stats[2])
    es = [jnp.exp2(st - mx) for st in stats]
    inv = 1.0 / (es[0] * dens[0] + es[1] * dens[1] + es[2] * dens[2])
    merged = None
    for e, o_ref in zip(es, (o0_ref, o1_ref, o2_ref)):
        w = jnp.where(head_lane, e * inv, 0.0)
        hi = w.astype(jnp.bfloat16).astype(jnp.float32)
        packed = (hi + pltpu.roll(w - hi, HEADS_PER_GROUP, axis=1)).astype(jnp.bfloat16)
        wide = jnp.dot(packed, expand_ref[...], preferred_element_type=jnp.float32)
        o = jnp.concatenate([o_ref[c] for c in range(N_SLABS)], axis=1)
        term = wide * o
        merged = term if merged is None else merged + term
    h = h_ref[...] + jnp.dot(merged.astype(jnp.bfloat16), wo_ref[...],
                             preferred_element_type=jnp.float32)
    h_out_ref[...] = h
    hn = h * _rms_scale(h) * fn_ref[...]
    hn_ref[...] = hn.astype(hn_ref.dtype)

    hi, lo = _split_bf16(hn)
    both = jnp.dot(hi, r2_ref[...], preferred_element_type=jnp.float32)
    logits = (both[:, :LANES] + both[:, LANES:]
              + jnp.dot(lo, r2_ref[:, :LANES], preferred_element_type=jnp.float32))
    logits = jnp.where(lane < N_EXPERTS, logits, -jnp.inf)
    v0 = jnp.max(logits, axis=1, keepdims=True)
    e0 = jnp.min(jnp.where(logits == v0, lane, LANES), axis=1, keepdims=True)
    rest = jnp.where(lane == e0, -jnp.inf, logits)
    v1 = jnp.max(rest, axis=1, keepdims=True)
    e1 = jnp.min(jnp.where(rest == v1, lane, LANES), axis=1, keepdims=True)
    t = jnp.exp(v1 - v0)
    g0 = 1.0 / (1.0 + t)
    g1 = t / (1.0 + t)

    hit0 = lane == e0
    hit1 = lane == e1
    onehot = jnp.where(jnp.logical_or(hit0, hit1), 1.0, 0.0)
    r_i = lax.broadcasted_iota(jnp.int32, (tm, tm), 0)
    c_i = lax.broadcasted_iota(jnp.int32, (tm, tm), 1)
    lower = jnp.where(c_i < r_i, 1.0, 0.0).astype(jnp.bfloat16)
    before = jnp.dot(lower, onehot.astype(jnp.bfloat16), preferred_element_type=jnp.float32)
    rank0 = jnp.sum(jnp.where(hit0, before, 0.0), axis=1, keepdims=True)
    rank1 = jnp.sum(jnp.where(hit1, before, 0.0), axis=1, keepdims=True)
    tile_counts = jnp.sum(onehot, axis=0, keepdims=True)
    base = carry_ref[0:1, :]
    total = base + jnp.floor((tile_counts + (SUBLANES - 1)) * (1.0 / SUBLANES)) * SUBLANES
    carry_ref[...] = jnp.broadcast_to(total, carry_ref.shape)
    counts_ref[...] = jnp.broadcast_to(total, counts_ref.shape)
    row_id = lax.broadcasted_iota(jnp.int32, tinfo_ref.shape, 0)
    tinfo_ref[...] = jnp.where(row_id == T_BASE, base, jnp.where(row_id == T_COUNT, tile_counts, 0.0))

    route = jnp.zeros((tm, LANES), jnp.float32)
    for ln, val in ((R_E0, e0.astype(jnp.float32)), (R_E1, e1.astype(jnp.float32)),
                    (R_RANK0, rank0), (R_RANK1, rank1), (R_G0, g0), (R_G1, g1)):
        route = jnp.where(lane == ln, val, route)
    route_ref[...] = route
    route_t_ref[...] = route.T[:SUBLANES, :]


def _merge_layer(outs, lses, h, w_o, ffn_norm, w_router):
    T, D = h.shape
    tm = MERGE_ROWS
    r_pad = jnp.zeros((D, LANES), jnp.float32).at[:, :N_EXPERTS].set(w_router)
    r2 = jnp.concatenate(_split_bf16(r_pad), axis=1)
    head_of_col = jnp.arange(D, dtype=jnp.int32) // HEAD_DIM
    lane_id = jnp.arange(LANES, dtype=jnp.int32)[:, None]
    expand = ((lane_id < 2 * HEADS_PER_GROUP)
              & (lane_id % HEADS_PER_GROUP == head_of_col[None, :])).astype(jnp.bfloat16)
    row = lambda i: (i, 0)
    const = lambda i: (0, 0)
    wide = pl.BlockSpec((tm, D), row)
    narrow = pl.BlockSpec((tm, LANES), row)
    slabs = pl.BlockSpec((N_SLABS, tm, LANES), lambda i: (0, i, 0))
    return pl.pallas_call(
        _merge_kernel,
        grid=(T // tm,),
        in_specs=[slabs, slabs, slabs, narrow, narrow, narrow, wide,
                  pl.BlockSpec((D, D), const), pl.BlockSpec((1, D), const),
                  pl.BlockSpec((D, 2 * LANES), const), pl.BlockSpec((LANES, D), const)],
        out_specs=[wide, wide, narrow,
                   pl.BlockSpec((None, SUBLANES, tm), lambda i: (i, 0, 0)),
                   pl.BlockSpec((None, SUBLANES, LANES), lambda i: (i, 0, 0)),
                   pl.BlockSpec((8, LANES), const)],
        out_shape=[jax.ShapeDtypeStruct((T, D), jnp.float32),
                   jax.ShapeDtypeStruct((T, D), jnp.bfloat16),
                   jax.ShapeDtypeStruct((T, LANES), jnp.float32),
                   jax.ShapeDtypeStruct((T // tm, SUBLANES, tm), jnp.float32),
                   jax.ShapeDtypeStruct((T // tm, SUBLANES, LANES), jnp.float32),
                   jax.ShapeDtypeStruct((8, LANES), jnp.float32)],
        scratch_shapes=[pltpu.VMEM((8, LANES), jnp.float32)],
        compiler_params=pltpu.CompilerParams(
            dimension_semantics=("arbitrary",), vmem_limit_bytes=VMEM_LIMIT),
        name="merge_outproj_router",
    )(*outs, *lses, h, w_o, ffn_norm.reshape(1, D), r2, expand)


def _run_offsets(cnt_ref, tile):
    sizes, offs, off = [], [], 0
    for e in range(N_EXPERTS):
        size = ((cnt_ref[tile * N_EXPERTS + e] + (SUBLANES - 1)) // SUBLANES) * SUBLANES
        sizes.append(size)
        offs.append(off)
        off = off + size
    return sizes, offs


def _for_each_run_piece(start_ref, cnt_ref, tile, fn):
    sizes, offs = _run_offsets(cnt_ref, tile)
    for e in range(N_EXPERTS):
        slot0 = start_ref[tile * N_EXPERTS + e]
        bit = MERGE_ROWS
        while bit >= SUBLANES:
            done = sizes[e] & -(2 * bit)

            @pl.when((sizes[e] & bit) != 0)
            def _():
                fn(pl.multiple_of(offs[e] + done, SUBLANES),
                   pl.multiple_of(slot0 + done, SUBLANES), bit)
            bit //= 2


def _staging_columns(expert, rank, offs):
    col = rank
    for e in range(N_EXPERTS):
        col = col + jnp.where(expert == e, offs[e], 0)
    return col


def _dispatch_kernel(start_ref, cnt_ref, pad_start_ref, pad_len_ref, used_ref,
                     hn_ref, route_t_ref, side_in_ref, xs_ref, side_out_ref,
                     xbuf, zbuf, sems, zsem):
    side_out_ref[...] = side_in_ref[...].astype(side_out_ref.dtype)
    i = pl.program_id(0)
    n_steps = pl.num_programs(0)
    slot = lax.rem(i, 2)
    n_tiles = xs_ref.shape[0] // MOE_ROWS

    def zero_fill(act):
        for e in range(N_EXPERTS):
            run_start = pl.multiple_of(pad_start_ref[e], SUBLANES)
            run_len = pad_len_ref[e]
            bit = MOE_ROWS // 2
            while bit >= SUBLANES:
                off = pl.multiple_of(run_start + (run_len & -(2 * bit)), SUBLANES)

                @pl.when((run_len & bit) != 0)
                def _():
                    act(pltpu.make_async_copy(zbuf.at[pl.ds(0, bit), :],
                                              xs_ref.at[pl.ds(off, bit), :], zsem))
                bit //= 2
        for t in range(n_tiles):
            @pl.when(t >= used_ref[0])
            def _():
                act(pltpu.make_async_copy(zbuf, xs_ref.at[pl.ds(t * MOE_ROWS, MOE_ROWS), :], zsem))

    def run_copies(tile, buf_slot, act):
        def piece(buf_row, slot_row, rows):
            act(pltpu.make_async_copy(xbuf.at[buf_slot, pl.ds(buf_row, rows), :],
                                      xs_ref.at[pl.ds(slot_row, rows), :], sems.at[buf_slot]))
        _for_each_run_piece(start_ref, cnt_ref, tile, piece)

    @pl.when(i == 0)
    def _():
        zbuf[...] = jnp.zeros_like(zbuf)
        zero_fill(lambda c: c.start())

    @pl.when(i >= 2)
    def _():
        run_copies(i - 2, slot, lambda c: c.wait())

    _, offs = _run_offsets(cnt_ref, i)
    route_t = route_t_ref[...]
    cols = [_staging_columns(route_t[R_E0 + k:R_E0 + k + 1, :].astype(jnp.int32),
                             route_t[R_RANK0 + k:R_RANK0 + k + 1, :].astype(jnp.int32), offs)
            for k in range(TOP_K)]
    buf_row = lax.broadcasted_iota(jnp.int32, (STAGE_ROWS, hn_ref.shape[0]), 0)
    onehot = jnp.where(jnp.logical_or(buf_row == cols[0], buf_row == cols[1]), 1.0, 0.0)
    xbuf[slot] = jnp.dot(onehot.astype(jnp.bfloat16), hn_ref[...],
                         preferred_element_type=jnp.float32)
    run_copies(i, slot, lambda c: c.start())

    @pl.when(i == n_steps - 1)
    def _():
        @pl.when(i >= 1)
        def _():
            run_copies(i - 1, 1 - slot, lambda c: c.wait())
        run_copies(i, slot, lambda c: c.wait())
        zero_fill(lambda c: c.wait())


def _dispatch(run_start, run_cnt, pad_start, pad_len, used, hn, route_t, n_slots, side):
    T, D = hn.shape
    tm = MERGE_ROWS
    side_spec, side_shape = _side_cast_specs(side, T // tm)
    return pl.pallas_call(
        _dispatch_kernel,
        grid_spec=pltpu.PrefetchScalarGridSpec(
            num_scalar_prefetch=5,
            grid=(T // tm,),
            in_specs=[pl.BlockSpec((tm, D), lambda i, *_: (i, 0)),
                      pl.BlockSpec((None, SUBLANES, tm), lambda i, *_: (i, 0, 0)),
                      side_spec],
            out_specs=[pl.BlockSpec(memory_space=pl.ANY), side_spec],
            scratch_shapes=[pltpu.VMEM((2, STAGE_ROWS, D), jnp.float32),
                            pltpu.VMEM((MOE_ROWS, D), jnp.float32),
                            pltpu.SemaphoreType.DMA((2,)), pltpu.SemaphoreType.DMA(())],
        ),
        out_shape=[jax.ShapeDtypeStruct((n_slots, D), jnp.float32), side_shape],
        compiler_params=pltpu.CompilerParams(
            dimension_semantics=("arbitrary",), vmem_limit_bytes=VMEM_LIMIT),
        name="moe_dispatch",
    )(run_start, run_cnt, pad_start, pad_len, used, hn, route_t, side)


def _expert_kernel(tile_e_ref, used_ref, valid_ref, x_ref, wg_ref, wu_ref, wd_ref, y_ref,
                   xb_ref, hid_ref):
    del tile_e_ref
    i = pl.program_id(0)
    s = pl.program_id(1)
    n_f, tm, tf = hid_ref.shape
    active = i < used_ref[0]
    n_valid = valid_ref[jnp.minimum(i, used_ref[0] - 1)]
    full = n_valid == tm
    gate_step = jnp.logical_and(active, s < n_f)
    down_step = jnp.logical_and(active, s >= n_f)

    @pl.when(jnp.logical_and(active, s == 0))
    def _():
        xb_ref[...] = x_ref[...].astype(xb_ref.dtype)

    def gate_up(rows):
        x = xb_ref[rows, :]
        for c0 in range(0, tf, MOE_CHUNK):
            cols = slice(c0, min(c0 + MOE_CHUNK, tf))
            g = jnp.dot(x, wg_ref[:, cols], preferred_element_type=jnp.float32)
            u = jnp.dot(x, wu_ref[:, cols], preferred_element_type=jnp.float32)
            hid_ref[s, rows, cols] = (_silu(g) * u).astype(hid_ref.dtype)

    def down(rows):
        hid = jnp.concatenate([hid_ref[c, rows, :] for c in range(n_f)], axis=1)
        y_ref[rows, :] = jnp.dot(hid, wd_ref[...], preferred_element_type=jnp.float32)

    @pl.when(jnp.logical_and(gate_step, full))
    def _():
        gate_up(slice(None))

    @pl.when(jnp.logical_and(down_step, full))
    def _():
        down(slice(None))

    for r0 in range(0, tm, MOE_SUBROWS):
        rows = slice(r0, r0 + MOE_SUBROWS)
        occupied = jnp.logical_and(jnp.logical_not(full), r0 < n_valid)

        @pl.when(jnp.logical_and(gate_step, occupied))
        def _():
            gate_up(rows)

        @pl.when(jnp.logical_and(down_step, occupied))
        def _():
            down(rows)

        @pl.when(jnp.logical_and(down_step, r0 >= n_valid))
        def _():
            y_ref[rows, :] = jnp.zeros((MOE_SUBROWS, y_ref.shape[1]), y_ref.dtype)

    @pl.when(jnp.logical_and(jnp.logical_not(active), s >= n_f))
    def _():
        y_ref[...] = jnp.zeros_like(y_ref)


def _expert_ffn(tile_e, used, tile_valid, xs, w_gate, w_up, w_down):
    n_slots, D = xs.shape
    F = w_gate.shape[2]
    tm, tf, tn = MOE_ROWS, MOE_FT, MOE_NT
    n_f, n_n = F // tf, D // tn
    n_tiles = n_slots // tm
    last = n_f + n_n - 1

    def eff(i, s, used_ref):
        idle = i >= used_ref[0]
        return jnp.where(idle, used_ref[0] - 1, i), jnp.where(idle, last, s)

    def x_map(i, s, te, us, nv):
        ie, se = eff(i, s, us)
        return jnp.minimum(ie + jnp.minimum(se, 1), us[0] - 1), 0

    def gate_map(i, s, te, us, nv):
        ie, se = eff(i, s, us)
        return te[ie], 0, jnp.minimum(se, n_f - 1)

    def down_map(i, s, te, us, nv):
        ie, se = eff(i, s, us)
        tile = jnp.where(se == 0, jnp.maximum(ie - 1, 0), ie)
        return te[tile], 0, jnp.where(se == 0, n_n - 1, jnp.clip(se - n_f, 0, n_n - 1))

    def out_map(i, s, te, us, nv):
        return i, jnp.clip(s - n_f, 0, n_n - 1)

    return pl.pallas_call(
        _expert_kernel,
        grid_spec=pltpu.PrefetchScalarGridSpec(
            num_scalar_prefetch=3,
            grid=(n_tiles, n_f + n_n),
            in_specs=[
                pl.BlockSpec((tm, D), x_map),
                pl.BlockSpec((None, D, tf), gate_map),
                pl.BlockSpec((None, D, tf), gate_map),
                pl.BlockSpec((None, F, tn), down_map),
            ],
            out_specs=pl.BlockSpec((tm, tn), out_map),
            scratch_shapes=[pltpu.VMEM((tm, D), jnp.bfloat16),
                            pltpu.VMEM((n_f, tm, tf), jnp.bfloat16)],
        ),
        out_shape=jax.ShapeDtypeStruct((n_slots, D), jnp.float32),
        compiler_params=pltpu.CompilerParams(
            dimension_semantics=("arbitrary", "arbitrary"), vmem_limit_bytes=VMEM_LIMIT),
        name="moe_experts",
    )(tile_e, used, tile_valid, xs, w_gate, w_up, w_down)


def _combine_kernel(start_ref, cnt_ref, h_ref, route_ref, fin_ref, ys_ref, o_ref, ybuf, sems):
    i = pl.program_id(0)
    n_steps = pl.num_programs(0)
    tile = h_ref.shape[0]
    slot = lax.rem(i, 2)

    def run_copies(step, buf_slot, act):
        def piece(buf_row, slot_row, rows):
            act(pltpu.make_async_copy(ys_ref.at[pl.ds(slot_row, rows), :],
                                      ybuf.at[buf_slot, pl.ds(buf_row, rows), :],
                                      sems.at[buf_slot]))
        _for_each_run_piece(start_ref, cnt_ref, step, piece)

    @pl.when(i == 0)
    def _():
        ybuf[...] = jnp.zeros_like(ybuf)
        run_copies(0, 0, lambda c: c.start())

    @pl.when(i + 1 < n_steps)
    def _():
        run_copies(i + 1, 1 - slot, lambda c: c.start())

    run_copies(i, slot, lambda c: c.wait())
    _, offs = _run_offsets(cnt_ref, i)
    route = route_ref[...]
    staged = ybuf[slot].astype(jnp.bfloat16)
    buf_col = lax.broadcasted_iota(jnp.int32, (tile, STAGE_ROWS), 1)
    ys = []
    for k in range(TOP_K):
        col = _staging_columns(route[:, R_E0 + k:R_E0 + k + 1].astype(jnp.int32),
                               route[:, R_RANK0 + k:R_RANK0 + k + 1].astype(jnp.int32), offs)
        onehot = jnp.where(buf_col == col, 1.0, 0.0).astype(jnp.bfloat16)
        ys.append(jnp.dot(onehot, staged, preferred_element_type=jnp.float32))
    g0 = route[:, R_G0:R_G0 + 1]
    g1 = route[:, R_G1:R_G1 + 1]
    h = h_ref[...] + (g0 * ys[0] + g1 * ys[1])
    o_ref[...] = h * _rms_scale(h) * fin_ref[...]


def _combine(run_start, run_cnt, h, route, final_norm, ys):
    T, D = h.shape
    tc = MERGE_ROWS
    return pl.pallas_call(
        _combine_kernel,
        grid_spec=pltpu.PrefetchScalarGridSpec(
            num_scalar_prefetch=2,
            grid=(T // tc,),
            in_specs=[pl.BlockSpec((tc, D), lambda i, *_: (i, 0)),
                      pl.BlockSpec((tc, LANES), lambda i, *_: (i, 0)),
                      pl.BlockSpec((1, D), lambda i, *_: (0, 0)),
                      pl.BlockSpec(memory_space=pl.ANY)],
            out_specs=pl.BlockSpec((tc, D), lambda i, *_: (i, 0)),
            scratch_shapes=[pltpu.VMEM((2, STAGE_ROWS, D), jnp.float32),
                            pltpu.SemaphoreType.DMA((2,))],
        ),
        out_shape=jax.ShapeDtypeStruct((T, D), jnp.float32),
        compiler_params=pltpu.CompilerParams(
            dimension_semantics=("arbitrary",), vmem_limit_bytes=VMEM_LIMIT),
        name="moe_combine_final_norm",
    )(run_start, run_cnt, h, route, final_norm.reshape(1, D), ys)


def _routing_tables(tinfo, counts_row, n_tokens):
    counts = counts_row[0, :N_EXPERTS].astype(jnp.int32)
    padded = ((counts + MOE_ROWS - 1) // MOE_ROWS) * MOE_ROWS
    ends = jnp.cumsum(padded)
    starts = ends - padded
    run_start = starts[None, :] + tinfo[:, T_BASE, :N_EXPERTS].astype(jnp.int32)
    run_cnt = tinfo[:, T_COUNT, :N_EXPERTS].astype(jnp.int32)
    run_padding = (n_tokens // MERGE_ROWS) * N_EXPERTS * (SUBLANES - 1)
    n_tiles = (TOP_K * n_tokens + run_padding + MOE_ROWS - 1) // MOE_ROWS + N_EXPERTS
    tile_start = jnp.arange(n_tiles, dtype=jnp.int32) * MOE_ROWS
    tile_e = jnp.minimum(jnp.sum(ends[None, :] <= tile_start[:, None], axis=1),
                         N_EXPERTS - 1).astype(jnp.int32)
    used = (ends[-1:] // MOE_ROWS).astype(jnp.int32)
    tile_valid = jnp.clip(counts[tile_e] - (tile_start - starts[tile_e]), 0, MOE_ROWS).astype(jnp.int32)
    pad_start = (starts + counts).astype(jnp.int32)
    pad_len = (padded - counts).astype(jnp.int32)
    return (run_start.reshape(-1), run_cnt.reshape(-1), tile_e, used, tile_valid, pad_start,
            pad_len, n_tiles * MOE_ROWS)


def kernel(x, a_norm, a_proj, a_scale, kv_norm, w_kv, b_norm, w_q, w_o, rel_bias, ffn_norm,
           dense_w_gate, dense_w_up, dense_w_down, moe_router, moe_w_gate, moe_w_up,
           moe_w_down, final_norm):
    B, S, D = x.shape
    T = B * S
    E, _, F = moe_w_gate[0].shape
    h2, (wg16, wq16, wkv16, wo16) = _layer0(
        x.reshape(T, D), a_norm[0], a_proj[0], a_scale[0], ffn_norm[0],
        dense_w_gate[0], dense_w_up[0], dense_w_down[0],
        (moe_w_gate[0].reshape(E * D, F), w_q[0], w_kv, w_o[0]))
    qkv, wu16 = _qkv_project(h2, kv_norm, b_norm[0], wq16, wkv16, B, S,
                             moe_w_up[0].reshape(E * D, F))
    outs, stats = [], []
    for g, (q, k, v) in enumerate(qkv):
        o, st = _attention_group(q, k, v, rel_bias, g, B, S)
        outs.append(o)
        stats.append(st)
    h3, hn3, route, route_t, tinfo, counts = _merge_layer(outs, stats, h2, wo16, ffn_norm[1],
                                                          moe_router[0])
    (run_start, run_cnt, tile_e, used, tile_valid, pad_start, pad_len,
     n_slots) = _routing_tables(tinfo, counts, T)
    xs, wd16 = _dispatch(run_start, run_cnt, pad_start, pad_len, used, hn3, route_t, n_slots,
                         moe_w_down[0].reshape(E * F, D))
    ys = _expert_ffn(tile_e, used, tile_valid, xs, wg16.reshape(E, D, F), wu16.reshape(E, D, F),
                     wd16.reshape(E, F, D))
    out = _combine(run_start, run_cnt, h3, route, final_norm, ys)
    return out.reshape(B, S, D)
```

```python
import functools
import math

import jax
import jax.numpy as jnp
from jax import lax
from jax.experimental import pallas as pl
from jax.experimental.pallas import tpu as pltpu

D_MODEL = 1024
EPS = 1e-6
POOL_WINDOWS = (2, 4, 8, 16)
POOL_GROUP_DIM = D_MODEL // len(POOL_WINDOWS)
MAX_POOL_WINDOW = max(POOL_WINDOWS)
HEAD_DIM = 64
HEADS_PER_GROUP = D_MODEL // HEAD_DIM
DILATED_PATTERNS = ((128, 1), (512, 4), (2048, 16))
N_ATT_GROUPS = len(DILATED_PATTERNS)
QBLK = 128
NEG_INF = -1e30
LOG2E = 1.4426950408889634
Q_SCALE = HEAD_DIM ** -0.5 * LOG2E
N_BUCKETS = 32
MAX_EXACT = N_BUCKETS // 2
MAX_DISTANCE = 2048
N_EXPERTS = 8
TOP_K = 2
SEQ_LEN = 2048

LANES = 128
SUBLANES = 8
N_SLABS = D_MODEL // LANES
HEAD_PAIR = 2 * HEAD_DIM
assert HEAD_PAIR == LANES

LAYER0_ROWS = 512
QKV_ROWS = 512
ATTN_BLOCKS_PER_STEP = 8
MERGE_ROWS = 512
MOE_ROWS = 1024
MOE_FT = 1792
MOE_CHUNK = 512
MOE_SUBROWS = 256
MOE_NT = 1024
RUN_ALIGN = 16
STAGE_ROWS = 1152
VMEM_LIMIT = 60 * 1024 * 1024

R_E0, R_E1, R_RANK0, R_RANK1, R_G0, R_G1 = 0, 1, 2, 3, 4, 5
T_BASE, T_COUNT = 0, 1


def _rms_scale(x):
    return lax.rsqrt(jnp.mean(x * x, axis=-1, keepdims=True) + EPS)


def _silu(g):
    return g * (1.0 / (1.0 + jnp.exp(-g)))


def _split_bf16(x):
    hi = x.astype(jnp.bfloat16)
    lo = (x - hi.astype(jnp.float32)).astype(jnp.bfloat16)
    return hi, lo


def _side_cast_specs(side, n_steps):
    rows = side.shape[0] // n_steps
    spec = pl.BlockSpec((rows, side.shape[1]), lambda i, *_: (i, 0))
    return spec, jax.ShapeDtypeStruct(side.shape, jnp.bfloat16)


def _pool_mix(x, halo, seq_tile, gain, proj_ref, scale):
    xn = x * _rms_scale(x) * gain
    hn_halo = halo * _rms_scale(halo) * gain
    hn_halo = jnp.where(seq_tile > 0, hn_halo, 0.0)
    full = jnp.concatenate([hn_halo, xn], axis=0)
    ts = x.shape[0]
    pos = seq_tile * ts + lax.broadcasted_iota(jnp.int32, (ts, 1), 0)
    outs = []
    for g, w in enumerate(POOL_WINDOWS):
        c0 = g * POOL_GROUP_DIM
        s = full[:, c0:c0 + POOL_GROUP_DIM]
        span = 1
        while span < w:
            s = s + pltpu.roll(s, span, axis=0)
            span *= 2
        s = s[MAX_POOL_WINDOW:, :]
        cnt = jnp.minimum(pos + 1, w).astype(jnp.float32)
        pooled = s / cnt - xn[:, c0:c0 + POOL_GROUP_DIM]
        outs.append(jnp.dot(pooled.astype(jnp.bfloat16), proj_ref[g],
                            preferred_element_type=jnp.float32))
    return x + jnp.concatenate(outs, axis=1) * scale


def _layer0_kernel(*refs, tiles_per_seq, n_side):
    (x_ref, halo_ref, an_ref, proj_ref, asc_ref, fn_ref, wg_ref, wu_ref, wd_ref) = refs[:9]
    side_in = refs[9:9 + n_side]
    h_out_ref = refs[9 + n_side]
    side_out = refs[10 + n_side:]
    for src, dst in zip(side_in, side_out):
        dst[...] = src[...].astype(dst.dtype)
    seq_tile = lax.rem(pl.program_id(0), tiles_per_seq)
    h = _pool_mix(x_ref[...], halo_ref[...], seq_tile, an_ref[...], proj_ref, asc_ref[...])
    hn = (h * _rms_scale(h) * fn_ref[...]).astype(jnp.bfloat16)
    g = jnp.dot(hn, wg_ref[...], preferred_element_type=jnp.float32)
    u = jnp.dot(hn, wu_ref[...], preferred_element_type=jnp.float32)
    hid = (_silu(g) * u).astype(jnp.bfloat16)
    h_out_ref[...] = h + jnp.dot(hid, wd_ref[...], preferred_element_type=jnp.float32)


def _layer0(x, a_norm, a_proj, a_scale, ffn_norm, w_gate, w_up, w_down, sides):
    T, D = x.shape
    F = w_gate.shape[1]
    ts = LAYER0_ROWS
    n_steps = T // ts
    halo_blocks = ts // MAX_POOL_WINDOW
    row = lambda i: (i, 0)
    vec = pl.BlockSpec((1, D), lambda i: (0, 0))
    resident = dict(index_map=lambda i: (0, 0), pipeline_mode=pl.Buffered(1))
    side_specs, side_shapes = zip(*[_side_cast_specs(a, n_steps) for a in sides])
    outs = pl.pallas_call(
        functools.partial(_layer0_kernel, tiles_per_seq=SEQ_LEN // ts, n_side=len(sides)),
        grid=(n_steps,),
        in_specs=[
            pl.BlockSpec((ts, D), row),
            pl.BlockSpec((MAX_POOL_WINDOW, D), lambda i: (jnp.maximum(i * halo_blocks - 1, 0), 0)),
            vec,
            pl.BlockSpec(a_proj.shape, lambda i: (0, 0, 0), pipeline_mode=pl.Buffered(1)),
            vec, vec,
            pl.BlockSpec((D, F), **resident),
            pl.BlockSpec((D, F), **resident),
            pl.BlockSpec((F, D), **resident),
            *side_specs,
        ],
        out_specs=[pl.BlockSpec((ts, D), row), *side_specs],
        out_shape=[jax.ShapeDtypeStruct((T, D), jnp.float32), *side_shapes],
        compiler_params=pltpu.CompilerParams(
            dimension_semantics=("arbitrary",), vmem_limit_bytes=VMEM_LIMIT),
        name="pool_mixer_dense_swiglu",
    )(x, x, a_norm.reshape(1, D), a_proj.astype(jnp.bfloat16), a_scale.reshape(1, D),
      ffn_norm.reshape(1, D), w_gate.astype(jnp.bfloat16), w_up.astype(jnp.bfloat16),
      w_down.astype(jnp.bfloat16), *sides)
    return outs[0], outs[1:]


def _qkv_kernel(*refs):
    slabs = refs[:N_SLABS]
    kvg_ref, qg_ref, wq_ref, wkv_ref, side_in_ref = refs[N_SLABS:N_SLABS + 5]
    out_refs = refs[N_SLABS + 5:-1]
    side_out_ref = refs[-1]
    side_out_ref[...] = side_in_ref[...].astype(side_out_ref.dtype)
    tm = slabs[0].shape[0]
    for g, (_, dilation) in enumerate(DILATED_PATTERNS):
        n = tm // dilation
        cols = []
        for slab in slabs:
            if dilation == 1:
                cols.append(slab[...])
            else:
                cols.append(jnp.concatenate(
                    [slab[pl.ds(r, n, stride=dilation), :] for r in range(dilation)], axis=0))
        x = jnp.concatenate(cols, axis=1)
        xs = x * _rms_scale(x)
        xq = (xs * qg_ref[...]).astype(jnp.bfloat16)
        xkv = (xs * kvg_ref[...]).astype(jnp.bfloat16)
        D = D_MODEL
        kcols = slice(g * D, (g + 1) * D)
        vcols = slice((N_ATT_GROUPS + g) * D, (N_ATT_GROUPS + g + 1) * D)
        q = jnp.dot(xq, wq_ref[:, kcols], preferred_element_type=jnp.float32) * Q_SCALE
        k = jnp.dot(xkv, wkv_ref[:, kcols], preferred_element_type=jnp.float32)
        v = jnp.dot(xkv, wkv_ref[:, vcols], preferred_element_type=jnp.float32)
        for out_ref, val in zip(out_refs[3 * g:3 * g + 3], (q, k, v)):
            for r in range(dilation):
                out_ref[r] = val[r * n:(r + 1) * n].astype(out_ref.dtype)


def _qkv_project(h, kv_norm, b_norm, w_q, w_kv, B, S, side):
    T, D = h.shape
    tm = QKV_ROWS
    tiles_per_seq = S // tm
    resident = dict(index_map=lambda i: (0, 0), pipeline_mode=pl.Buffered(1))
    slab_specs = [pl.BlockSpec((tm, LANES), functools.partial(lambda i, c: (i, c), c=c))
                  for c in range(N_SLABS)]
    vec = pl.BlockSpec((1, D), lambda i: (0, 0))
    out_specs, out_shapes = [], []
    for _, d in DILATED_PATTERNS:
        spec = pl.BlockSpec((None, d, tm // d, D),
                            lambda i: (i // tiles_per_seq, 0, i % tiles_per_seq, 0))
        out_specs += [spec] * 3
        out_shapes += [jax.ShapeDtypeStruct((B, d, S // d, D), jnp.bfloat16)] * 3
    side_spec, side_shape = _side_cast_specs(side, T // tm)
    outs = pl.pallas_call(
        _qkv_kernel,
        grid=(T // tm,),
        in_specs=slab_specs + [vec, vec, pl.BlockSpec(w_q.shape, **resident),
                               pl.BlockSpec(w_kv.shape, **resident), side_spec],
        out_specs=out_specs + [side_spec],
        out_shape=out_shapes + [side_shape],
        compiler_params=pltpu.CompilerParams(
            dimension_semantics=("arbitrary",), vmem_limit_bytes=VMEM_LIMIT),
        name="qkv_proj",
    )(*([h] * N_SLABS), kv_norm.reshape(1, D), b_norm.reshape(1, D), w_q, w_kv, side)
    return [tuple(outs[3 * g:3 * g + 3]) for g in range(N_ATT_GROUPS)], outs[-1]


def _attn_kernel(*refs, dilation, has_prev, jb, rb):
    if has_prev:
        (table_ref, bmap_ref, q_ref, kp_ref, kc_ref, vp_ref, vc_ref,
         o_ref, stat_ref, bias_ref, stat_scr) = refs
    else:
        table_ref, bmap_ref, q_ref, kc_ref, vc_ref, o_ref, stat_ref, bias_ref, stat_scr = refs
    b, jstep, rstep = pl.program_id(0), pl.program_id(1), pl.program_id(2)
    n_keys = bias_ref.shape[2]

    @pl.when(jnp.logical_and(jnp.logical_and(b == 0, jstep == 0), rstep == 0))
    def _():
        stat_scr[...] = jnp.zeros_like(stat_scr)
        bmap = bmap_ref[...]
        in_prev = lax.broadcasted_iota(jnp.int32, bmap.shape, 1) < n_keys - QBLK

        def head(h, carry):
            acc = jnp.full(bmap.shape, NEG_INF, jnp.float32)
            for bucket in range(N_BUCKETS):
                acc = jnp.where(bmap == bucket, table_ref[bucket, h], acc)
            rows_h = pl.ds(pl.multiple_of(h * QBLK, QBLK), QBLK)
            bias_ref[0, rows_h, :] = jnp.where(in_prev, NEG_INF, acc)
            bias_ref[1, rows_h, :] = acc
            return carry

        lax.fori_loop(0, HEADS_PER_GROUP, head, 0)

    lane = lax.broadcasted_iota(jnp.int32, (QBLK, LANES), 1)
    first_head = lane < HEAD_DIM
    for rr in range(rb):
        for jj in range(jb):
            qrows = slice(jj * QBLK, (jj + 1) * QBLK)
            bias_copy = jnp.minimum(jstep, 1) if jj == 0 else 1
            if dilation > 1:
                out_rows = pl.ds(dilation * QBLK * jj + rstep * rb + rr, QBLK, stride=dilation)
            else:
                out_rows = qrows
            for hp in range(HEADS_PER_GROUP // 2):
                sl = slice(hp * HEAD_PAIR, (hp + 1) * HEAD_PAIR)
                qp = q_ref[rr, qrows, sl]
                zero = jnp.zeros_like(qp)
                q2 = jnp.concatenate([jnp.where(first_head, qp, zero),
                                      jnp.where(first_head, zero, qp)], axis=0)
                if has_prev:
                    if jj == 0:
                        k_prev, v_prev = kp_ref[rr, :, sl], vp_ref[rr, :, sl]
                    else:
                        prows = slice((jj - 1) * QBLK, jj * QBLK)
                        k_prev, v_prev = kc_ref[rr, prows, sl], vc_ref[rr, prows, sl]
                    kcat = jnp.concatenate([k_prev, kc_ref[rr, qrows, sl]], axis=0)
                    vcat = jnp.concatenate([v_prev, vc_ref[rr, qrows, sl]], axis=0)
                else:
                    kcat, vcat = kc_ref[rr, qrows, sl], vc_ref[rr, qrows, sl]
                s = lax.dot_general(q2, kcat, (((1,), (1,)), ((), ())),
                                    preferred_element_type=jnp.float32)
                s = s + bias_ref[bias_copy, hp * 2 * QBLK:(hp + 1) * 2 * QBLK, :]
                m = jnp.max(s, axis=1, keepdims=True)
                p = jnp.exp2(s - m)
                l = jnp.sum(p, axis=1, keepdims=True)
                o2 = jnp.dot(p.astype(jnp.bfloat16), vcat, preferred_element_type=jnp.float32)
                o_ref[hp, out_rows, :] = jnp.where(first_head, o2[:QBLK], o2[QBLK:])
                for half, head in ((slice(0, QBLK), 2 * hp), (slice(QBLK, 2 * QBLK), 2 * hp + 1)):
                    stat_scr[:, head:head + 1] = m[half]
                    stat_scr[:, HEADS_PER_GROUP + head:HEADS_PER_GROUP + head + 1] = l[half]
            stat_ref[out_rows, :] = stat_scr[...]


def _bucket_map(dilation, window, n_keys):
    W = window // dilation
    a = jnp.arange(QBLK, dtype=jnp.int32)[:, None]
    c = jnp.arange(n_keys, dtype=jnp.int32)[None, :]
    m = a + (n_keys - QBLK) - c
    band = (m >= 0) & (m <= W)
    n = jnp.maximum(m, 0) * dilation
    nf = jnp.maximum(n, 1).astype(jnp.float32)
    large = MAX_EXACT + (jnp.log(nf / MAX_EXACT) / math.log(MAX_DISTANCE / MAX_EXACT)
                         * (N_BUCKETS - MAX_EXACT)).astype(jnp.int32)
    large = jnp.minimum(large, N_BUCKETS - 1)
    bucket = jnp.where(n < MAX_EXACT, n, large)
    return jnp.where(band, bucket, -1)


def _attention_group(q, k, v, rel_bias, g, B, S):
    window, d = DILATED_PATTERNS[g]
    L = S // d
    n = L // QBLK
    D = D_MODEL
    has_prev = n > 1
    n_keys = 2 * QBLK if has_prev else QBLK
    table = rel_bias[:, g * HEADS_PER_GROUP:(g + 1) * HEADS_PER_GROUP].astype(jnp.float32) * LOG2E
    bmap = _bucket_map(d, window, n_keys)
    jb = min(ATTN_BLOCKS_PER_STEP, n)
    rb = ATTN_BLOCKS_PER_STEP // jb
    cur_blk = (None, rb, jb * QBLK, D)
    prev_blk = (None, rb, QBLK, D)
    cur = lambda b, j, r: (b, r, j, 0)
    prev = lambda b, j, r: (b, r, jnp.maximum(j * jb - 1, 0), 0)
    if has_prev:
        kv_specs = [pl.BlockSpec(prev_blk, prev), pl.BlockSpec(cur_blk, cur),
                    pl.BlockSpec(prev_blk, prev), pl.BlockSpec(cur_blk, cur)]
        kv_args = (k, k, v, v)
    else:
        kv_specs = [pl.BlockSpec(cur_blk, cur), pl.BlockSpec(cur_blk, cur)]
        kv_args = (k, v)
    n_j = n // jb
    span = QBLK * jb * d
    return pl.pallas_call(
        functools.partial(_attn_kernel, dilation=d, has_prev=has_prev, jb=jb, rb=rb),
        grid=(B, n_j, d // rb),
        in_specs=[pl.BlockSpec(memory_space=pltpu.SMEM),
                  pl.BlockSpec(bmap.shape, lambda b, j, r: (0, 0)),
                  pl.BlockSpec(cur_blk, cur)] + kv_specs,
        out_specs=[pl.BlockSpec((N_SLABS, span, LANES), lambda b, j, r: (0, b * n_j + j, 0)),
                   pl.BlockSpec((span, LANES), lambda b, j, r: (b * n_j + j, 0))],
        out_shape=[jax.ShapeDtypeStruct((N_SLABS, B * S, LANES), jnp.float32),
                   jax.ShapeDtypeStruct((B * S, LANES), jnp.float32)],
        scratch_shapes=[pltpu.VMEM((2, HEADS_PER_GROUP * QBLK, n_keys), jnp.float32),
                        pltpu.VMEM((QBLK, LANES), jnp.float32)],
        compiler_params=pltpu.CompilerParams(
            dimension_semantics=("arbitrary",) * 3, vmem_limit_bytes=VMEM_LIMIT),
        name=f"dilated_attn_g{g}",
    )(table, bmap, q, *kv_args)


def _merge_kernel(o0_ref, o1_ref, o2_ref, s0_ref, s1_ref, s2_ref, h_ref, wo_ref, fn_ref,
                  r2_ref, expand_ref,
                  h_out_ref, hn_ref, route_ref, route_t_ref, tinfo_ref, counts_ref, carry_ref):
    i = pl.program_id(0)
    tm = h_ref.shape[0]

    @pl.when(i == 0)
    def _():
        carry_ref[...] = jnp.zeros_like(carry_ref)

    lane = lax.broadcasted_iota(jnp.int32, (tm, LANES), 1)
    head_lane = lane < HEADS_PER_GROUP
    stats = [s0_ref[...], s1_ref[...], s2_ref[...]]
    dens = [pltpu.roll(st, LANES - HEADS_PER_GROUP, axis=1) for st in stats]
    mx = jnp.maximum(jnp.maximum(stats[0], stats[1]), stats[2])
    es = [jnp.exp2(st - mx) for st in stats]
    inv = 1.0 / (es[0] * dens[0] + es[1] * dens[1] + es[2] * dens[2])
    merged = None
    for e, o_ref in zip(es, (o0_ref, o1_ref, o2_ref)):
        w = jnp.where(head_lane, e * inv, 0.0)
        hi = w.astype(jnp.bfloat16).astype(jnp.float32)
        packed = (hi + pltpu.roll(w - hi, HEADS_PER_GROUP, axis=1)).astype(jnp.bfloat16)
        wide = jnp.dot(packed, expand_ref[...], preferred_element_type=jnp.float32)
        o = jnp.concatenate([o_ref[c] for c in range(N_SLABS)], axis=1)
        term = wide * o
        merged = term if merged is None else merged + term
    h = h_ref[...] + jnp.dot(merged.astype(jnp.bfloat16), wo_ref[...],
                             preferred_element_type=jnp.float32)
    h_out_ref[...] = h
    hn = h * _rms_scale(h) * fn_ref[...]
    hn_ref[...] = hn.astype(hn_ref.dtype)

    hi, lo = _split_bf16(hn)
    both = jnp.dot(hi, r2_ref[...], preferred_element_type=jnp.float32)
    logits = (both[:, :LANES] + both[:, LANES:]
              + jnp.dot(lo, r2_ref[:, :LANES], preferred_element_type=jnp.float32))
    logits = jnp.where(lane < N_EXPERTS, logits, -jnp.inf)
    v0 = jnp.max(logits, axis=1, keepdims=True)
    e0 = jnp.min(jnp.where(logits == v0, lane, LANES), axis=1, keepdims=True)
    rest = jnp.where(lane == e0, -jnp.inf, logits)
    v1 = jnp.max(rest, axis=1, keepdims=True)
    e1 = jnp.min(jnp.where(rest == v1, lane, LANES), axis=1, keepdims=True)
    t = jnp.exp(v1 - v0)
    g0 = 1.0 / (1.0 + t)
    g1 = t / (1.0 + t)

    hit0 = lane == e0
    hit1 = lane == e1
    onehot = jnp.where(jnp.logical_or(hit0, hit1), 1.0, 0.0)
    r_i = lax.broadcasted_iota(jnp.int32, (tm, tm), 0)
    c_i = lax.broadcasted_iota(jnp.int32, (tm, tm), 1)
    lower = jnp.where(c_i < r_i, 1.0, 0.0).astype(jnp.bfloat16)
    before = jnp.dot(lower, onehot.astype(jnp.bfloat16), preferred_element_type=jnp.float32)
    rank0 = jnp.sum(jnp.where(hit0, before, 0.0), axis=1, keepdims=True)
    rank1 = jnp.sum(jnp.where(hit1, before, 0.0), axis=1, keepdims=True)
    tile_counts = jnp.sum(onehot, axis=0, keepdims=True)
    base = carry_ref[0:1, :]
    total = base + jnp.floor((tile_counts + (RUN_ALIGN - 1)) * (1.0 / RUN_ALIGN)) * RUN_ALIGN
    carry_ref[...] = jnp.broadcast_to(total, carry_ref.shape)
    counts_ref[...] = jnp.broadcast_to(total, counts_ref.shape)
    row_id = lax.broadcasted_iota(jnp.int32, tinfo_ref.shape, 0)
    tinfo_ref[...] = jnp.where(row_id == T_BASE, base, jnp.where(row_id == T_COUNT, tile_counts, 0.0))

    route = jnp.zeros((tm, LANES), jnp.float32)
    for ln, val in ((R_E0, e0.astype(jnp.float32)), (R_E1, e1.astype(jnp.float32)),
                    (R_RANK0, rank0), (R_RANK1, rank1), (R_G0, g0), (R_G1, g1)):
        route = jnp.where(lane == ln, val, route)
    route_ref[...] = route
    route_t_ref[...] = route.T[:SUBLANES, :]


def _merge_layer(outs, lses, h, w_o, ffn_norm, w_router):
    T, D = h.shape
    tm = MERGE_ROWS
    r_pad = jnp.zeros((D, LANES), jnp.float32).at[:, :N_EXPERTS].set(w_router)
    r2 = jnp.concatenate(_split_bf16(r_pad), axis=1)
    head_of_col = jnp.arange(D, dtype=jnp.int32) // HEAD_DIM
    lane_id = jnp.arange(LANES, dtype=jnp.int32)[:, None]
    expand = ((lane_id < 2 * HEADS_PER_GROUP)
              & (lane_id % HEADS_PER_GROUP == head_of_col[None, :])).astype(jnp.bfloat16)
    row = lambda i: (i, 0)
    const = lambda i: (0, 0)
    wide = pl.BlockSpec((tm, D), row)
    narrow = pl.BlockSpec((tm, LANES), row)
    slabs = pl.BlockSpec((N_SLABS, tm, LANES), lambda i: (0, i, 0))
    return pl.pallas_call(
        _merge_kernel,
        grid=(T // tm,),
        in_specs=[slabs, slabs, slabs, narrow, narrow, narrow, wide,
                  pl.BlockSpec((D, D), const), pl.BlockSpec((1, D), const),
                  pl.BlockSpec((D, 2 * LANES), const), pl.BlockSpec((LANES, D), const)],
        out_specs=[wide, wide, narrow,
                   pl.BlockSpec((None, SUBLANES, tm), lambda i: (i, 0, 0)),
                   pl.BlockSpec((None, SUBLANES, LANES), lambda i: (i, 0, 0)),
                   pl.BlockSpec((8, LANES), const)],
        out_shape=[jax.ShapeDtypeStruct((T, D), jnp.float32),
                   jax.ShapeDtypeStruct((T, D), jnp.bfloat16),
                   jax.ShapeDtypeStruct((T, LANES), jnp.float32),
                   jax.ShapeDtypeStruct((T // tm, SUBLANES, tm), jnp.float32),
                   jax.ShapeDtypeStruct((T // tm, SUBLANES, LANES), jnp.float32),
                   jax.ShapeDtypeStruct((8, LANES), jnp.float32)],
        scratch_shapes=[pltpu.VMEM((8, LANES), jnp.float32)],
        compiler_params=pltpu.CompilerParams(
            dimension_semantics=("arbitrary",), vmem_limit_bytes=VMEM_LIMIT),
        name="merge_outproj_router",
    )(*outs, *lses, h, w_o, ffn_norm.reshape(1, D), r2, expand)


def _run_offsets(cnt_ref, tile):
    sizes, offs, off = [], [], 0
    for e in range(N_EXPERTS):
        size = ((cnt_ref[tile * N_EXPERTS + e] + (RUN_ALIGN - 1)) // RUN_ALIGN) * RUN_ALIGN
        sizes.append(size)
        offs.append(off)
        off = off + size
    return sizes, offs


def _for_each_run_piece(start_ref, cnt_ref, tile, fn):
    sizes, offs = _run_offsets(cnt_ref, tile)
    for e in range(N_EXPERTS):
        slot0 = start_ref[tile * N_EXPERTS + e]
        bit = MERGE_ROWS
        while bit >= RUN_ALIGN:
            done = sizes[e] & -(2 * bit)

            @pl.when((sizes[e] & bit) != 0)
            def _():
                fn(pl.multiple_of(offs[e] + done, RUN_ALIGN),
                   pl.multiple_of(slot0 + done, RUN_ALIGN), bit)
            bit //= 2


def _staging_columns(expert, rank, offs):
    col = rank
    for e in range(N_EXPERTS):
        col = col + jnp.where(expert == e, offs[e], 0)
    return col


def _dispatch_kernel(start_ref, cnt_ref, pad_start_ref, pad_len_ref, used_ref,
                     hn_ref, route_t_ref, side_in_ref, xs_ref, side_out_ref,
                     xbuf, zbuf, sems, zsem):
    side_out_ref[...] = side_in_ref[...].astype(side_out_ref.dtype)
    i = pl.program_id(0)
    n_steps = pl.num_programs(0)
    slot = lax.rem(i, 2)
    n_tiles = xs_ref.shape[0] // MOE_ROWS

    def zero_fill(act):
        for e in range(N_EXPERTS):
            run_start = pl.multiple_of(pad_start_ref[e], RUN_ALIGN)
            run_len = pad_len_ref[e]
            bit = MOE_ROWS // 2
            while bit >= RUN_ALIGN:
                off = pl.multiple_of(run_start + (run_len & -(2 * bit)), RUN_ALIGN)

                @pl.when((run_len & bit) != 0)
                def _():
                    act(pltpu.make_async_copy(zbuf.at[pl.ds(0, bit), :],
                                              xs_ref.at[pl.ds(off, bit), :], zsem))
                bit //= 2
        for t in range(n_tiles):
            @pl.when(t >= used_ref[0])
            def _():
                act(pltpu.make_async_copy(zbuf, xs_ref.at[pl.ds(t * MOE_ROWS, MOE_ROWS), :], zsem))

    def run_copies(tile, buf_slot, act):
        def piece(buf_row, slot_row, rows):
            act(pltpu.make_async_copy(xbuf.at[buf_slot, pl.ds(buf_row, rows), :],
                                      xs_ref.at[pl.ds(slot_row, rows), :], sems.at[buf_slot]))
        _for_each_run_piece(start_ref, cnt_ref, tile, piece)

    @pl.when(i == 0)
    def _():
        zbuf[...] = jnp.zeros_like(zbuf)
        zero_fill(lambda c: c.start())

    @pl.when(i >= 2)
    def _():
        run_copies(i - 2, slot, lambda c: c.wait())

    _, offs = _run_offsets(cnt_ref, i)
    route_t = route_t_ref[...]
    cols = [_staging_columns(route_t[R_E0 + k:R_E0 + k + 1, :].astype(jnp.int32),
                             route_t[R_RANK0 + k:R_RANK0 + k + 1, :].astype(jnp.int32), offs)
            for k in range(TOP_K)]
    buf_row = lax.broadcasted_iota(jnp.int32, (STAGE_ROWS, hn_ref.shape[0]), 0)
    onehot = jnp.where(jnp.logical_or(buf_row == cols[0], buf_row == cols[1]), 1.0, 0.0)
    xbuf[slot] = jnp.dot(onehot.astype(jnp.bfloat16), hn_ref[...],
                         preferred_element_type=jnp.float32).astype(xbuf.dtype)
    run_copies(i, slot, lambda c: c.start())

    @pl.when(i == n_steps - 1)
    def _():
        @pl.when(i >= 1)
        def _():
            run_copies(i - 1, 1 - slot, lambda c: c.wait())
        run_copies(i, slot, lambda c: c.wait())
        zero_fill(lambda c: c.wait())


def _dispatch(run_start, run_cnt, pad_start, pad_len, used, hn, route_t, n_slots, side):
    T, D = hn.shape
    tm = MERGE_ROWS
    side_spec, side_shape = _side_cast_specs(side, T // tm)
    return pl.pallas_call(
        _dispatch_kernel,
        grid_spec=pltpu.PrefetchScalarGridSpec(
            num_scalar_prefetch=5,
            grid=(T // tm,),
            in_specs=[pl.BlockSpec((tm, D), lambda i, *_: (i, 0)),
                      pl.BlockSpec((None, SUBLANES, tm), lambda i, *_: (i, 0, 0)),
                      side_spec],
            out_specs=[pl.BlockSpec(memory_space=pl.ANY), side_spec],
            scratch_shapes=[pltpu.VMEM((2, STAGE_ROWS, D), jnp.bfloat16),
                            pltpu.VMEM((MOE_ROWS, D), jnp.bfloat16),
                            pltpu.SemaphoreType.DMA((2,)), pltpu.SemaphoreType.DMA(())],
        ),
        out_shape=[jax.ShapeDtypeStruct((n_slots, D), jnp.bfloat16), side_shape],
        compiler_params=pltpu.CompilerParams(
            dimension_semantics=("arbitrary",), vmem_limit_bytes=VMEM_LIMIT),
        name="moe_dispatch",
    )(run_start, run_cnt, pad_start, pad_len, used, hn, route_t, side)


def _expert_kernel(tile_e_ref, used_ref, valid_ref, x_ref, wg_ref, wu_ref, wd_ref, y_ref,
                   hid_ref):
    del tile_e_ref
    i = pl.program_id(0)
    s = pl.program_id(1)
    n_f, tm, tf = hid_ref.shape
    active = i < used_ref[0]
    n_valid = valid_ref[jnp.minimum(i, used_ref[0] - 1)]
    full = n_valid == tm
    gate_step = jnp.logical_and(active, s < n_f)
    down_step = jnp.logical_and(active, s >= n_f)

    def gate_up(rows):
        x = x_ref[rows, :]
        for c0 in range(0, tf, MOE_CHUNK):
            cols = slice(c0, min(c0 + MOE_CHUNK, tf))
            g = jnp.dot(x, wg_ref[:, cols], preferred_element_type=jnp.float32)
            u = jnp.dot(x, wu_ref[:, cols], preferred_element_type=jnp.float32)
            hid_ref[s, rows, cols] = (_silu(g) * u).astype(hid_ref.dtype)

    def down(rows):
        hid = jnp.concatenate([hid_ref[c, rows, :] for c in range(n_f)], axis=1)
        y_ref[rows, :] = jnp.dot(hid, wd_ref[...],
                                 preferred_element_type=jnp.float32).astype(y_ref.dtype)

    @pl.when(jnp.logical_and(gate_step, full))
    def _():
        gate_up(slice(None))

    @pl.when(jnp.logical_and(down_step, full))
    def _():
        down(slice(None))

    for r0 in range(0, tm, MOE_SUBROWS):
        rows = slice(r0, r0 + MOE_SUBROWS)
        occupied = jnp.logical_and(jnp.logical_not(full), r0 < n_valid)

        @pl.when(jnp.logical_and(gate_step, occupied))
        def _():
            gate_up(rows)

        @pl.when(jnp.logical_and(down_step, occupied))
        def _():
            down(rows)

        @pl.when(jnp.logical_and(down_step, r0 >= n_valid))
        def _():
            y_ref[rows, :] = jnp.zeros((MOE_SUBROWS, y_ref.shape[1]), y_ref.dtype)

    @pl.when(jnp.logical_and(jnp.logical_not(active), s >= n_f))
    def _():
        y_ref[...] = jnp.zeros_like(y_ref)


def _expert_ffn(tile_e, used, tile_valid, xs, w_gate, w_up, w_down):
    n_slots, D = xs.shape
    F = w_gate.shape[2]
    tm, tf, tn = MOE_ROWS, MOE_FT, MOE_NT
    n_f, n_n = F // tf, D // tn
    n_tiles = n_slots // tm
    last = n_f + n_n - 1

    def eff(i, s, used_ref):
        idle = i >= used_ref[0]
        return jnp.where(idle, used_ref[0] - 1, i), jnp.where(idle, last, s)

    def x_map(i, s, te, us, nv):
        ie, se = eff(i, s, us)
        return jnp.minimum(ie + jnp.where(se >= n_f, 1, 0), us[0] - 1), 0

    def gate_map(i, s, te, us, nv):
        ie, se = eff(i, s, us)
        return te[ie], 0, jnp.minimum(se, n_f - 1)

    def down_map(i, s, te, us, nv):
        ie, se = eff(i, s, us)
        tile = jnp.where(se == 0, jnp.maximum(ie - 1, 0), ie)
        return te[tile], 0, jnp.where(se == 0, n_n - 1, jnp.clip(se - n_f, 0, n_n - 1))

    def out_map(i, s, te, us, nv):
        return i, jnp.clip(s - n_f, 0, n_n - 1)

    return pl.pallas_call(
        _expert_kernel,
        grid_spec=pltpu.PrefetchScalarGridSpec(
            num_scalar_prefetch=3,
            grid=(n_tiles, n_f + n_n),
            in_specs=[
                pl.BlockSpec((tm, D), x_map),
                pl.BlockSpec((None, D, tf), gate_map),
                pl.BlockSpec((None, D, tf), gate_map),
                pl.BlockSpec((None, F, tn), down_map),
            ],
            out_specs=pl.BlockSpec((tm, tn), out_map),
            scratch_shapes=[pltpu.VMEM((n_f, tm, tf), jnp.bfloat16)],
        ),
        out_shape=jax.ShapeDtypeStruct((n_slots, D), jnp.bfloat16),
        compiler_params=pltpu.CompilerParams(
            dimension_semantics=("arbitrary", "arbitrary"), vmem_limit_bytes=VMEM_LIMIT),
        name="moe_experts",
    )(tile_e, used, tile_valid, xs, w_gate, w_up, w_down)


def _combine_kernel(start_ref, cnt_ref, h_ref, route_ref, fin_ref, ys_ref, o_ref, ybuf, sems):
    i = pl.program_id(0)
    n_steps = pl.num_programs(0)
    tile = h_ref.shape[0]
    slot = lax.rem(i, 2)

    def run_copies(step, buf_slot, act):
        def piece(buf_row, slot_row, rows):
            act(pltpu.make_async_copy(ys_ref.at[pl.ds(slot_row, rows), :],
                                      ybuf.at[buf_slot, pl.ds(buf_row, rows), :],
                                      sems.at[buf_slot]))
        _for_each_run_piece(start_ref, cnt_ref, step, piece)

    @pl.when(i == 0)
    def _():
        ybuf[...] = jnp.zeros_like(ybuf)
        run_copies(0, 0, lambda c: c.start())

    @pl.when(i + 1 < n_steps)
    def _():
        run_copies(i + 1, 1 - slot, lambda c: c.start())

    run_copies(i, slot, lambda c: c.wait())
    _, offs = _run_offsets(cnt_ref, i)
    route = route_ref[...]
    staged = ybuf[slot]
    buf_col = lax.broadcasted_iota(jnp.int32, (tile, STAGE_ROWS), 1)
    ys = []
    for k in range(TOP_K):
        col = _staging_columns(route[:, R_E0 + k:R_E0 + k + 1].astype(jnp.int32),
                               route[:, R_RANK0 + k:R_RANK0 + k + 1].astype(jnp.int32), offs)
        onehot = jnp.where(buf_col == col, 1.0, 0.0).astype(jnp.bfloat16)
        ys.append(jnp.dot(onehot, staged, preferred_element_type=jnp.float32))
    g0 = route[:, R_G0:R_G0 + 1]
    g1 = route[:, R_G1:R_G1 + 1]
    h = h_ref[...] + (g0 * ys[0] + g1 * ys[1])
    o_ref[...] = h * _rms_scale(h) * fin_ref[...]


def _combine(run_start, run_cnt, h, route, final_norm, ys):
    T, D = h.shape
    tc = MERGE_ROWS
    return pl.pallas_call(
        _combine_kernel,
        grid_spec=pltpu.PrefetchScalarGridSpec(
            num_scalar_prefetch=2,
            grid=(T // tc,),
            in_specs=[pl.BlockSpec((tc, D), lambda i, *_: (i, 0)),
                      pl.BlockSpec((tc, LANES), lambda i, *_: (i, 0)),
                      pl.BlockSpec((1, D), lambda i, *_: (0, 0)),
                      pl.BlockSpec(memory_space=pl.ANY)],
            out_specs=pl.BlockSpec((tc, D), lambda i, *_: (i, 0)),
            scratch_shapes=[pltpu.VMEM((2, STAGE_ROWS, D), jnp.bfloat16),
                            pltpu.SemaphoreType.DMA((2,))],
        ),
        out_shape=jax.ShapeDtypeStruct((T, D), jnp.float32),
        compiler_params=pltpu.CompilerParams(
            dimension_semantics=("arbitrary",), vmem_limit_bytes=VMEM_LIMIT),
        name="moe_combine_final_norm",
    )(run_start, run_cnt, h, route, final_norm.reshape(1, D), ys)


def _routing_tables(tinfo, counts_row, n_tokens):
    counts = counts_row[0, :N_EXPERTS].astype(jnp.int32)
    padded = ((counts + MOE_ROWS - 1) // MOE_ROWS) * MOE_ROWS
    ends = jnp.cumsum(padded)
    starts = ends - padded
    run_start = starts[None, :] + tinfo[:, T_BASE, :N_EXPERTS].astype(jnp.int32)
    run_cnt = tinfo[:, T_COUNT, :N_EXPERTS].astype(jnp.int32)
    run_padding = (n_tokens // MERGE_ROWS) * N_EXPERTS * (RUN_ALIGN - 1)
    n_tiles = (TOP_K * n_tokens + run_padding + MOE_ROWS - 1) // MOE_ROWS + N_EXPERTS
    tile_start = jnp.arange(n_tiles, dtype=jnp.int32) * MOE_ROWS
    tile_e = jnp.minimum(jnp.sum(ends[None, :] <= tile_start[:, None], axis=1),
                         N_EXPERTS - 1).astype(jnp.int32)
    used = (ends[-1:] // MOE_ROWS).astype(jnp.int32)
    tile_valid = jnp.clip(counts[tile_e] - (tile_start - starts[tile_e]), 0, MOE_ROWS).astype(jnp.int32)
    pad_start = (starts + counts).astype(jnp.int32)
    pad_len = (padded - counts).astype(jnp.int32)
    return (run_start.reshape(-1), run_cnt.reshape(-1), tile_e, used, tile_valid, pad_start,
            pad_len, n_tiles * MOE_ROWS)


def kernel(x, a_norm, a_proj, a_scale, kv_norm, w_kv, b_norm, w_q, w_o, rel_bias, ffn_norm,
           dense_w_gate, dense_w_up, dense_w_down, moe_router, moe_w_gate, moe_w_up,
           moe_w_down, final_norm):
    B, S, D = x.shape
    T = B * S
    E, _, F = moe_w_gate[0].shape
    h2, (wg16, wq16, wkv16, wo16) = _layer0(
        x.reshape(T, D), a_norm[0], a_proj[0], a_scale[0], ffn_norm[0],
        dense_w_gate[0], dense_w_up[0], dense_w_down[0],
        (moe_w_gate[0].reshape(E * D, F), w_q[0], w_kv, w_o[0]))
    qkv, wu16 = _qkv_project(h2, kv_norm, b_norm[0], wq16, wkv16, B, S,
                             moe_w_up[0].reshape(E * D, F))
    outs, stats = [], []
    for g, (q, k, v) in enumerate(qkv):
        o, st = _attention_group(q, k, v, rel_bias, g, B, S)
        outs.append(o)
        stats.append(st)
    h3, hn3, route, route_t, tinfo, counts = _merge_layer(outs, stats, h2, wo16, ffn_norm[1],
                                                          moe_router[0])
    (run_start, run_cnt, tile_e, used, tile_valid, pad_start, pad_len,
     n_slots) = _routing_tables(tinfo, counts, T)
    xs, wd16 = _dispatch(run_start, run_cnt, pad_start, pad_len, used, hn3, route_t, n_slots,
                         moe_w_down[0].reshape(E * F, D))
    ys = _expert_ffn(tile_e, used, tile_valid, xs, wg16.reshape(E, D, F), wu16.reshape(E, D, F),
                     wd16.reshape(E, F, D))
    out = _combine(run_start, run_cnt, h3, route, final_norm, ys)
    return out.reshape(B, S, D)
```

```python
import functools
import math

import jax
import jax.numpy as jnp
from jax import lax
from jax.experimental import pallas as pl
from jax.experimental.pallas import tpu as pltpu

D_MODEL = 1024
EPS = 1e-6
POOL_WINDOWS = (2, 4, 8, 16)
POOL_GROUP_DIM = D_MODEL // len(POOL_WINDOWS)
MAX_POOL_WINDOW = max(POOL_WINDOWS)
HEAD_DIM = 64
HEADS_PER_GROUP = D_MODEL // HEAD_DIM
DILATED_PATTERNS = ((128, 1), (512, 4), (2048, 16))
N_ATT_GROUPS = len(DILATED_PATTERNS)
QBLK = 128
NEG_INF = -1e30
LOG2E = 1.4426950408889634
Q_SCALE = HEAD_DIM ** -0.5 * LOG2E
N_BUCKETS = 32
MAX_EXACT = N_BUCKETS // 2
MAX_DISTANCE = 2048
N_EXPERTS = 8
TOP_K = 2

LANES = 128
SUBLANES = 8
N_SLABS = D_MODEL // LANES
HEAD_PAIR = 2 * HEAD_DIM
assert HEAD_PAIR == LANES

LAYER0_ROWS = 512
QKV_ROWS = 512
ATTN_BLOCKS_PER_STEP = 8
MERGE_ROWS = 512
MOE_ROWS = 1024
MOE_FT = 1792
MOE_CHUNK = 256
MOE_SUBROWS = 128
MOE_NT = 1024
RUN_ALIGN = 16
STAGE_ROWS = -(-(TOP_K * MERGE_ROWS + N_EXPERTS * (RUN_ALIGN - 1)) // LANES) * LANES
VMEM_LIMIT = 60 * 1024 * 1024

R_E0, R_E1, R_RANK0, R_RANK1, R_G0, R_G1 = 0, 1, 2, 3, 4, 5
T_BASE, T_COUNT = 0, 1


def _rms_scale(x):
    return lax.rsqrt(jnp.mean(x * x, axis=-1, keepdims=True) + EPS)


def _silu(g):
    return g * (1.0 / (1.0 + jnp.exp(-g)))


def _split_bf16(x):
    hi = x.astype(jnp.bfloat16)
    lo = (x - hi.astype(jnp.float32)).astype(jnp.bfloat16)
    return hi, lo


def _pack_bf16_pair(a, b):
    hi = lax.bitcast_convert_type(a.astype(jnp.bfloat16).astype(jnp.float32), jnp.uint32)
    lo = lax.bitcast_convert_type(b.astype(jnp.bfloat16).astype(jnp.float32), jnp.uint32)
    return hi | (lo >> 16)


def _unpack_bf16_pair(w):
    return (lax.bitcast_convert_type(w & jnp.uint32(0xFFFF0000), jnp.float32),
            lax.bitcast_convert_type(w << 16, jnp.float32))


def _side_cast_specs(side, n_steps):
    rows = side.shape[0] // n_steps
    spec = pl.BlockSpec((rows, side.shape[1]), lambda i, *_: (i, 0))
    return spec, jax.ShapeDtypeStruct(side.shape, jnp.bfloat16)


def _pool_mix(x, halo, seq_tile, gain, proj_ref, scale):
    xn = x * _rms_scale(x) * gain
    hn_halo = halo * _rms_scale(halo) * gain
    hn_halo = jnp.where(seq_tile > 0, hn_halo, 0.0)
    full = jnp.concatenate([hn_halo, xn], axis=0)
    ts = x.shape[0]
    pos = seq_tile * ts + lax.broadcasted_iota(jnp.int32, (ts, 1), 0)
    outs = []
    for g, w in enumerate(POOL_WINDOWS):
        c0 = g * POOL_GROUP_DIM
        s = full[:, c0:c0 + POOL_GROUP_DIM]
        span = 1
        while span < w:
            s = s + pltpu.roll(s, span, axis=0)
            span *= 2
        s = s[MAX_POOL_WINDOW:, :]
        cnt = jnp.minimum(pos + 1, w).astype(jnp.float32)
        pooled = s / cnt - xn[:, c0:c0 + POOL_GROUP_DIM]
        outs.append(jnp.dot(pooled.astype(jnp.bfloat16), proj_ref[g],
                            preferred_element_type=jnp.float32))
    return x + jnp.concatenate(outs, axis=1) * scale


def _layer0_kernel(*refs, tiles_per_seq, n_side):
    (x_ref, halo_ref, an_ref, proj_ref, asc_ref, fn_ref, wg_ref, wu_ref, wd_ref) = refs[:9]
    side_in = refs[9:9 + n_side]
    h_out_ref = refs[9 + n_side]
    side_out = refs[10 + n_side:]
    for src, dst in zip(side_in, side_out):
        dst[...] = src[...].astype(dst.dtype)
    seq_tile = lax.rem(pl.program_id(0), tiles_per_seq)
    h = _pool_mix(x_ref[...], halo_ref[...], seq_tile, an_ref[...], proj_ref, asc_ref[...])
    hn = (h * _rms_scale(h) * fn_ref[...]).astype(jnp.bfloat16)
    g = jnp.dot(hn, wg_ref[...], preferred_element_type=jnp.float32)
    u = jnp.dot(hn, wu_ref[...], preferred_element_type=jnp.float32)
    hid = (_silu(g) * u).astype(jnp.bfloat16)
    h_out_ref[...] = h + jnp.dot(hid, wd_ref[...], preferred_element_type=jnp.float32)


def _layer0(x, seq_len, a_norm, a_proj, a_scale, ffn_norm, w_gate, w_up, w_down, sides):
    T, D = x.shape
    F = w_gate.shape[1]
    ts = LAYER0_ROWS
    n_steps = T // ts
    halo_blocks = ts // MAX_POOL_WINDOW
    row = lambda i: (i, 0)
    vec = pl.BlockSpec((1, D), lambda i: (0, 0))
    resident = dict(index_map=lambda i: (0, 0), pipeline_mode=pl.Buffered(1))
    side_specs, side_shapes = zip(*[_side_cast_specs(a, n_steps) for a in sides])
    outs = pl.pallas_call(
        functools.partial(_layer0_kernel, tiles_per_seq=seq_len // ts, n_side=len(sides)),
        grid=(n_steps,),
        in_specs=[
            pl.BlockSpec((ts, D), row),
            pl.BlockSpec((MAX_POOL_WINDOW, D), lambda i: (jnp.maximum(i * halo_blocks - 1, 0), 0)),
            vec,
            pl.BlockSpec(a_proj.shape, lambda i: (0, 0, 0), pipeline_mode=pl.Buffered(1)),
            vec, vec,
            pl.BlockSpec((D, F), **resident),
            pl.BlockSpec((D, F), **resident),
            pl.BlockSpec((F, D), **resident),
            *side_specs,
        ],
        out_specs=[pl.BlockSpec((ts, D), row), *side_specs],
        out_shape=[jax.ShapeDtypeStruct((T, D), jnp.float32), *side_shapes],
        compiler_params=pltpu.CompilerParams(
            dimension_semantics=("arbitrary",), vmem_limit_bytes=VMEM_LIMIT),
        name="pool_mixer_dense_swiglu",
    )(x, x, a_norm.reshape(1, D), a_proj.astype(jnp.bfloat16), a_scale.reshape(1, D),
      ffn_norm.reshape(1, D), w_gate.astype(jnp.bfloat16), w_up.astype(jnp.bfloat16),
      w_down.astype(jnp.bfloat16), *sides)
    return outs[0], outs[1:]


def _qkv_kernel(*refs):
    slabs = refs[:N_SLABS]
    kvg_ref, qg_ref, wq_ref, wkv_ref, side_in_ref = refs[N_SLABS:N_SLABS + 5]
    out_refs = refs[N_SLABS + 5:-1]
    side_out_ref = refs[-1]
    side_out_ref[...] = side_in_ref[...].astype(side_out_ref.dtype)
    tm = slabs[0].shape[0]
    for g, (_, dilation) in enumerate(DILATED_PATTERNS):
        n = tm // dilation
        cols = []
        for slab in slabs:
            if dilation == 1:
                cols.append(slab[...])
            else:
                cols.append(jnp.concatenate(
                    [slab[pl.ds(r, n, stride=dilation), :] for r in range(dilation)], axis=0))
        x = jnp.concatenate(cols, axis=1)
        xs = x * _rms_scale(x)
        xq = (xs * qg_ref[...]).astype(jnp.bfloat16)
        xkv = (xs * kvg_ref[...]).astype(jnp.bfloat16)
        D = D_MODEL
        kcols = slice(g * D, (g + 1) * D)
        vcols = slice((N_ATT_GROUPS + g) * D, (N_ATT_GROUPS + g + 1) * D)
        q = jnp.dot(xq, wq_ref[:, kcols], preferred_element_type=jnp.float32) * Q_SCALE
        k = jnp.dot(xkv, wkv_ref[:, kcols], preferred_element_type=jnp.float32)
        v = jnp.dot(xkv, wkv_ref[:, vcols], preferred_element_type=jnp.float32)
        for out_ref, val in zip(out_refs[3 * g:3 * g + 3], (q, k, v)):
            for r in range(dilation):
                out_ref[r] = val[r * n:(r + 1) * n].astype(out_ref.dtype)


def _qkv_project(h, kv_norm, b_norm, w_q, w_kv, B, S, side):
    T, D = h.shape
    tm = QKV_ROWS
    tiles_per_seq = S // tm
    resident = dict(index_map=lambda i: (0, 0), pipeline_mode=pl.Buffered(1))
    slab_specs = [pl.BlockSpec((tm, LANES), functools.partial(lambda i, c: (i, c), c=c))
                  for c in range(N_SLABS)]
    vec = pl.BlockSpec((1, D), lambda i: (0, 0))
    out_specs, out_shapes = [], []
    for _, d in DILATED_PATTERNS:
        spec = pl.BlockSpec((None, d, tm // d, D),
                            lambda i: (i // tiles_per_seq, 0, i % tiles_per_seq, 0))
        out_specs += [spec] * 3
        out_shapes += [jax.ShapeDtypeStruct((B, d, S // d, D), jnp.bfloat16)] * 3
    side_spec, side_shape = _side_cast_specs(side, T // tm)
    outs = pl.pallas_call(
        _qkv_kernel,
        grid=(T // tm,),
        in_specs=slab_specs + [vec, vec, pl.BlockSpec(w_q.shape, **resident),
                               pl.BlockSpec(w_kv.shape, **resident), side_spec],
        out_specs=out_specs + [side_spec],
        out_shape=out_shapes + [side_shape],
        compiler_params=pltpu.CompilerParams(
            dimension_semantics=("arbitrary",), vmem_limit_bytes=VMEM_LIMIT),
        name="qkv_proj",
    )(*([h] * N_SLABS), kv_norm.reshape(1, D), b_norm.reshape(1, D), w_q, w_kv, side)
    return [tuple(outs[3 * g:3 * g + 3]) for g in range(N_ATT_GROUPS)], outs[-1]


def _attn_kernel(*refs, dilation, has_prev, jb, rb):
    if has_prev:
        (table_ref, bmap_ref, q_ref, kp_ref, kc_ref, vp_ref, vc_ref,
         o_ref, stat_ref, bias_ref, stat_scr) = refs
    else:
        table_ref, bmap_ref, q_ref, kc_ref, vc_ref, o_ref, stat_ref, bias_ref, stat_scr = refs
    b, jstep, rstep = pl.program_id(0), pl.program_id(1), pl.program_id(2)
    n_keys = bias_ref.shape[2]

    @pl.when(jnp.logical_and(jnp.logical_and(b == 0, jstep == 0), rstep == 0))
    def _():
        stat_scr[...] = jnp.zeros_like(stat_scr)
        bmap = bmap_ref[...]
        in_prev = lax.broadcasted_iota(jnp.int32, bmap.shape, 1) < n_keys - QBLK

        def head(h, carry):
            acc = jnp.full(bmap.shape, NEG_INF, jnp.float32)
            for bucket in range(N_BUCKETS):
                acc = jnp.where(bmap == bucket, table_ref[bucket, h], acc)
            rows_h = pl.ds(pl.multiple_of(h * QBLK, QBLK), QBLK)
            bias_ref[0, rows_h, :] = jnp.where(in_prev, NEG_INF, acc)
            bias_ref[1, rows_h, :] = acc
            return carry

        lax.fori_loop(0, HEADS_PER_GROUP, head, 0)

    lane = lax.broadcasted_iota(jnp.int32, (QBLK, LANES), 1)
    first_head = lane < HEAD_DIM
    for rr in range(rb):
        for jj in range(jb):
            qrows = slice(jj * QBLK, (jj + 1) * QBLK)
            bias_copy = jnp.minimum(jstep, 1) if jj == 0 else 1
            if dilation > 1:
                out_rows = pl.ds(dilation * QBLK * jj + rstep * rb + rr, QBLK, stride=dilation)
            else:
                out_rows = qrows
            for hp in range(HEADS_PER_GROUP // 2):
                sl = slice(hp * HEAD_PAIR, (hp + 1) * HEAD_PAIR)
                qp = q_ref[rr, qrows, sl]
                zero = jnp.zeros_like(qp)
                q2 = jnp.concatenate([jnp.where(first_head, qp, zero),
                                      jnp.where(first_head, zero, qp)], axis=0)
                if has_prev:
                    if jj == 0:
                        k_prev, v_prev = kp_ref[rr, :, sl], vp_ref[rr, :, sl]
                    else:
                        prows = slice((jj - 1) * QBLK, jj * QBLK)
                        k_prev, v_prev = kc_ref[rr, prows, sl], vc_ref[rr, prows, sl]
                    kcat = jnp.concatenate([k_prev, kc_ref[rr, qrows, sl]], axis=0)
                    vcat = jnp.concatenate([v_prev, vc_ref[rr, qrows, sl]], axis=0)
                else:
                    kcat, vcat = kc_ref[rr, qrows, sl], vc_ref[rr, qrows, sl]
                s = lax.dot_general(q2, kcat, (((1,), (1,)), ((), ())),
                                    preferred_element_type=jnp.float32)
                s = s + bias_ref[bias_copy, hp * 2 * QBLK:(hp + 1) * 2 * QBLK, :]
                m = jnp.max(s, axis=1, keepdims=True)
                p = jnp.exp2(s - m)
                l = jnp.sum(p, axis=1, keepdims=True)
                o2 = jnp.dot(p.astype(jnp.bfloat16), vcat, preferred_element_type=jnp.float32)
                o_pair = jnp.where(first_head, o2[:QBLK], o2[QBLK:])
                if hp % 2 == 0:
                    o_even = o_pair
                else:
                    o_ref[hp // 2, out_rows, :] = _pack_bf16_pair(o_even, o_pair)
                for half, head in ((slice(0, QBLK), 2 * hp), (slice(QBLK, 2 * QBLK), 2 * hp + 1)):
                    stat_scr[:, head:head + 1] = m[half]
                    stat_scr[:, HEADS_PER_GROUP + head:HEADS_PER_GROUP + head + 1] = l[half]
            stat_ref[out_rows, :] = stat_scr[...]


def _bucket_map(dilation, window, n_keys):
    W = window // dilation
    a = jnp.arange(QBLK, dtype=jnp.int32)[:, None]
    c = jnp.arange(n_keys, dtype=jnp.int32)[None, :]
    m = a + (n_keys - QBLK) - c
    band = (m >= 0) & (m <= W)
    n = jnp.maximum(m, 0) * dilation
    nf = jnp.maximum(n, 1).astype(jnp.float32)
    large = MAX_EXACT + (jnp.log(nf / MAX_EXACT) / math.log(MAX_DISTANCE / MAX_EXACT)
                         * (N_BUCKETS - MAX_EXACT)).astype(jnp.int32)
    large = jnp.minimum(large, N_BUCKETS - 1)
    bucket = jnp.where(n < MAX_EXACT, n, large)
    return jnp.where(band, bucket, -1)


def _attention_group(q, k, v, rel_bias, g, B, S):
    window, d = DILATED_PATTERNS[g]
    L = S // d
    n = L // QBLK
    D = D_MODEL
    has_prev = n > 1
    n_keys = 2 * QBLK if has_prev else QBLK
    table = rel_bias[:, g * HEADS_PER_GROUP:(g + 1) * HEADS_PER_GROUP].astype(jnp.float32) * LOG2E
    bmap = _bucket_map(d, window, n_keys)
    jb = min(ATTN_BLOCKS_PER_STEP, n)
    rb = ATTN_BLOCKS_PER_STEP // jb
    cur_blk = (None, rb, jb * QBLK, D)
    prev_blk = (None, rb, QBLK, D)
    cur = lambda b, j, r: (b, r, j, 0)
    prev = lambda b, j, r: (b, r, jnp.maximum(j * jb - 1, 0), 0)
    if has_prev:
        kv_specs = [pl.BlockSpec(prev_blk, prev), pl.BlockSpec(cur_blk, cur),
                    pl.BlockSpec(prev_blk, prev), pl.BlockSpec(cur_blk, cur)]
        kv_args = (k, k, v, v)
    else:
        kv_specs = [pl.BlockSpec(cur_blk, cur), pl.BlockSpec(cur_blk, cur)]
        kv_args = (k, v)
    n_j = n // jb
    span = QBLK * jb * d
    return pl.pallas_call(
        functools.partial(_attn_kernel, dilation=d, has_prev=has_prev, jb=jb, rb=rb),
        grid=(B, n_j, d // rb),
        in_specs=[pl.BlockSpec(memory_space=pltpu.SMEM),
                  pl.BlockSpec(bmap.shape, lambda b, j, r: (0, 0)),
                  pl.BlockSpec(cur_blk, cur)] + kv_specs,
        out_specs=[pl.BlockSpec((N_SLABS // 2, span, LANES), lambda b, j, r: (0, b * n_j + j, 0)),
                   pl.BlockSpec((span, LANES), lambda b, j, r: (b * n_j + j, 0))],
        out_shape=[jax.ShapeDtypeStruct((N_SLABS // 2, B * S, LANES), jnp.uint32),
                   jax.ShapeDtypeStruct((B * S, LANES), jnp.float32)],
        scratch_shapes=[pltpu.VMEM((2, HEADS_PER_GROUP * QBLK, n_keys), jnp.float32),
                        pltpu.VMEM((QBLK, LANES), jnp.float32)],
        compiler_params=pltpu.CompilerParams(
            dimension_semantics=("arbitrary",) * 3, vmem_limit_bytes=VMEM_LIMIT),
        name=f"dilated_attn_g{g}",
    )(table, bmap, q, *kv_args)


def _merge_kernel(o0_ref, o1_ref, o2_ref, s0_ref, s1_ref, s2_ref, h_ref, wo_ref, fn_ref,
                  r2_ref, expand_ref,
                  h_out_ref, hn_ref, route_ref, route_t_ref, tinfo_ref, counts_ref, carry_ref):
    i = pl.program_id(0)
    tm = h_ref.shape[0]

    @pl.when(i == 0)
    def _():
        carry_ref[...] = jnp.zeros_like(carry_ref)

    lane = lax.broadcasted_iota(jnp.int32, (tm, LANES), 1)
    head_lane = lane < HEADS_PER_GROUP
    stats = [s0_ref[...], s1_ref[...], s2_ref[...]]
    dens = [pltpu.roll(st, LANES - HEADS_PER_GROUP, axis=1) for st in stats]
    mx = jnp.maximum(jnp.maximum(stats[0], stats[1]), stats[2])
    es = [jnp.exp2(st - mx) for st in stats]
    inv = 1.0 / (es[0] * dens[0] + es[1] * dens[1] + es[2] * dens[2])
    merged = None
    for e, o_ref in zip(es, (o0_ref, o1_ref, o2_ref)):
        w = jnp.where(head_lane, e * inv, 0.0)
        hi = w.astype(jnp.bfloat16).astype(jnp.float32)
        packed = (hi + pltpu.roll(w - hi, HEADS_PER_GROUP, axis=1)).astype(jnp.bfloat16)
        wide = jnp.dot(packed, expand_ref[...], preferred_element_type=jnp.float32)
        o = jnp.concatenate([part for c in range(N_SLABS // 2)
                             for part in _unpack_bf16_pair(o_ref[c])], axis=1)
        term = wide * o
        merged = term if merged is None else merged + term
    h = h_ref[...] + jnp.dot(merged.astype(jnp.bfloat16), wo_ref[...],
                             preferred_element_type=jnp.float32)
    h_out_ref[...] = h
    hn = h * _rms_scale(h) * fn_ref[...]
    hn_ref[...] = hn.astype(hn_ref.dtype)

    hi, lo = _split_bf16(hn)
    both = jnp.dot(hi, r2_ref[...], preferred_element_type=jnp.float32)
    logits = (both[:, :LANES] + both[:, LANES:]
              + jnp.dot(lo, r2_ref[:, :LANES], preferred_element_type=jnp.float32))
    logits = jnp.where(lane < N_EXPERTS, logits, -jnp.inf)
    v0 = jnp.max(logits, axis=1, keepdims=True)
    e0 = jnp.min(jnp.where(logits == v0, lane, LANES), axis=1, keepdims=True)
    rest = jnp.where(lane == e0, -jnp.inf, logits)
    v1 = jnp.max(rest, axis=1, keepdims=True)
    e1 = jnp.min(jnp.where(rest == v1, lane, LANES), axis=1, keepdims=True)
    t = jnp.exp(v1 - v0)
    g0 = 1.0 / (1.0 + t)
    g1 = t / (1.0 + t)

    hit0 = lane == e0
    hit1 = lane == e1
    onehot = jnp.where(jnp.logical_or(hit0, hit1), 1.0, 0.0)
    r_i = lax.broadcasted_iota(jnp.int32, (tm, tm), 0)
    c_i = lax.broadcasted_iota(jnp.int32, (tm, tm), 1)
    lower = jnp.where(c_i < r_i, 1.0, 0.0).astype(jnp.bfloat16)
    before = jnp.dot(lower, onehot.astype(jnp.bfloat16), preferred_element_type=jnp.float32)
    rank0 = jnp.sum(jnp.where(hit0, before, 0.0), axis=1, keepdims=True)
    rank1 = jnp.sum(jnp.where(hit1, before, 0.0), axis=1, keepdims=True)
    tile_counts = jnp.sum(onehot, axis=0, keepdims=True)
    base = carry_ref[0:1, :]
    total = base + jnp.floor((tile_counts + (RUN_ALIGN - 1)) * (1.0 / RUN_ALIGN)) * RUN_ALIGN
    carry_ref[...] = jnp.broadcast_to(total, carry_ref.shape)
    counts_ref[...] = jnp.broadcast_to(total, counts_ref.shape)
    row_id = lax.broadcasted_iota(jnp.int32, tinfo_ref.shape, 0)
    tinfo_ref[...] = jnp.where(row_id == T_BASE, base, jnp.where(row_id == T_COUNT, tile_counts, 0.0))

    route = jnp.zeros((tm, LANES), jnp.float32)
    for ln, val in ((R_E0, e0.astype(jnp.float32)), (R_E1, e1.astype(jnp.float32)),
                    (R_RANK0, rank0), (R_RANK1, rank1), (R_G0, g0), (R_G1, g1)):
        route = jnp.where(lane == ln, val, route)
    route_ref[...] = route
    route_t_ref[...] = route.T[:SUBLANES, :]


def _merge_layer(outs, lses, h, w_o, ffn_norm, w_router):
    T, D = h.shape
    tm = MERGE_ROWS
    r_pad = jnp.zeros((D, LANES), jnp.float32).at[:, :N_EXPERTS].set(w_router)
    r2 = jnp.concatenate(_split_bf16(r_pad), axis=1)
    head_of_col = jnp.arange(D, dtype=jnp.int32) // HEAD_DIM
    lane_id = jnp.arange(LANES, dtype=jnp.int32)[:, None]
    expand = ((lane_id < 2 * HEADS_PER_GROUP)
              & (lane_id % HEADS_PER_GROUP == head_of_col[None, :])).astype(jnp.bfloat16)
    row = lambda i: (i, 0)
    const = lambda i: (0, 0)
    wide = pl.BlockSpec((tm, D), row)
    narrow = pl.BlockSpec((tm, LANES), row)
    slabs = pl.BlockSpec((N_SLABS // 2, tm, LANES), lambda i: (0, i, 0))
    return pl.pallas_call(
        _merge_kernel,
        grid=(T // tm,),
        in_specs=[slabs, slabs, slabs, narrow, narrow, narrow, wide,
                  pl.BlockSpec((D, D), const), pl.BlockSpec((1, D), const),
                  pl.BlockSpec((D, 2 * LANES), const), pl.BlockSpec((LANES, D), const)],
        out_specs=[wide, wide, narrow,
                   pl.BlockSpec((None, SUBLANES, tm), lambda i: (i, 0, 0)),
                   pl.BlockSpec((None, SUBLANES, LANES), lambda i: (i, 0, 0)),
                   pl.BlockSpec((8, LANES), const)],
        out_shape=[jax.ShapeDtypeStruct((T, D), jnp.float32),
                   jax.ShapeDtypeStruct((T, D), jnp.bfloat16),
                   jax.ShapeDtypeStruct((T, LANES), jnp.float32),
                   jax.ShapeDtypeStruct((T // tm, SUBLANES, tm), jnp.float32),
                   jax.ShapeDtypeStruct((T // tm, SUBLANES, LANES), jnp.float32),
                   jax.ShapeDtypeStruct((8, LANES), jnp.float32)],
        scratch_shapes=[pltpu.VMEM((8, LANES), jnp.float32)],
        compiler_params=pltpu.CompilerParams(
            dimension_semantics=("arbitrary",), vmem_limit_bytes=VMEM_LIMIT),
        name="merge_outproj_router",
    )(*outs, *lses, h, w_o, ffn_norm.reshape(1, D), r2, expand)


def _run_offsets(cnt_ref, tile):
    sizes, offs, off = [], [], 0
    for e in range(N_EXPERTS):
        size = ((cnt_ref[tile * N_EXPERTS + e] + (RUN_ALIGN - 1)) // RUN_ALIGN) * RUN_ALIGN
        sizes.append(size)
        offs.append(off)
        off = off + size
    return sizes, offs


def _for_each_run_piece(start_ref, cnt_ref, tile, fn):
    sizes, offs = _run_offsets(cnt_ref, tile)
    for e in range(N_EXPERTS):
        slot0 = start_ref[tile * N_EXPERTS + e]
        bit = MERGE_ROWS
        while bit >= RUN_ALIGN:
            done = sizes[e] & -(2 * bit)

            @pl.when((sizes[e] & bit) != 0)
            def _():
                fn(pl.multiple_of(offs[e] + done, RUN_ALIGN),
                   pl.multiple_of(slot0 + done, RUN_ALIGN), bit)
            bit //= 2


def _staging_columns(expert, rank, offs):
    col = rank
    for e in range(N_EXPERTS):
        col = col + jnp.where(expert == e, offs[e], 0)
    return col


def _dispatch_kernel(start_ref, cnt_ref, pad_start_ref, pad_len_ref, used_ref,
                     hn_ref, route_t_ref, side_in_ref, xs_ref, side_out_ref,
                     xbuf, zbuf, sems, zsem):
    side_out_ref[...] = side_in_ref[...].astype(side_out_ref.dtype)
    i = pl.program_id(0)
    n_steps = pl.num_programs(0)
    slot = lax.rem(i, 2)
    n_tiles = xs_ref.shape[0] // MOE_ROWS

    def zero_fill(act):
        for e in range(N_EXPERTS):
            run_start = pl.multiple_of(pad_start_ref[e], RUN_ALIGN)
            run_len = pad_len_ref[e]
            bit = MOE_ROWS // 2
            while bit >= RUN_ALIGN:
                off = pl.multiple_of(run_start + (run_len & -(2 * bit)), RUN_ALIGN)

                @pl.when((run_len & bit) != 0)
                def _():
                    act(pltpu.make_async_copy(zbuf.at[pl.ds(0, bit), :],
                                              xs_ref.at[pl.ds(off, bit), :], zsem))
                bit //= 2
        for t in range(n_tiles):
            @pl.when(t >= used_ref[0])
            def _():
                act(pltpu.make_async_copy(zbuf, xs_ref.at[pl.ds(t * MOE_ROWS, MOE_ROWS), :], zsem))

    def run_copies(tile, buf_slot, act):
        def piece(buf_row, slot_row, rows):
            act(pltpu.make_async_copy(xbuf.at[buf_slot, pl.ds(buf_row, rows), :],
                                      xs_ref.at[pl.ds(slot_row, rows), :], sems.at[buf_slot]))
        _for_each_run_piece(start_ref, cnt_ref, tile, piece)

    @pl.when(i == 0)
    def _():
        zbuf[...] = jnp.zeros_like(zbuf)
        zero_fill(lambda c: c.start())

    @pl.when(i >= 2)
    def _():
        run_copies(i - 2, slot, lambda c: c.wait())

    _, offs = _run_offsets(cnt_ref, i)
    route_t = route_t_ref[...]
    cols = [_staging_columns(route_t[R_E0 + k:R_E0 + k + 1, :].astype(jnp.int32),
                             route_t[R_RANK0 + k:R_RANK0 + k + 1, :].astype(jnp.int32), offs)
            for k in range(TOP_K)]
    buf_row = lax.broadcasted_iota(jnp.int32, (STAGE_ROWS, hn_ref.shape[0]), 0)
    onehot = jnp.where(jnp.logical_or(buf_row == cols[0], buf_row == cols[1]), 1.0, 0.0)
    xbuf[slot] = jnp.dot(onehot.astype(jnp.bfloat16), hn_ref[...],
                         preferred_element_type=jnp.float32).astype(xbuf.dtype)
    run_copies(i, slot, lambda c: c.start())

    @pl.when(i == n_steps - 1)
    def _():
        @pl.when(i >= 1)
        def _():
            run_copies(i - 1, 1 - slot, lambda c: c.wait())
        run_copies(i, slot, lambda c: c.wait())
        zero_fill(lambda c: c.wait())


def _dispatch(run_start, run_cnt, pad_start, pad_len, used, hn, route_t, n_slots, side):
    T, D = hn.shape
    tm = MERGE_ROWS
    side_spec, side_shape = _side_cast_specs(side, T // tm)
    return pl.pallas_call(
        _dispatch_kernel,
        grid_spec=pltpu.PrefetchScalarGridSpec(
            num_scalar_prefetch=5,
            grid=(T // tm,),
            in_specs=[pl.BlockSpec((tm, D), lambda i, *_: (i, 0)),
                      pl.BlockSpec((None, SUBLANES, tm), lambda i, *_: (i, 0, 0)),
                      side_spec],
            out_specs=[pl.BlockSpec(memory_space=pl.ANY), side_spec],
            scratch_shapes=[pltpu.VMEM((2, STAGE_ROWS, D), jnp.bfloat16),
                            pltpu.VMEM((MOE_ROWS, D), jnp.bfloat16),
                            pltpu.SemaphoreType.DMA((2,)), pltpu.SemaphoreType.DMA(())],
        ),
        out_shape=[jax.ShapeDtypeStruct((n_slots, D), jnp.bfloat16), side_shape],
        compiler_params=pltpu.CompilerParams(
            dimension_semantics=("arbitrary",), vmem_limit_bytes=VMEM_LIMIT),
        name="moe_dispatch",
    )(run_start, run_cnt, pad_start, pad_len, used, hn, route_t, side)


def _expert_kernel(tile_e_ref, used_ref, valid_ref, x_ref, wg_ref, wu_ref, wd_ref, y_ref,
                   hid_ref):
    del tile_e_ref
    i = pl.program_id(0)
    s = pl.program_id(1)
    n_f, tm, tf = hid_ref.shape
    active = i < used_ref[0]
    n_valid = valid_ref[jnp.minimum(i, used_ref[0] - 1)]
    full = n_valid == tm
    gate_step = jnp.logical_and(active, s < n_f)
    down_step = jnp.logical_and(active, s >= n_f)

    def gate_up(rows):
        x = x_ref[rows, :]
        for c0 in range(0, tf, MOE_CHUNK):
            cols = slice(c0, min(c0 + MOE_CHUNK, tf))
            g = jnp.dot(x, wg_ref[:, cols], preferred_element_type=jnp.float32)
            u = jnp.dot(x, wu_ref[:, cols], preferred_element_type=jnp.float32)
            hid_ref[s, rows, cols] = (_silu(g) * u).astype(hid_ref.dtype)

    def down(rows):
        hid = jnp.concatenate([hid_ref[c, rows, :] for c in range(n_f)], axis=1)
        y_ref[rows, :] = jnp.dot(hid, wd_ref[...],
                                 preferred_element_type=jnp.float32).astype(y_ref.dtype)

    @pl.when(jnp.logical_and(gate_step, full))
    def _():
        gate_up(slice(None))

    @pl.when(jnp.logical_and(down_step, full))
    def _():
        down(slice(None))

    for r0 in range(0, tm, MOE_SUBROWS):
        rows = slice(r0, r0 + MOE_SUBROWS)
        occupied = jnp.logical_and(jnp.logical_not(full), r0 < n_valid)

        @pl.when(jnp.logical_and(gate_step, occupied))
        def _():
            gate_up(rows)

        @pl.when(jnp.logical_and(down_step, occupied))
        def _():
            down(rows)

        @pl.when(jnp.logical_and(down_step, r0 >= n_valid))
        def _():
            y_ref[rows, :] = jnp.zeros((MOE_SUBROWS, y_ref.shape[1]), y_ref.dtype)

    @pl.when(jnp.logical_and(jnp.logical_not(active), s >= n_f))
    def _():
        y_ref[...] = jnp.zeros_like(y_ref)


def _expert_ffn(tile_e, used, tile_valid, xs, w_gate, w_up, w_down):
    n_slots, D = xs.shape
    F = w_gate.shape[2]
    tm, tf, tn = MOE_ROWS, MOE_FT, MOE_NT
    n_f, n_n = F // tf, D // tn
    n_tiles = n_slots // tm
    last = n_f + n_n - 1

    def eff(i, s, used_ref):
        idle = i >= used_ref[0]
        return jnp.where(idle, used_ref[0] - 1, i), jnp.where(idle, last, s)

    def x_map(i, s, te, us, nv):
        ie, se = eff(i, s, us)
        return jnp.minimum(ie + jnp.where(se >= n_f, 1, 0), us[0] - 1), 0

    def gate_map(i, s, te, us, nv):
        ie, se = eff(i, s, us)
        return te[ie], 0, jnp.minimum(se, n_f - 1)

    def down_map(i, s, te, us, nv):
        ie, se = eff(i, s, us)
        tile = jnp.where(se == 0, jnp.maximum(ie - 1, 0), ie)
        return te[tile], 0, jnp.where(se == 0, n_n - 1, jnp.clip(se - n_f, 0, n_n - 1))

    def out_map(i, s, te, us, nv):
        return i, jnp.clip(s - n_f, 0, n_n - 1)

    return pl.pallas_call(
        _expert_kernel,
        grid_spec=pltpu.PrefetchScalarGridSpec(
            num_scalar_prefetch=3,
            grid=(n_tiles, n_f + n_n),
            in_specs=[
                pl.BlockSpec((tm, D), x_map),
                pl.BlockSpec((None, D, tf), gate_map),
                pl.BlockSpec((None, D, tf), gate_map),
                pl.BlockSpec((None, F, tn), down_map),
            ],
            out_specs=pl.BlockSpec((tm, tn), out_map),
            scratch_shapes=[pltpu.VMEM((n_f, tm, tf), jnp.bfloat16)],
        ),
        out_shape=jax.ShapeDtypeStruct((n_slots, D), jnp.bfloat16),
        compiler_params=pltpu.CompilerParams(
            dimension_semantics=("arbitrary", "arbitrary"), vmem_limit_bytes=VMEM_LIMIT),
        name="moe_experts",
    )(tile_e, used, tile_valid, xs, w_gate, w_up, w_down)


def _combine_kernel(start_ref, cnt_ref, h_ref, route_ref, fin_ref, ys_ref, o_ref, ybuf, sems):
    i = pl.program_id(0)
    n_steps = pl.num_programs(0)
    tile = h_ref.shape[0]
    slot = lax.rem(i, 2)

    def run_copies(step, buf_slot, act):
        def piece(buf_row, slot_row, rows):
            act(pltpu.make_async_copy(ys_ref.at[pl.ds(slot_row, rows), :],
                                      ybuf.at[buf_slot, pl.ds(buf_row, rows), :],
                                      sems.at[buf_slot]))
        _for_each_run_piece(start_ref, cnt_ref, step, piece)

    @pl.when(i == 0)
    def _():
        ybuf[...] = jnp.zeros_like(ybuf)
        run_copies(0, 0, lambda c: c.start())

    @pl.when(i + 1 < n_steps)
    def _():
        run_copies(i + 1, 1 - slot, lambda c: c.start())

    run_copies(i, slot, lambda c: c.wait())
    _, offs = _run_offsets(cnt_ref, i)
    route = route_ref[...]
    staged = ybuf[slot]
    buf_col = lax.broadcasted_iota(jnp.int32, (tile, STAGE_ROWS), 1)
    ys = []
    for k in range(TOP_K):
        col = _staging_columns(route[:, R_E0 + k:R_E0 + k + 1].astype(jnp.int32),
                               route[:, R_RANK0 + k:R_RANK0 + k + 1].astype(jnp.int32), offs)
        onehot = jnp.where(buf_col == col, 1.0, 0.0).astype(jnp.bfloat16)
        ys.append(jnp.dot(onehot, staged, preferred_element_type=jnp.float32))
    g0 = route[:, R_G0:R_G0 + 1]
    g1 = route[:, R_G1:R_G1 + 1]
    h = h_ref[...] + (g0 * ys[0] + g1 * ys[1])
    o_ref[...] = h * _rms_scale(h) * fin_ref[...]


def _combine(run_start, run_cnt, h, route, final_norm, ys):
    T, D = h.shape
    tc = MERGE_ROWS
    return pl.pallas_call(
        _combine_kernel,
        grid_spec=pltpu.PrefetchScalarGridSpec(
            num_scalar_prefetch=2,
            grid=(T // tc,),
            in_specs=[pl.BlockSpec((tc, D), lambda i, *_: (i, 0)),
                      pl.BlockSpec((tc, LANES), lambda i, *_: (i, 0)),
                      pl.BlockSpec((1, D), lambda i, *_: (0, 0)),
                      pl.BlockSpec(memory_space=pl.ANY)],
            out_specs=pl.BlockSpec((tc, D), lambda i, *_: (i, 0)),
            scratch_shapes=[pltpu.VMEM((2, STAGE_ROWS, D), jnp.bfloat16),
                            pltpu.SemaphoreType.DMA((2,))],
        ),
        out_shape=jax.ShapeDtypeStruct((T, D), jnp.float32),
        compiler_params=pltpu.CompilerParams(
            dimension_semantics=("arbitrary",), vmem_limit_bytes=VMEM_LIMIT),
        name="moe_combine_final_norm",
    )(run_start, run_cnt, h, route, final_norm.reshape(1, D), ys)


def _routing_tables(tinfo, counts_row, n_tokens):
    counts = counts_row[0, :N_EXPERTS].astype(jnp.int32)
    padded = ((counts + MOE_ROWS - 1) // MOE_ROWS) * MOE_ROWS
    ends = jnp.cumsum(padded)
    starts = ends - padded
    run_start = starts[None, :] + tinfo[:, T_BASE, :N_EXPERTS].astype(jnp.int32)
    run_cnt = tinfo[:, T_COUNT, :N_EXPERTS].astype(jnp.int32)
    run_padding = (n_tokens // MERGE_ROWS) * N_EXPERTS * (RUN_ALIGN - 1)
    n_tiles = (TOP_K * n_tokens + run_padding + MOE_ROWS - 1) // MOE_ROWS + N_EXPERTS
    tile_start = jnp.arange(n_tiles, dtype=jnp.int32) * MOE_ROWS
    tile_e = jnp.minimum(jnp.sum(ends[None, :] <= tile_start[:, None], axis=1),
                         N_EXPERTS - 1).astype(jnp.int32)
    used = (ends[-1:] // MOE_ROWS).astype(jnp.int32)
    tile_valid = jnp.clip(counts[tile_e] - (tile_start - starts[tile_e]), 0, MOE_ROWS).astype(jnp.int32)
    pad_start = (starts + counts).astype(jnp.int32)
    pad_len = (padded - counts).astype(jnp.int32)
    return (run_start.reshape(-1), run_cnt.reshape(-1), tile_e, used, tile_valid, pad_start,
            pad_len, n_tiles * MOE_ROWS)


def kernel(x, a_norm, a_proj, a_scale, kv_norm, w_kv, b_norm, w_q, w_o, rel_bias, ffn_norm,
           dense_w_gate, dense_w_up, dense_w_down, moe_router, moe_w_gate, moe_w_up,
           moe_w_down, final_norm):
    B, S, D = x.shape
    T = B * S
    E, _, F = moe_w_gate[0].shape
    h2, (wg16, wq16, wkv16, wo16) = _layer0(
        x.reshape(T, D), S, a_norm[0], a_proj[0], a_scale[0], ffn_norm[0],
        dense_w_gate[0], dense_w_up[0], dense_w_down[0],
        (moe_w_gate[0].reshape(E * D, F), w_q[0], w_kv, w_o[0]))
    qkv, wu16 = _qkv_project(h2, kv_norm, b_norm[0], wq16, wkv16, B, S,
                             moe_w_up[0].reshape(E * D, F))
    outs, stats = [], []
    for g, (q, k, v) in enumerate(qkv):
        o, st = _attention_group(q, k, v, rel_bias, g, B, S)
        outs.append(o)
        stats.append(st)
    h3, hn3, route, route_t, tinfo, counts = _merge_layer(outs, stats, h2, wo16, ffn_norm[1],
                                                          moe_router[0])
    (run_start, run_cnt, tile_e, used, tile_valid, pad_start, pad_len,
     n_slots) = _routing_tables(tinfo, counts, T)
    xs, wd16 = _dispatch(run_start, run_cnt, pad_start, pad_len, used, hn3, route_t, n_slots,
                         moe_w_down[0].reshape(E * F, D))
    ys = _expert_ffn(tile_e, used, tile_valid, xs, wg16.reshape(E, D, F), wu16.reshape(E, D, F),
                     wd16.reshape(E, F, D))
    out = _combine(run_start, run_cnt, h3, route, final_norm, ys)
    return out.reshape(B, S, D)
```

```python
import functools
import math

import jax
import jax.numpy as jnp
from jax import lax
from jax.experimental import pallas as pl
from jax.experimental.pallas import tpu as pltpu

D_MODEL = 1024
EPS = 1e-6
POOL_WINDOWS = (2, 4, 8, 16)
POOL_GROUP_DIM = D_MODEL // len(POOL_WINDOWS)
MAX_POOL_WINDOW = max(POOL_WINDOWS)
HEAD_DIM = 64
HEADS_PER_GROUP = D_MODEL // HEAD_DIM
DILATED_PATTERNS = ((128, 1), (512, 4), (2048, 16))
N_ATT_GROUPS = len(DILATED_PATTERNS)
QBLK = 128
NEG_INF = -1e30
LOG2E = 1.4426950408889634
Q_SCALE = HEAD_DIM ** -0.5 * LOG2E
N_BUCKETS = 32
MAX_EXACT = N_BUCKETS // 2
MAX_DISTANCE = 2048
N_EXPERTS = 8
TOP_K = 2

LANES = 128
SUBLANES = 8
N_SLABS = D_MODEL // LANES
HEAD_PAIR = 2 * HEAD_DIM
assert HEAD_PAIR == LANES

LAYER0_ROWS = 512
QKV_ROWS = 512
ATTN_BLOCKS_PER_STEP = 8
MERGE_ROWS = 512
MOE_ROWS = 1024
MOE_FT = 1792
MOE_CHUNK = 256
MOE_SUBROWS = 256
MOE_NT = 1024
RUN_ALIGN = 16
STAGE_ROWS = -(-(TOP_K * MERGE_ROWS + N_EXPERTS * (RUN_ALIGN - 1)) // LANES) * LANES
VMEM_LIMIT = 60 * 1024 * 1024

R_E0, R_E1, R_RANK0, R_RANK1, R_G0, R_G1 = 0, 1, 2, 3, 4, 5
T_BASE, T_COUNT = 0, 1


def _rms_scale(x):
    return lax.rsqrt(jnp.mean(x * x, axis=-1, keepdims=True) + EPS)


def _silu(g):
    return g * (1.0 / (1.0 + jnp.exp(-g)))


def _split_bf16(x):
    hi = x.astype(jnp.bfloat16)
    lo = (x - hi.astype(jnp.float32)).astype(jnp.bfloat16)
    return hi, lo


def _pack_bf16_pair(a, b):
    hi = lax.bitcast_convert_type(a.astype(jnp.bfloat16).astype(jnp.float32), jnp.uint32)
    lo = lax.bitcast_convert_type(b.astype(jnp.bfloat16).astype(jnp.float32), jnp.uint32)
    return hi | (lo >> 16)


def _unpack_bf16_pair(w):
    return (lax.bitcast_convert_type(w & jnp.uint32(0xFFFF0000), jnp.float32),
            lax.bitcast_convert_type(w << 16, jnp.float32))


def _side_cast_specs(side, n_steps):
    rows = side.shape[0] // n_steps
    spec = pl.BlockSpec((rows, side.shape[1]), lambda i, *_: (i, 0))
    return spec, jax.ShapeDtypeStruct(side.shape, jnp.bfloat16)


def _pool_mix(x, halo, seq_tile, gain, proj_ref, scale):
    xn = x * _rms_scale(x) * gain
    hn_halo = halo * _rms_scale(halo) * gain
    hn_halo = jnp.where(seq_tile > 0, hn_halo, 0.0)
    full = jnp.concatenate([hn_halo, xn], axis=0)
    ts = x.shape[0]
    pos = seq_tile * ts + lax.broadcasted_iota(jnp.int32, (ts, 1), 0)
    outs = []
    for g, w in enumerate(POOL_WINDOWS):
        c0 = g * POOL_GROUP_DIM
        s = full[:, c0:c0 + POOL_GROUP_DIM]
        span = 1
        while span < w:
            s = s + pltpu.roll(s, span, axis=0)
            span *= 2
        s = s[MAX_POOL_WINDOW:, :]
        cnt = jnp.minimum(pos + 1, w).astype(jnp.float32)
        pooled = s / cnt - xn[:, c0:c0 + POOL_GROUP_DIM]
        outs.append(jnp.dot(pooled.astype(jnp.bfloat16), proj_ref[g],
                            preferred_element_type=jnp.float32))
    return x + jnp.concatenate(outs, axis=1) * scale


def _layer0_kernel(*refs, tiles_per_seq, n_side):
    (x_ref, halo_ref, an_ref, proj_ref, asc_ref, fn_ref, wg_ref, wu_ref, wd_ref) = refs[:9]
    side_in = refs[9:9 + n_side]
    h_out_ref = refs[9 + n_side]
    side_out = refs[10 + n_side:]
    for src, dst in zip(side_in, side_out):
        dst[...] = src[...].astype(dst.dtype)
    seq_tile = lax.rem(pl.program_id(0), tiles_per_seq)
    h = _pool_mix(x_ref[...], halo_ref[...], seq_tile, an_ref[...], proj_ref, asc_ref[...])
    hn = (h * _rms_scale(h) * fn_ref[...]).astype(jnp.bfloat16)
    g = jnp.dot(hn, wg_ref[...], preferred_element_type=jnp.float32)
    u = jnp.dot(hn, wu_ref[...], preferred_element_type=jnp.float32)
    hid = (_silu(g) * u).astype(jnp.bfloat16)
    h_out_ref[...] = h + jnp.dot(hid, wd_ref[...], preferred_element_type=jnp.float32)


def _layer0(x, seq_len, a_norm, a_proj, a_scale, ffn_norm, w_gate, w_up, w_down, sides):
    T, D = x.shape
    F = w_gate.shape[1]
    ts = LAYER0_ROWS
    n_steps = T // ts
    halo_blocks = ts // MAX_POOL_WINDOW
    row = lambda i: (i, 0)
    vec = pl.BlockSpec((1, D), lambda i: (0, 0))
    resident = dict(index_map=lambda i: (0, 0), pipeline_mode=pl.Buffered(1))
    side_specs, side_shapes = zip(*[_side_cast_specs(a, n_steps) for a in sides])
    outs = pl.pallas_call(
        functools.partial(_layer0_kernel, tiles_per_seq=seq_len // ts, n_side=len(sides)),
        grid=(n_steps,),
        in_specs=[
            pl.BlockSpec((ts, D), row),
            pl.BlockSpec((MAX_POOL_WINDOW, D), lambda i: (jnp.maximum(i * halo_blocks - 1, 0), 0)),
            vec,
            pl.BlockSpec(a_proj.shape, lambda i: (0, 0, 0), pipeline_mode=pl.Buffered(1)),
            vec, vec,
            pl.BlockSpec((D, F), **resident),
            pl.BlockSpec((D, F), **resident),
            pl.BlockSpec((F, D), **resident),
            *side_specs,
        ],
        out_specs=[pl.BlockSpec((ts, D), row), *side_specs],
        out_shape=[jax.ShapeDtypeStruct((T, D), jnp.float32), *side_shapes],
        compiler_params=pltpu.CompilerParams(
            dimension_semantics=("arbitrary",), vmem_limit_bytes=VMEM_LIMIT),
        name="pool_mixer_dense_swiglu",
    )(x, x, a_norm.reshape(1, D), a_proj.astype(jnp.bfloat16), a_scale.reshape(1, D),
      ffn_norm.reshape(1, D), w_gate.astype(jnp.bfloat16), w_up.astype(jnp.bfloat16),
      w_down.astype(jnp.bfloat16), *sides)
    return outs[0], outs[1:]


def _qkv_kernel(*refs):
    slabs = refs[:N_SLABS]
    kvg_ref, qg_ref, wq_ref, wkv_ref, side_in_ref = refs[N_SLABS:N_SLABS + 5]
    out_refs = refs[N_SLABS + 5:-2]
    side_out_ref, xs_scr = refs[-2:]
    side_out_ref[...] = side_in_ref[...].astype(side_out_ref.dtype)
    tm = slabs[0].shape[0]
    x_nat = jnp.concatenate([slab[...] for slab in slabs], axis=1)
    scale = _rms_scale(x_nat)
    for c, slab in enumerate(slabs):
        xs_scr[c] = slab[...] * scale
    for g, (_, dilation) in enumerate(DILATED_PATTERNS):
        n = tm // dilation
        cols = []
        for c in range(N_SLABS):
            if dilation == 1:
                cols.append(xs_scr[c])
            else:
                cols.append(jnp.concatenate(
                    [xs_scr[c, pl.ds(r, n, stride=dilation), :] for r in range(dilation)], axis=0))
        xs = jnp.concatenate(cols, axis=1)
        xq = (xs * qg_ref[...]).astype(jnp.bfloat16)
        xkv = (xs * kvg_ref[...]).astype(jnp.bfloat16)
        D = D_MODEL
        kcols = slice(g * D, (g + 1) * D)
        vcols = slice((N_ATT_GROUPS + g) * D, (N_ATT_GROUPS + g + 1) * D)
        q = jnp.dot(xq, wq_ref[:, kcols], preferred_element_type=jnp.float32) * Q_SCALE
        k = jnp.dot(xkv, wkv_ref[:, kcols], preferred_element_type=jnp.float32)
        v = jnp.dot(xkv, wkv_ref[:, vcols], preferred_element_type=jnp.float32)
        for out_ref, val in zip(out_refs[3 * g:3 * g + 3], (q, k, v)):
            for r in range(dilation):
                out_ref[r] = val[r * n:(r + 1) * n].astype(out_ref.dtype)


def _qkv_project(h, kv_norm, b_norm, w_q, w_kv, B, S, side):
    T, D = h.shape
    tm = QKV_ROWS
    tiles_per_seq = S // tm
    resident = dict(index_map=lambda i: (0, 0), pipeline_mode=pl.Buffered(1))
    slab_specs = [pl.BlockSpec((tm, LANES), functools.partial(lambda i, c: (i, c), c=c))
                  for c in range(N_SLABS)]
    vec = pl.BlockSpec((1, D), lambda i: (0, 0))
    out_specs, out_shapes = [], []
    for _, d in DILATED_PATTERNS:
        spec = pl.BlockSpec((None, d, tm // d, D),
                            lambda i: (i // tiles_per_seq, 0, i % tiles_per_seq, 0))
        out_specs += [spec] * 3
        out_shapes += [jax.ShapeDtypeStruct((B, d, S // d, D), jnp.bfloat16)] * 3
    side_spec, side_shape = _side_cast_specs(side, T // tm)
    outs = pl.pallas_call(
        _qkv_kernel,
        grid=(T // tm,),
        in_specs=slab_specs + [vec, vec, pl.BlockSpec(w_q.shape, **resident),
                               pl.BlockSpec(w_kv.shape, **resident), side_spec],
        out_specs=out_specs + [side_spec],
        out_shape=out_shapes + [side_shape],
        scratch_shapes=[pltpu.VMEM((N_SLABS, tm, LANES), jnp.float32)],
        compiler_params=pltpu.CompilerParams(
            dimension_semantics=("arbitrary",), vmem_limit_bytes=VMEM_LIMIT),
        name="qkv_proj",
    )(*([h] * N_SLABS), kv_norm.reshape(1, D), b_norm.reshape(1, D), w_q, w_kv, side)
    return [tuple(outs[3 * g:3 * g + 3]) for g in range(N_ATT_GROUPS)], outs[-1]


def _attn_kernel(*refs, dilation, has_prev, jb, rb):
    if has_prev:
        (table_ref, bmap_ref, q_ref, kp_ref, kc_ref, vp_ref, vc_ref,
         o_ref, stat_ref, bias_ref, stat_scr) = refs
    else:
        table_ref, bmap_ref, q_ref, kc_ref, vc_ref, o_ref, stat_ref, bias_ref, stat_scr = refs
    b, jstep, rstep = pl.program_id(0), pl.program_id(1), pl.program_id(2)
    n_keys = bias_ref.shape[2]

    @pl.when(jnp.logical_and(jnp.logical_and(b == 0, jstep == 0), rstep == 0))
    def _():
        stat_scr[...] = jnp.zeros_like(stat_scr)
        bmap = bmap_ref[...]
        in_prev = lax.broadcasted_iota(jnp.int32, bmap.shape, 1) < n_keys - QBLK

        def head(h, carry):
            acc = jnp.full(bmap.shape, NEG_INF, jnp.float32)
            for bucket in range(N_BUCKETS):
                acc = jnp.where(bmap == bucket, table_ref[bucket, h], acc)
            rows_h = pl.ds(pl.multiple_of(h * QBLK, QBLK), QBLK)
            bias_ref[0, rows_h, :] = jnp.where(in_prev, NEG_INF, acc)
            bias_ref[1, rows_h, :] = acc
            return carry

        lax.fori_loop(0, HEADS_PER_GROUP, head, 0)

    lane = lax.broadcasted_iota(jnp.int32, (QBLK, LANES), 1)
    first_head = lane < HEAD_DIM
    for rr in range(rb):
        for jj in range(jb):
            qrows = slice(jj * QBLK, (jj + 1) * QBLK)
            bias_copy = jnp.minimum(jstep, 1) if jj == 0 else 1
            if dilation > 1:
                out_rows = pl.ds(dilation * QBLK * jj + rstep * rb + rr, QBLK, stride=dilation)
            else:
                out_rows = qrows
            for hp in range(HEADS_PER_GROUP // 2):
                sl = slice(hp * HEAD_PAIR, (hp + 1) * HEAD_PAIR)
                qp = q_ref[rr, qrows, sl]
                zero = jnp.zeros_like(qp)
                q2 = jnp.concatenate([jnp.where(first_head, qp, zero),
                                      jnp.where(first_head, zero, qp)], axis=0)
                if has_prev:
                    if jj == 0:
                        k_prev, v_prev = kp_ref[rr, :, sl], vp_ref[rr, :, sl]
                    else:
                        prows = slice((jj - 1) * QBLK, jj * QBLK)
                        k_prev, v_prev = kc_ref[rr, prows, sl], vc_ref[rr, prows, sl]
                    kcat = jnp.concatenate([k_prev, kc_ref[rr, qrows, sl]], axis=0)
                    vcat = jnp.concatenate([v_prev, vc_ref[rr, qrows, sl]], axis=0)
                else:
                    kcat, vcat = kc_ref[rr, qrows, sl], vc_ref[rr, qrows, sl]
                s = lax.dot_general(q2, kcat, (((1,), (1,)), ((), ())),
                                    preferred_element_type=jnp.float32)
                s = s + bias_ref[bias_copy, hp * 2 * QBLK:(hp + 1) * 2 * QBLK, :]
                m = jnp.max(s, axis=1, keepdims=True)
                p = jnp.exp2(s - m)
                l = jnp.sum(p, axis=1, keepdims=True)
                o2 = jnp.dot(p.astype(jnp.bfloat16), vcat, preferred_element_type=jnp.float32)
                o_pair = jnp.where(first_head, o2[:QBLK], o2[QBLK:])
                if hp % 2 == 0:
                    o_even = o_pair
                else:
                    o_ref[hp // 2, out_rows, :] = _pack_bf16_pair(o_even, o_pair)
                for half, head in ((slice(0, QBLK), 2 * hp), (slice(QBLK, 2 * QBLK), 2 * hp + 1)):
                    stat_scr[:, head:head + 1] = m[half]
                    stat_scr[:, HEADS_PER_GROUP + head:HEADS_PER_GROUP + head + 1] = l[half]
            stat_ref[out_rows, :] = stat_scr[...]


def _bucket_map(dilation, window, n_keys):
    W = window // dilation
    a = jnp.arange(QBLK, dtype=jnp.int32)[:, None]
    c = jnp.arange(n_keys, dtype=jnp.int32)[None, :]
    m = a + (n_keys - QBLK) - c
    band = (m >= 0) & (m <= W)
    n = jnp.maximum(m, 0) * dilation
    nf = jnp.maximum(n, 1).astype(jnp.float32)
    large = MAX_EXACT + (jnp.log(nf / MAX_EXACT) / math.log(MAX_DISTANCE / MAX_EXACT)
                         * (N_BUCKETS - MAX_EXACT)).astype(jnp.int32)
    large = jnp.minimum(large, N_BUCKETS - 1)
    bucket = jnp.where(n < MAX_EXACT, n, large)
    return jnp.where(band, bucket, -1)


def _attention_group(q, k, v, rel_bias, g, B, S):
    window, d = DILATED_PATTERNS[g]
    L = S // d
    n = L // QBLK
    D = D_MODEL
    has_prev = n > 1
    n_keys = 2 * QBLK if has_prev else QBLK
    table = rel_bias[:, g * HEADS_PER_GROUP:(g + 1) * HEADS_PER_GROUP].astype(jnp.float32) * LOG2E
    bmap = _bucket_map(d, window, n_keys)
    jb = min(ATTN_BLOCKS_PER_STEP, n)
    rb = ATTN_BLOCKS_PER_STEP // jb
    cur_blk = (None, rb, jb * QBLK, D)
    prev_blk = (None, rb, QBLK, D)
    cur = lambda b, j, r: (b, r, j, 0)
    prev = lambda b, j, r: (b, r, jnp.maximum(j * jb - 1, 0), 0)
    if has_prev:
        kv_specs = [pl.BlockSpec(prev_blk, prev), pl.BlockSpec(cur_blk, cur),
                    pl.BlockSpec(prev_blk, prev), pl.BlockSpec(cur_blk, cur)]
        kv_args = (k, k, v, v)
    else:
        kv_specs = [pl.BlockSpec(cur_blk, cur), pl.BlockSpec(cur_blk, cur)]
        kv_args = (k, v)
    n_j = n // jb
    span = QBLK * jb * d
    return pl.pallas_call(
        functools.partial(_attn_kernel, dilation=d, has_prev=has_prev, jb=jb, rb=rb),
        grid=(B, n_j, d // rb),
        in_specs=[pl.BlockSpec(memory_space=pltpu.SMEM),
                  pl.BlockSpec(bmap.shape, lambda b, j, r: (0, 0)),
                  pl.BlockSpec(cur_blk, cur)] + kv_specs,
        out_specs=[pl.BlockSpec((N_SLABS // 2, span, LANES), lambda b, j, r: (0, b * n_j + j, 0)),
                   pl.BlockSpec((span, LANES), lambda b, j, r: (b * n_j + j, 0))],
        out_shape=[jax.ShapeDtypeStruct((N_SLABS // 2, B * S, LANES), jnp.uint32),
                   jax.ShapeDtypeStruct((B * S, LANES), jnp.float32)],
        scratch_shapes=[pltpu.VMEM((2, HEADS_PER_GROUP * QBLK, n_keys), jnp.float32),
                        pltpu.VMEM((QBLK, LANES), jnp.float32)],
        compiler_params=pltpu.CompilerParams(
            dimension_semantics=("arbitrary",) * 3, vmem_limit_bytes=VMEM_LIMIT),
        name=f"dilated_attn_g{g}",
    )(table, bmap, q, *kv_args)


def _merge_kernel(o0_ref, o1_ref, o2_ref, s0_ref, s1_ref, s2_ref, h_ref, wo_ref, fn_ref,
                  r2_ref, expand_ref,
                  h_out_ref, hn_ref, route_ref, route_t_ref, tinfo_ref, counts_ref, carry_ref):
    i = pl.program_id(0)
    tm = h_ref.shape[0]

    @pl.when(i == 0)
    def _():
        carry_ref[...] = jnp.zeros_like(carry_ref)

    lane = lax.broadcasted_iota(jnp.int32, (tm, LANES), 1)
    head_lane = lane < HEADS_PER_GROUP
    stats = [s0_ref[...], s1_ref[...], s2_ref[...]]
    dens = [pltpu.roll(st, LANES - HEADS_PER_GROUP, axis=1) for st in stats]
    mx = jnp.maximum(jnp.maximum(stats[0], stats[1]), stats[2])
    es = [jnp.exp2(st - mx) for st in stats]
    inv = 1.0 / (es[0] * dens[0] + es[1] * dens[1] + es[2] * dens[2])
    merged = None
    for e, o_ref in zip(es, (o0_ref, o1_ref, o2_ref)):
        w = jnp.where(head_lane, e * inv, 0.0)
        hi = w.astype(jnp.bfloat16).astype(jnp.float32)
        packed = (hi + pltpu.roll(w - hi, HEADS_PER_GROUP, axis=1)).astype(jnp.bfloat16)
        wide = jnp.dot(packed, expand_ref[...], preferred_element_type=jnp.float32)
        o = jnp.concatenate([part for c in range(N_SLABS // 2)
                             for part in _unpack_bf16_pair(o_ref[c])], axis=1)
        term = wide * o
        merged = term if merged is None else merged + term
    h = h_ref[...] + jnp.dot(merged.astype(jnp.bfloat16), wo_ref[...],
                             preferred_element_type=jnp.float32)
    h_out_ref[...] = h
    hn = h * _rms_scale(h) * fn_ref[...]
    hn_ref[...] = hn.astype(hn_ref.dtype)

    hi, lo = _split_bf16(hn)
    both = jnp.dot(hi, r2_ref[...], preferred_element_type=jnp.float32)
    logits = (both[:, :LANES] + both[:, LANES:]
              + jnp.dot(lo, r2_ref[:, :LANES], preferred_element_type=jnp.float32))
    logits = jnp.where(lane < N_EXPERTS, logits, -jnp.inf)
    v0 = jnp.max(logits, axis=1, keepdims=True)
    e0 = jnp.min(jnp.where(logits == v0, lane, LANES), axis=1, keepdims=True)
    rest = jnp.where(lane == e0, -jnp.inf, logits)
    v1 = jnp.max(rest, axis=1, keepdims=True)
    e1 = jnp.min(jnp.where(rest == v1, lane, LANES), axis=1, keepdims=True)
    t = jnp.exp(v1 - v0)
    g0 = 1.0 / (1.0 + t)
    g1 = t / (1.0 + t)

    hit0 = lane == e0
    hit1 = lane == e1
    onehot = jnp.where(jnp.logical_or(hit0, hit1), 1.0, 0.0)
    r_i = lax.broadcasted_iota(jnp.int32, (tm, tm), 0)
    c_i = lax.broadcasted_iota(jnp.int32, (tm, tm), 1)
    lower = jnp.where(c_i < r_i, 1.0, 0.0).astype(jnp.bfloat16)
    before = jnp.dot(lower, onehot.astype(jnp.bfloat16), preferred_element_type=jnp.float32)
    rank0 = jnp.sum(jnp.where(hit0, before, 0.0), axis=1, keepdims=True)
    rank1 = jnp.sum(jnp.where(hit1, before, 0.0), axis=1, keepdims=True)
    tile_counts = jnp.sum(onehot, axis=0, keepdims=True)
    base = carry_ref[0:1, :]
    total = base + jnp.floor((tile_counts + (RUN_ALIGN - 1)) * (1.0 / RUN_ALIGN)) * RUN_ALIGN
    carry_ref[...] = jnp.broadcast_to(total, carry_ref.shape)
    counts_ref[...] = jnp.broadcast_to(total, counts_ref.shape)
    row_id = lax.broadcasted_iota(jnp.int32, tinfo_ref.shape, 0)
    tinfo_ref[...] = jnp.where(row_id == T_BASE, base, jnp.where(row_id == T_COUNT, tile_counts, 0.0))

    route = jnp.zeros((tm, LANES), jnp.float32)
    for ln, val in ((R_E0, e0.astype(jnp.float32)), (R_E1, e1.astype(jnp.float32)),
                    (R_RANK0, rank0), (R_RANK1, rank1), (R_G0, g0), (R_G1, g1)):
        route = jnp.where(lane == ln, val, route)
    route_ref[...] = route
    route_t_ref[...] = route.T[:SUBLANES, :]


def _merge_layer(outs, lses, h, w_o, ffn_norm, w_router):
    T, D = h.shape
    tm = MERGE_ROWS
    r_pad = jnp.zeros((D, LANES), jnp.float32).at[:, :N_EXPERTS].set(w_router)
    r2 = jnp.concatenate(_split_bf16(r_pad), axis=1)
    head_of_col = jnp.arange(D, dtype=jnp.int32) // HEAD_DIM
    lane_id = jnp.arange(LANES, dtype=jnp.int32)[:, None]
    expand = ((lane_id < 2 * HEADS_PER_GROUP)
              & (lane_id % HEADS_PER_GROUP == head_of_col[None, :])).astype(jnp.bfloat16)
    row = lambda i: (i, 0)
    const = lambda i: (0, 0)
    wide = pl.BlockSpec((tm, D), row)
    narrow = pl.BlockSpec((tm, LANES), row)
    slabs = pl.BlockSpec((N_SLABS // 2, tm, LANES), lambda i: (0, i, 0))
    return pl.pallas_call(
        _merge_kernel,
        grid=(T // tm,),
        in_specs=[slabs, slabs, slabs, narrow, narrow, narrow, wide,
                  pl.BlockSpec((D, D), const), pl.BlockSpec((1, D), const),
                  pl.BlockSpec((D, 2 * LANES), const), pl.BlockSpec((LANES, D), const)],
        out_specs=[wide, wide, narrow,
                   pl.BlockSpec((None, SUBLANES, tm), lambda i: (i, 0, 0)),
                   pl.BlockSpec((None, SUBLANES, LANES), lambda i: (i, 0, 0)),
                   pl.BlockSpec((8, LANES), const)],
        out_shape=[jax.ShapeDtypeStruct((T, D), jnp.float32),
                   jax.ShapeDtypeStruct((T, D), jnp.bfloat16),
                   jax.ShapeDtypeStruct((T, LANES), jnp.float32),
                   jax.ShapeDtypeStruct((T // tm, SUBLANES, tm), jnp.float32),
                   jax.ShapeDtypeStruct((T // tm, SUBLANES, LANES), jnp.float32),
                   jax.ShapeDtypeStruct((8, LANES), jnp.float32)],
        scratch_shapes=[pltpu.VMEM((8, LANES), jnp.float32)],
        compiler_params=pltpu.CompilerParams(
            dimension_semantics=("arbitrary",), vmem_limit_bytes=VMEM_LIMIT),
        name="merge_outproj_router",
    )(*outs, *lses, h, w_o, ffn_norm.reshape(1, D), r2, expand)


def _run_offsets(cnt_ref, tile):
    sizes, offs, off = [], [], 0
    for e in range(N_EXPERTS):
        size = ((cnt_ref[tile * N_EXPERTS + e] + (RUN_ALIGN - 1)) // RUN_ALIGN) * RUN_ALIGN
        sizes.append(size)
        offs.append(off)
        off = off + size
    return sizes, offs


def _for_each_run_piece(start_ref, cnt_ref, tile, fn):
    sizes, offs = _run_offsets(cnt_ref, tile)
    for e in range(N_EXPERTS):
        slot0 = start_ref[tile * N_EXPERTS + e]
        bit = MERGE_ROWS
        while bit >= RUN_ALIGN:
            done = sizes[e] & -(2 * bit)

            @pl.when((sizes[e] & bit) != 0)
            def _():
                fn(pl.multiple_of(offs[e] + done, RUN_ALIGN),
                   pl.multiple_of(slot0 + done, RUN_ALIGN), bit)
            bit //= 2


def _staging_columns(expert, rank, offs):
    col = rank
    for e in range(N_EXPERTS):
        col = col + jnp.where(expert == e, offs[e], 0)
    return col


def _dispatch_kernel(start_ref, cnt_ref, pad_start_ref, pad_len_ref, used_ref,
                     hn_ref, route_t_ref, side_in_ref, xs_ref, side_out_ref,
                     xbuf, zbuf, sems, zsem):
    side_out_ref[...] = side_in_ref[...].astype(side_out_ref.dtype)
    i = pl.program_id(0)
    n_steps = pl.num_programs(0)
    slot = lax.rem(i, 2)
    n_tiles = xs_ref.shape[0] // MOE_ROWS

    def zero_fill(act):
        for e in range(N_EXPERTS):
            run_start = pl.multiple_of(pad_start_ref[e], RUN_ALIGN)
            run_len = pad_len_ref[e]
            bit = MOE_ROWS // 2
            while bit >= RUN_ALIGN:
                off = pl.multiple_of(run_start + (run_len & -(2 * bit)), RUN_ALIGN)

                @pl.when((run_len & bit) != 0)
                def _():
                    act(pltpu.make_async_copy(zbuf.at[pl.ds(0, bit), :],
                                              xs_ref.at[pl.ds(off, bit), :], zsem))
                bit //= 2
        for t in range(n_tiles):
            @pl.when(t >= used_ref[0])
            def _():
                act(pltpu.make_async_copy(zbuf, xs_ref.at[pl.ds(t * MOE_ROWS, MOE_ROWS), :], zsem))

    def run_copies(tile, buf_slot, act):
        def piece(buf_row, slot_row, rows):
            act(pltpu.make_async_copy(xbuf.at[buf_slot, pl.ds(buf_row, rows), :],
                                      xs_ref.at[pl.ds(slot_row, rows), :], sems.at[buf_slot]))
        _for_each_run_piece(start_ref, cnt_ref, tile, piece)

    @pl.when(i == 0)
    def _():
        zbuf[...] = jnp.zeros_like(zbuf)
        zero_fill(lambda c: c.start())

    @pl.when(i >= 2)
    def _():
        run_copies(i - 2, slot, lambda c: c.wait())

    _, offs = _run_offsets(cnt_ref, i)
    route_t = route_t_ref[...]
    cols = [_staging_columns(route_t[R_E0 + k:R_E0 + k + 1, :].astype(jnp.int32),
                             route_t[R_RANK0 + k:R_RANK0 + k + 1, :].astype(jnp.int32), offs)
            for k in range(TOP_K)]
    buf_row = lax.broadcasted_iota(jnp.int32, (STAGE_ROWS, hn_ref.shape[0]), 0)
    onehot = jnp.where(jnp.logical_or(buf_row == cols[0], buf_row == cols[1]), 1.0, 0.0)
    xbuf[slot] = jnp.dot(onehot.astype(jnp.bfloat16), hn_ref[...],
                         preferred_element_type=jnp.float32).astype(xbuf.dtype)
    run_copies(i, slot, lambda c: c.start())

    @pl.when(i == n_steps - 1)
    def _():
        @pl.when(i >= 1)
        def _():
            run_copies(i - 1, 1 - slot, lambda c: c.wait())
        run_copies(i, slot, lambda c: c.wait())
        zero_fill(lambda c: c.wait())


def _dispatch(run_start, run_cnt, pad_start, pad_len, used, hn, route_t, n_slots, side):
    T, D = hn.shape
    tm = MERGE_ROWS
    side_spec, side_shape = _side_cast_specs(side, T // tm)
    return pl.pallas_call(
        _dispatch_kernel,
        grid_spec=pltpu.PrefetchScalarGridSpec(
            num_scalar_prefetch=5,
            grid=(T // tm,),
            in_specs=[pl.BlockSpec((tm, D), lambda i, *_: (i, 0)),
                      pl.BlockSpec((None, SUBLANES, tm), lambda i, *_: (i, 0, 0)),
                      side_spec],
            out_specs=[pl.BlockSpec(memory_space=pl.ANY), side_spec],
            scratch_shapes=[pltpu.VMEM((2, STAGE_ROWS, D), jnp.bfloat16),
                            pltpu.VMEM((MOE_ROWS, D), jnp.bfloat16),
                            pltpu.SemaphoreType.DMA((2,)), pltpu.SemaphoreType.DMA(())],
        ),
        out_shape=[jax.ShapeDtypeStruct((n_slots, D), jnp.bfloat16), side_shape],
        compiler_params=pltpu.CompilerParams(
            dimension_semantics=("arbitrary",), vmem_limit_bytes=VMEM_LIMIT),
        name="moe_dispatch",
    )(run_start, run_cnt, pad_start, pad_len, used, hn, route_t, side)


def _expert_kernel(tile_e_ref, used_ref, valid_ref, x_ref, wg_ref, wu_ref, wd_ref, y_ref,
                   hid_ref):
    del tile_e_ref
    i = pl.program_id(0)
    s = pl.program_id(1)
    n_f, tm, tf = hid_ref.shape
    active = i < used_ref[0]
    n_valid = valid_ref[jnp.minimum(i, used_ref[0] - 1)]
    full = n_valid == tm
    gate_step = jnp.logical_and(active, s < n_f)
    down_step = jnp.logical_and(active, s >= n_f)

    def gate_up(rows):
        x = x_ref[rows, :]
        for c0 in range(0, tf, MOE_CHUNK):
            cols = slice(c0, min(c0 + MOE_CHUNK, tf))
            g = jnp.dot(x, wg_ref[:, cols], preferred_element_type=jnp.float32)
            u = jnp.dot(x, wu_ref[:, cols], preferred_element_type=jnp.float32)
            hid_ref[s, rows, cols] = (_silu(g) * u).astype(hid_ref.dtype)

    def down(rows):
        hid = jnp.concatenate([hid_ref[c, rows, :] for c in range(n_f)], axis=1)
        y_ref[rows, :] = jnp.dot(hid, wd_ref[...],
                                 preferred_element_type=jnp.float32).astype(y_ref.dtype)

    @pl.when(jnp.logical_and(gate_step, full))
    def _():
        gate_up(slice(None))

    @pl.when(jnp.logical_and(down_step, full))
    def _():
        down(slice(None))

    for r0 in range(0, tm, MOE_SUBROWS):
        rows = slice(r0, r0 + MOE_SUBROWS)
        occupied = jnp.logical_and(jnp.logical_not(full), r0 < n_valid)

        @pl.when(jnp.logical_and(gate_step, occupied))
        def _():
            gate_up(rows)

        @pl.when(jnp.logical_and(down_step, occupied))
        def _():
            down(rows)

        @pl.when(jnp.logical_and(down_step, r0 >= n_valid))
        def _():
            y_ref[rows, :] = jnp.zeros((MOE_SUBROWS, y_ref.shape[1]), y_ref.dtype)

    @pl.when(jnp.logical_and(jnp.logical_not(active), s >= n_f))
    def _():
        y_ref[...] = jnp.zeros_like(y_ref)


def _expert_ffn(tile_e, used, tile_valid, xs, w_gate, w_up, w_down):
    n_slots, D = xs.shape
    F = w_gate.shape[2]
    tm, tf, tn = MOE_ROWS, MOE_FT, MOE_NT
    n_f, n_n = F // tf, D // tn
    n_tiles = n_slots // tm
    last = n_f + n_n - 1

    def eff(i, s, used_ref):
        idle = i >= used_ref[0]
        return jnp.where(idle, used_ref[0] - 1, i), jnp.where(idle, last, s)

    def x_map(i, s, te, us, nv):
        ie, se = eff(i, s, us)
        return jnp.minimum(ie + jnp.where(se >= n_f, 1, 0), us[0] - 1), 0

    def gate_map(i, s, te, us, nv):
        ie, se = eff(i, s, us)
        return te[ie], 0, jnp.minimum(se, n_f - 1)

    def down_map(i, s, te, us, nv):
        ie, se = eff(i, s, us)
        tile = jnp.where(se == 0, jnp.maximum(ie - 1, 0), ie)
        return te[tile], 0, jnp.where(se == 0, n_n - 1, jnp.clip(se - n_f, 0, n_n - 1))

    def out_map(i, s, te, us, nv):
        return i, jnp.clip(s - n_f, 0, n_n - 1)

    return pl.pallas_call(
        _expert_kernel,
        grid_spec=pltpu.PrefetchScalarGridSpec(
            num_scalar_prefetch=3,
            grid=(n_tiles, n_f + n_n),
            in_specs=[
                pl.BlockSpec((tm, D), x_map),
                pl.BlockSpec((None, D, tf), gate_map),
                pl.BlockSpec((None, D, tf), gate_map),
                pl.BlockSpec((None, F, tn), down_map),
            ],
            out_specs=pl.BlockSpec((tm, tn), out_map),
            scratch_shapes=[pltpu.VMEM((n_f, tm, tf), jnp.bfloat16)],
        ),
        out_shape=jax.ShapeDtypeStruct((n_slots, D), jnp.bfloat16),
        compiler_params=pltpu.CompilerParams(
            dimension_semantics=("arbitrary", "arbitrary"), vmem_limit_bytes=VMEM_LIMIT),
        name="moe_experts",
    )(tile_e, used, tile_valid, xs, w_gate, w_up, w_down)


def _combine_kernel(start_ref, cnt_ref, h_ref, route_ref, fin_ref, ys_ref, o_ref, ybuf, sems):
    i = pl.program_id(0)
    n_steps = pl.num_programs(0)
    tile = h_ref.shape[0]
    slot = lax.rem(i, 2)

    def run_copies(step, buf_slot, act):
        def piece(buf_row, slot_row, rows):
            act(pltpu.make_async_copy(ys_ref.at[pl.ds(slot_row, rows), :],
                                      ybuf.at[buf_slot, pl.ds(buf_row, rows), :],
                                      sems.at[buf_slot]))
        _for_each_run_piece(start_ref, cnt_ref, step, piece)

    @pl.when(i == 0)
    def _():
        ybuf[...] = jnp.zeros_like(ybuf)
        run_copies(0, 0, lambda c: c.start())

    @pl.when(i + 1 < n_steps)
    def _():
        run_copies(i + 1, 1 - slot, lambda c: c.start())

    run_copies(i, slot, lambda c: c.wait())
    _, offs = _run_offsets(cnt_ref, i)
    route = route_ref[...]
    staged = ybuf[slot]
    buf_col = lax.broadcasted_iota(jnp.int32, (tile, STAGE_ROWS), 1)
    ys = []
    for k in range(TOP_K):
        col = _staging_columns(route[:, R_E0 + k:R_E0 + k + 1].astype(jnp.int32),
                               route[:, R_RANK0 + k:R_RANK0 + k + 1].astype(jnp.int32), offs)
        onehot = jnp.where(buf_col == col, 1.0, 0.0).astype(jnp.bfloat16)
        ys.append(jnp.dot(onehot, staged, preferred_element_type=jnp.float32))
    g0 = route[:, R_G0:R_G0 + 1]
    g1 = route[:, R_G1:R_G1 + 1]
    h = h_ref[...] + (g0 * ys[0] + g1 * ys[1])
    o_ref[...] = h * _rms_scale(h) * fin_ref[...]


def _combine(run_start, run_cnt, h, route, final_norm, ys):
    T, D = h.shape
    tc = MERGE_ROWS
    return pl.pallas_call(
        _combine_kernel,
        grid_spec=pltpu.PrefetchScalarGridSpec(
            num_scalar_prefetch=2,
            grid=(T // tc,),
            in_specs=[pl.BlockSpec((tc, D), lambda i, *_: (i, 0)),
                      pl.BlockSpec((tc, LANES), lambda i, *_: (i, 0)),
                      pl.BlockSpec((1, D), lambda i, *_: (0, 0)),
                      pl.BlockSpec(memory_space=pl.ANY)],
            out_specs=pl.BlockSpec((tc, D), lambda i, *_: (i, 0)),
            scratch_shapes=[pltpu.VMEM((2, STAGE_ROWS, D), jnp.bfloat16),
                            pltpu.SemaphoreType.DMA((2,))],
        ),
        out_shape=jax.ShapeDtypeStruct((T, D), jnp.float32),
        compiler_params=pltpu.CompilerParams(
            dimension_semantics=("arbitrary",), vmem_limit_bytes=VMEM_LIMIT),
        name="moe_combine_final_norm",
    )(run_start, run_cnt, h, route, final_norm.reshape(1, D), ys)


def _routing_tables(tinfo, counts_row, n_tokens):
    counts = counts_row[0, :N_EXPERTS].astype(jnp.int32)
    padded = ((counts + MOE_ROWS - 1) // MOE_ROWS) * MOE_ROWS
    ends = jnp.cumsum(padded)
    starts = ends - padded
    run_start = starts[None, :] + tinfo[:, T_BASE, :N_EXPERTS].astype(jnp.int32)
    run_cnt = tinfo[:, T_COUNT, :N_EXPERTS].astype(jnp.int32)
    run_padding = (n_tokens // MERGE_ROWS) * N_EXPERTS * (RUN_ALIGN - 1)
    n_tiles = (TOP_K * n_tokens + run_padding + MOE_ROWS - 1) // MOE_ROWS + N_EXPERTS
    tile_start = jnp.arange(n_tiles, dtype=jnp.int32) * MOE_ROWS
    tile_e = jnp.minimum(jnp.sum(ends[None, :] <= tile_start[:, None], axis=1),
                         N_EXPERTS - 1).astype(jnp.int32)
    used = (ends[-1:] // MOE_ROWS).astype(jnp.int32)
    tile_valid = jnp.clip(counts[tile_e] - (tile_start - starts[tile_e]), 0, MOE_ROWS).astype(jnp.int32)
    pad_start = (starts + counts).astype(jnp.int32)
    pad_len = (padded - counts).astype(jnp.int32)
    return (run_start.reshape(-1), run_cnt.reshape(-1), tile_e, used, tile_valid, pad_start,
            pad_len, n_tiles * MOE_ROWS)


def kernel(x, a_norm, a_proj, a_scale, kv_norm, w_kv, b_norm, w_q, w_o, rel_bias, ffn_norm,
           dense_w_gate, dense_w_up, dense_w_down, moe_router, moe_w_gate, moe_w_up,
           moe_w_down, final_norm):
    B, S, D = x.shape
    T = B * S
    E, _, F = moe_w_gate[0].shape
    h2, (wg16, wq16, wkv16, wo16) = _layer0(
        x.reshape(T, D), S, a_norm[0], a_proj[0], a_scale[0], ffn_norm[0],
        dense_w_gate[0], dense_w_up[0], dense_w_down[0],
        (moe_w_gate[0].reshape(E * D, F), w_q[0], w_kv, w_o[0]))
    qkv, wu16 = _qkv_project(h2, kv_norm, b_norm[0], wq16, wkv16, B, S,
                             moe_w_up[0].reshape(E * D, F))
    outs, stats = [], []
    for g, (q, k, v) in enumerate(qkv):
        o, st = _attention_group(q, k, v, rel_bias, g, B, S)
        outs.append(o)
        stats.append(st)
    h3, hn3, route, route_t, tinfo, counts = _merge_layer(outs, stats, h2, wo16, ffn_norm[1],
                                                          moe_router[0])
    (run_start, run_cnt, tile_e, used, tile_valid, pad_start, pad_len,
     n_slots) = _routing_tables(tinfo, counts, T)
    xs, wd16 = _dispatch(run_start, run_cnt, pad_start, pad_len, used, hn3, route_t, n_slots,
                         moe_w_down[0].reshape(E * F, D))
    ys = _expert_ffn(tile_e, used, tile_valid, xs, wg16.reshape(E, D, F), wu16.reshape(E, D, F),
                     wd16.reshape(E, F, D))
    out = _combine(run_start, run_cnt, h3, route, final_norm, ys)
    return out.reshape(B, S, D)
```

```python
import functools
import math

import jax
import jax.numpy as jnp
from jax import lax
from jax.experimental import pallas as pl
from jax.experimental.pallas import tpu as pltpu

D_MODEL = 1024
EPS = 1e-6
POOL_WINDOWS = (2, 4, 8, 16)
POOL_GROUP_DIM = D_MODEL // len(POOL_WINDOWS)
MAX_POOL_WINDOW = max(POOL_WINDOWS)
HEAD_DIM = 64
HEADS_PER_GROUP = D_MODEL // HEAD_DIM
DILATED_PATTERNS = ((128, 1), (512, 4), (2048, 16))
N_ATT_GROUPS = len(DILATED_PATTERNS)
QBLK = 128
NEG_INF = -1e30
LOG2E = 1.4426950408889634
Q_SCALE = HEAD_DIM ** -0.5 * LOG2E
N_BUCKETS = 32
MAX_EXACT = N_BUCKETS // 2
MAX_DISTANCE = 2048
N_EXPERTS = 8
TOP_K = 2

LANES = 128
SUBLANES = 8
N_SLABS = D_MODEL // LANES
HEAD_PAIR = 2 * HEAD_DIM
assert HEAD_PAIR == LANES

LAYER0_ROWS = 512
QKV_ROWS = 512
ATTN_BLOCKS_PER_STEP = 16
MERGE_ROWS = 512
MOE_ROWS = 1024
MOE_FT = 1792
MOE_CHUNK = 256
MOE_SUBROWS = 256
MOE_NT = 1024
RUN_ALIGN = 16
STAGE_ROWS = -(-(TOP_K * MERGE_ROWS + N_EXPERTS * (RUN_ALIGN - 1)) // LANES) * LANES
VMEM_LIMIT = 60 * 1024 * 1024

R_E0, R_E1, R_RANK0, R_RANK1, R_G0, R_G1 = 0, 1, 2, 3, 4, 5
T_BASE, T_COUNT = 0, 1


def _rms_scale(x):
    return lax.rsqrt(jnp.mean(x * x, axis=-1, keepdims=True) + EPS)


def _silu(g):
    return g * (1.0 / (1.0 + jnp.exp(-g)))


def _split_bf16(x):
    hi = x.astype(jnp.bfloat16)
    lo = (x - hi.astype(jnp.float32)).astype(jnp.bfloat16)
    return hi, lo


def _pack_bf16_pair(a, b):
    hi = lax.bitcast_convert_type(a.astype(jnp.bfloat16).astype(jnp.float32), jnp.uint32)
    lo = lax.bitcast_convert_type(b.astype(jnp.bfloat16).astype(jnp.float32), jnp.uint32)
    return hi | (lo >> 16)


def _unpack_bf16_pair(w):
    return (lax.bitcast_convert_type(w & jnp.uint32(0xFFFF0000), jnp.float32),
            lax.bitcast_convert_type(w << 16, jnp.float32))


def _side_cast_specs(side, n_steps):
    rows = side.shape[0] // n_steps
    spec = pl.BlockSpec((rows, side.shape[1]), lambda i, *_: (i, 0))
    return spec, jax.ShapeDtypeStruct(side.shape, jnp.bfloat16)


def _pool_mix(x, halo, seq_tile, gain, proj_ref, scale):
    xn = x * _rms_scale(x) * gain
    hn_halo = halo * _rms_scale(halo) * gain
    hn_halo = jnp.where(seq_tile > 0, hn_halo, 0.0)
    full = jnp.concatenate([hn_halo, xn], axis=0)
    ts = x.shape[0]
    pos = seq_tile * ts + lax.broadcasted_iota(jnp.int32, (ts, 1), 0)
    outs = []
    for g, w in enumerate(POOL_WINDOWS):
        c0 = g * POOL_GROUP_DIM
        s = full[:, c0:c0 + POOL_GROUP_DIM]
        span = 1
        while span < w:
            s = s + pltpu.roll(s, span, axis=0)
            span *= 2
        s = s[MAX_POOL_WINDOW:, :]
        cnt = jnp.minimum(pos + 1, w).astype(jnp.float32)
        pooled = s / cnt - xn[:, c0:c0 + POOL_GROUP_DIM]
        outs.append(jnp.dot(pooled.astype(jnp.bfloat16), proj_ref[g],
                            preferred_element_type=jnp.float32))
    return x + jnp.concatenate(outs, axis=1) * scale


def _layer0_kernel(*refs, tiles_per_seq, n_side):
    (x_ref, halo_ref, an_ref, proj_ref, asc_ref, fn_ref, wg_ref, wu_ref, wd_ref) = refs[:9]
    side_in = refs[9:9 + n_side]
    h_out_ref = refs[9 + n_side]
    side_out = refs[10 + n_side:]
    for src, dst in zip(side_in, side_out):
        dst[...] = src[...].astype(dst.dtype)
    seq_tile = lax.rem(pl.program_id(0), tiles_per_seq)
    h = _pool_mix(x_ref[...], halo_ref[...], seq_tile, an_ref[...], proj_ref, asc_ref[...])
    hn = (h * _rms_scale(h) * fn_ref[...]).astype(jnp.bfloat16)
    g = jnp.dot(hn, wg_ref[...], preferred_element_type=jnp.float32)
    u = jnp.dot(hn, wu_ref[...], preferred_element_type=jnp.float32)
    hid = (_silu(g) * u).astype(jnp.bfloat16)
    h_out_ref[...] = h + jnp.dot(hid, wd_ref[...], preferred_element_type=jnp.float32)


def _layer0(x, seq_len, a_norm, a_proj, a_scale, ffn_norm, w_gate, w_up, w_down, sides):
    T, D = x.shape
    F = w_gate.shape[1]
    ts = LAYER0_ROWS
    n_steps = T // ts
    halo_blocks = ts // MAX_POOL_WINDOW
    row = lambda i: (i, 0)
    vec = pl.BlockSpec((1, D), lambda i: (0, 0))
    resident = dict(index_map=lambda i: (0, 0), pipeline_mode=pl.Buffered(1))
    side_specs, side_shapes = zip(*[_side_cast_specs(a, n_steps) for a in sides])
    outs = pl.pallas_call(
        functools.partial(_layer0_kernel, tiles_per_seq=seq_len // ts, n_side=len(sides)),
        grid=(n_steps,),
        in_specs=[
            pl.BlockSpec((ts, D), row),
            pl.BlockSpec((MAX_POOL_WINDOW, D), lambda i: (jnp.maximum(i * halo_blocks - 1, 0), 0)),
            vec,
            pl.BlockSpec(a_proj.shape, lambda i: (0, 0, 0), pipeline_mode=pl.Buffered(1)),
            vec, vec,
            pl.BlockSpec((D, F), **resident),
            pl.BlockSpec((D, F), **resident),
            pl.BlockSpec((F, D), **resident),
            *side_specs,
        ],
        out_specs=[pl.BlockSpec((ts, D), row), *side_specs],
        out_shape=[jax.ShapeDtypeStruct((T, D), jnp.float32), *side_shapes],
        compiler_params=pltpu.CompilerParams(
            dimension_semantics=("arbitrary",), vmem_limit_bytes=VMEM_LIMIT),
        name="pool_mixer_dense_swiglu",
    )(x, x, a_norm.reshape(1, D), a_proj.astype(jnp.bfloat16), a_scale.reshape(1, D),
      ffn_norm.reshape(1, D), w_gate.astype(jnp.bfloat16), w_up.astype(jnp.bfloat16),
      w_down.astype(jnp.bfloat16), *sides)
    return outs[0], outs[1:]


def _qkv_kernel(*refs):
    slabs = refs[:N_SLABS]
    kvg_ref, qg_ref, wq_ref, wkv_ref, side_in_ref = refs[N_SLABS:N_SLABS + 5]
    out_refs = refs[N_SLABS + 5:-2]
    side_out_ref, xs_scr = refs[-2:]
    side_out_ref[...] = side_in_ref[...].astype(side_out_ref.dtype)
    tm = slabs[0].shape[0]
    x_nat = jnp.concatenate([slab[...] for slab in slabs], axis=1)
    scale = _rms_scale(x_nat)
    for c, slab in enumerate(slabs):
        xs_scr[c] = slab[...] * scale
    for g, (_, dilation) in enumerate(DILATED_PATTERNS):
        n = tm // dilation
        cols = []
        for c in range(N_SLABS):
            if dilation == 1:
                cols.append(xs_scr[c])
            else:
                cols.append(jnp.concatenate(
                    [xs_scr[c, pl.ds(r, n, stride=dilation), :] for r in range(dilation)], axis=0))
        xs = jnp.concatenate(cols, axis=1)
        xq = (xs * qg_ref[...]).astype(jnp.bfloat16)
        xkv = (xs * kvg_ref[...]).astype(jnp.bfloat16)
        D = D_MODEL
        kcols = slice(g * D, (g + 1) * D)
        vcols = slice((N_ATT_GROUPS + g) * D, (N_ATT_GROUPS + g + 1) * D)
        q = jnp.dot(xq, wq_ref[:, kcols], preferred_element_type=jnp.float32) * Q_SCALE
        k = jnp.dot(xkv, wkv_ref[:, kcols], preferred_element_type=jnp.float32)
        v = jnp.dot(xkv, wkv_ref[:, vcols], preferred_element_type=jnp.float32)
        for out_ref, val in zip(out_refs[3 * g:3 * g + 3], (q, k, v)):
            for r in range(dilation):
                out_ref[r] = val[r * n:(r + 1) * n].astype(out_ref.dtype)


def _qkv_project(h, kv_norm, b_norm, w_q, w_kv, B, S, side):
    T, D = h.shape
    tm = QKV_ROWS
    tiles_per_seq = S // tm
    resident = dict(index_map=lambda i: (0, 0), pipeline_mode=pl.Buffered(1))
    slab_specs = [pl.BlockSpec((tm, LANES), functools.partial(lambda i, c: (i, c), c=c))
                  for c in range(N_SLABS)]
    vec = pl.BlockSpec((1, D), lambda i: (0, 0))
    out_specs, out_shapes = [], []
    for _, d in DILATED_PATTERNS:
        spec = pl.BlockSpec((None, d, tm // d, D),
                            lambda i: (i // tiles_per_seq, 0, i % tiles_per_seq, 0))
        out_specs += [spec] * 3
        out_shapes += [jax.ShapeDtypeStruct((B, d, S // d, D), jnp.bfloat16)] * 3
    side_spec, side_shape = _side_cast_specs(side, T // tm)
    outs = pl.pallas_call(
        _qkv_kernel,
        grid=(T // tm,),
        in_specs=slab_specs + [vec, vec, pl.BlockSpec(w_q.shape, **resident),
                               pl.BlockSpec(w_kv.shape, **resident), side_spec],
        out_specs=out_specs + [side_spec],
        out_shape=out_shapes + [side_shape],
        scratch_shapes=[pltpu.VMEM((N_SLABS, tm, LANES), jnp.float32)],
        compiler_params=pltpu.CompilerParams(
            dimension_semantics=("arbitrary",), vmem_limit_bytes=VMEM_LIMIT),
        name="qkv_proj",
    )(*([h] * N_SLABS), kv_norm.reshape(1, D), b_norm.reshape(1, D), w_q, w_kv, side)
    return [tuple(outs[3 * g:3 * g + 3]) for g in range(N_ATT_GROUPS)], outs[-1]


def _attn_kernel(*refs, dilation, has_prev, jb, rb):
    if has_prev:
        (table_ref, bmap_ref, q_ref, kp_ref, kc_ref, vp_ref, vc_ref,
         o_ref, stat_ref, bias_ref, stat_scr) = refs
    else:
        table_ref, bmap_ref, q_ref, kc_ref, vc_ref, o_ref, stat_ref, bias_ref, stat_scr = refs
    b, jstep, rstep = pl.program_id(0), pl.program_id(1), pl.program_id(2)
    n_keys = bias_ref.shape[2]

    @pl.when(jnp.logical_and(jnp.logical_and(b == 0, jstep == 0), rstep == 0))
    def _():
        stat_scr[...] = jnp.zeros_like(stat_scr)
        bmap = bmap_ref[...]
        in_prev = lax.broadcasted_iota(jnp.int32, bmap.shape, 1) < n_keys - QBLK

        def head(h, carry):
            acc = jnp.full(bmap.shape, NEG_INF, jnp.float32)
            for bucket in range(N_BUCKETS):
                acc = jnp.where(bmap == bucket, table_ref[bucket, h], acc)
            rows_h = pl.ds(pl.multiple_of(h * QBLK, QBLK), QBLK)
            bias_ref[0, rows_h, :] = jnp.where(in_prev, NEG_INF, acc)
            bias_ref[1, rows_h, :] = acc
            return carry

        lax.fori_loop(0, HEADS_PER_GROUP, head, 0)

    lane = lax.broadcasted_iota(jnp.int32, (QBLK, LANES), 1)
    first_head = lane < HEAD_DIM
    for rr in range(rb):
        for jj in range(jb):
            qrows = slice(jj * QBLK, (jj + 1) * QBLK)
            bias_copy = jnp.minimum(jstep, 1) if jj == 0 else 1
            if dilation > 1:
                out_rows = pl.ds(dilation * QBLK * jj + rstep * rb + rr, QBLK, stride=dilation)
            else:
                out_rows = qrows
            for hp in range(HEADS_PER_GROUP // 2):
                sl = slice(hp * HEAD_PAIR, (hp + 1) * HEAD_PAIR)
                qp = q_ref[rr, qrows, sl]
                zero = jnp.zeros_like(qp)
                q2 = jnp.concatenate([jnp.where(first_head, qp, zero),
                                      jnp.where(first_head, zero, qp)], axis=0)
                if has_prev:
                    if jj == 0:
                        k_prev, v_prev = kp_ref[rr, :, sl], vp_ref[rr, :, sl]
                    else:
                        prows = slice((jj - 1) * QBLK, jj * QBLK)
                        k_prev, v_prev = kc_ref[rr, prows, sl], vc_ref[rr, prows, sl]
                    kcat = jnp.concatenate([k_prev, kc_ref[rr, qrows, sl]], axis=0)
                    vcat = jnp.concatenate([v_prev, vc_ref[rr, qrows, sl]], axis=0)
                else:
                    kcat, vcat = kc_ref[rr, qrows, sl], vc_ref[rr, qrows, sl]
                s = lax.dot_general(q2, kcat, (((1,), (1,)), ((), ())),
                                    preferred_element_type=jnp.float32)
                s = s + bias_ref[bias_copy, hp * 2 * QBLK:(hp + 1) * 2 * QBLK, :]
                m = jnp.max(s, axis=1, keepdims=True)
                p = jnp.exp2(s - m)
                l = jnp.sum(p, axis=1, keepdims=True)
                o2 = jnp.dot(p.astype(jnp.bfloat16), vcat, preferred_element_type=jnp.float32)
                o_pair = jnp.where(first_head, o2[:QBLK], o2[QBLK:])
                if hp % 2 == 0:
                    o_even = o_pair
                else:
                    o_ref[hp // 2, out_rows, :] = _pack_bf16_pair(o_even, o_pair)
                for half, head in ((slice(0, QBLK), 2 * hp), (slice(QBLK, 2 * QBLK), 2 * hp + 1)):
                    stat_scr[:, head:head + 1] = m[half]
                    stat_scr[:, HEADS_PER_GROUP + head:HEADS_PER_GROUP + head + 1] = l[half]
            stat_ref[out_rows, :] = stat_scr[...]


def _bucket_map(dilation, window, n_keys):
    W = window // dilation
    a = jnp.arange(QBLK, dtype=jnp.int32)[:, None]
    c = jnp.arange(n_keys, dtype=jnp.int32)[None, :]
    m = a + (n_keys - QBLK) - c
    band = (m >= 0) & (m <= W)
    n = jnp.maximum(m, 0) * dilation
    nf = jnp.maximum(n, 1).astype(jnp.float32)
    large = MAX_EXACT + (jnp.log(nf / MAX_EXACT) / math.log(MAX_DISTANCE / MAX_EXACT)
                         * (N_BUCKETS - MAX_EXACT)).astype(jnp.int32)
    large = jnp.minimum(large, N_BUCKETS - 1)
    bucket = jnp.where(n < MAX_EXACT, n, large)
    return jnp.where(band, bucket, -1)


def _attention_group(q, k, v, rel_bias, g, B, S):
    window, d = DILATED_PATTERNS[g]
    L = S // d
    n = L // QBLK
    D = D_MODEL
    has_prev = n > 1
    n_keys = 2 * QBLK if has_prev else QBLK
    table = rel_bias[:, g * HEADS_PER_GROUP:(g + 1) * HEADS_PER_GROUP].astype(jnp.float32) * LOG2E
    bmap = _bucket_map(d, window, n_keys)
    jb = min(ATTN_BLOCKS_PER_STEP, n)
    rb = ATTN_BLOCKS_PER_STEP // jb
    cur_blk = (None, rb, jb * QBLK, D)
    prev_blk = (None, rb, QBLK, D)
    cur = lambda b, j, r: (b, r, j, 0)
    prev = lambda b, j, r: (b, r, jnp.maximum(j * jb - 1, 0), 0)
    if has_prev:
        kv_specs = [pl.BlockSpec(prev_blk, prev), pl.BlockSpec(cur_blk, cur),
                    pl.BlockSpec(prev_blk, prev), pl.BlockSpec(cur_blk, cur)]
        kv_args = (k, k, v, v)
    else:
        kv_specs = [pl.BlockSpec(cur_blk, cur), pl.BlockSpec(cur_blk, cur)]
        kv_args = (k, v)
    n_j = n // jb
    span = QBLK * jb * d
    return pl.pallas_call(
        functools.partial(_attn_kernel, dilation=d, has_prev=has_prev, jb=jb, rb=rb),
        grid=(B, n_j, d // rb),
        in_specs=[pl.BlockSpec(memory_space=pltpu.SMEM),
                  pl.BlockSpec(bmap.shape, lambda b, j, r: (0, 0)),
                  pl.BlockSpec(cur_blk, cur)] + kv_specs,
        out_specs=[pl.BlockSpec((N_SLABS // 2, span, LANES), lambda b, j, r: (0, b * n_j + j, 0)),
                   pl.BlockSpec((span, LANES), lambda b, j, r: (b * n_j + j, 0))],
        out_shape=[jax.ShapeDtypeStruct((N_SLABS // 2, B * S, LANES), jnp.uint32),
                   jax.ShapeDtypeStruct((B * S, LANES), jnp.float32)],
        scratch_shapes=[pltpu.VMEM((2, HEADS_PER_GROUP * QBLK, n_keys), jnp.float32),
                        pltpu.VMEM((QBLK, LANES), jnp.float32)],
        compiler_params=pltpu.CompilerParams(
            dimension_semantics=("arbitrary",) * 3, vmem_limit_bytes=VMEM_LIMIT),
        name=f"dilated_attn_g{g}",
    )(table, bmap, q, *kv_args)


def _merge_kernel(o0_ref, o1_ref, o2_ref, s0_ref, s1_ref, s2_ref, h_ref, wo_ref, fn_ref,
                  r2_ref, expand_ref,
                  h_out_ref, hn_ref, route_ref, route_t_ref, tinfo_ref, counts_ref, carry_ref):
    i = pl.program_id(0)
    tm = h_ref.shape[0]

    @pl.when(i == 0)
    def _():
        carry_ref[...] = jnp.zeros_like(carry_ref)

    lane = lax.broadcasted_iota(jnp.int32, (tm, LANES), 1)
    head_lane = lane < HEADS_PER_GROUP
    stats = [s0_ref[...], s1_ref[...], s2_ref[...]]
    dens = [pltpu.roll(st, LANES - HEADS_PER_GROUP, axis=1) for st in stats]
    mx = jnp.maximum(jnp.maximum(stats[0], stats[1]), stats[2])
    es = [jnp.exp2(st - mx) for st in stats]
    inv = 1.0 / (es[0] * dens[0] + es[1] * dens[1] + es[2] * dens[2])
    merged = None
    for e, o_ref in zip(es, (o0_ref, o1_ref, o2_ref)):
        w = jnp.where(head_lane, e * inv, 0.0)
        hi = w.astype(jnp.bfloat16).astype(jnp.float32)
        packed = (hi + pltpu.roll(w - hi, HEADS_PER_GROUP, axis=1)).astype(jnp.bfloat16)
        wide = jnp.dot(packed, expand_ref[...], preferred_element_type=jnp.float32)
        o = jnp.concatenate([part for c in range(N_SLABS // 2)
                             for part in _unpack_bf16_pair(o_ref[c])], axis=1)
        term = wide * o
        merged = term if merged is None else merged + term
    h = h_ref[...] + jnp.dot(merged.astype(jnp.bfloat16), wo_ref[...],
                             preferred_element_type=jnp.float32)
    h_out_ref[...] = h
    hn = h * _rms_scale(h) * fn_ref[...]
    hn_ref[...] = hn.astype(hn_ref.dtype)

    hi, lo = _split_bf16(hn)
    both = jnp.dot(hi, r2_ref[...], preferred_element_type=jnp.float32)
    logits = (both[:, :LANES] + both[:, LANES:]
              + jnp.dot(lo, r2_ref[:, :LANES], preferred_element_type=jnp.float32))
    logits = jnp.where(lane < N_EXPERTS, logits, -jnp.inf)
    v0 = jnp.max(logits, axis=1, keepdims=True)
    e0 = jnp.min(jnp.where(logits == v0, lane, LANES), axis=1, keepdims=True)
    rest = jnp.where(lane == e0, -jnp.inf, logits)
    v1 = jnp.max(rest, axis=1, keepdims=True)
    e1 = jnp.min(jnp.where(rest == v1, lane, LANES), axis=1, keepdims=True)
    t = jnp.exp(v1 - v0)
    g0 = 1.0 / (1.0 + t)
    g1 = t / (1.0 + t)

    hit0 = lane == e0
    hit1 = lane == e1
    onehot = jnp.where(jnp.logical_or(hit0, hit1), 1.0, 0.0)
    r_i = lax.broadcasted_iota(jnp.int32, (tm, tm), 0)
    c_i = lax.broadcasted_iota(jnp.int32, (tm, tm), 1)
    lower = jnp.where(c_i < r_i, 1.0, 0.0).astype(jnp.bfloat16)
    before = jnp.dot(lower, onehot.astype(jnp.bfloat16), preferred_element_type=jnp.float32)
    rank0 = jnp.sum(jnp.where(hit0, before, 0.0), axis=1, keepdims=True)
    rank1 = jnp.sum(jnp.where(hit1, before, 0.0), axis=1, keepdims=True)
    tile_counts = jnp.sum(onehot, axis=0, keepdims=True)
    base = carry_ref[0:1, :]
    total = base + jnp.floor((tile_counts + (RUN_ALIGN - 1)) * (1.0 / RUN_ALIGN)) * RUN_ALIGN
    carry_ref[...] = jnp.broadcast_to(total, carry_ref.shape)
    counts_ref[...] = jnp.broadcast_to(total, counts_ref.shape)
    row_id = lax.broadcasted_iota(jnp.int32, tinfo_ref.shape, 0)
    tinfo_ref[...] = jnp.where(row_id == T_BASE, base, jnp.where(row_id == T_COUNT, tile_counts, 0.0))

    route = jnp.zeros((tm, LANES), jnp.float32)
    for ln, val in ((R_E0, e0.astype(jnp.float32)), (R_E1, e1.astype(jnp.float32)),
                    (R_RANK0, rank0), (R_RANK1, rank1), (R_G0, g0), (R_G1, g1)):
        route = jnp.where(lane == ln, val, route)
    route_ref[...] = route
    route_t_ref[...] = route.T[:SUBLANES, :]


def _merge_layer(outs, lses, h, w_o, ffn_norm, w_router):
    T, D = h.shape
    tm = MERGE_ROWS
    r_pad = jnp.zeros((D, LANES), jnp.float32).at[:, :N_EXPERTS].set(w_router)
    r2 = jnp.concatenate(_split_bf16(r_pad), axis=1)
    head_of_col = jnp.arange(D, dtype=jnp.int32) // HEAD_DIM
    lane_id = jnp.arange(LANES, dtype=jnp.int32)[:, None]
    expand = ((lane_id < 2 * HEADS_PER_GROUP)
              & (lane_id % HEADS_PER_GROUP == head_of_col[None, :])).astype(jnp.bfloat16)
    row = lambda i: (i, 0)
    const = lambda i: (0, 0)
    wide = pl.BlockSpec((tm, D), row)
    narrow = pl.BlockSpec((tm, LANES), row)
    slabs = pl.BlockSpec((N_SLABS // 2, tm, LANES), lambda i: (0, i, 0))
    return pl.pallas_call(
        _merge_kernel,
        grid=(T // tm,),
        in_specs=[slabs, slabs, slabs, narrow, narrow, narrow, wide,
                  pl.BlockSpec((D, D), const), pl.BlockSpec((1, D), const),
                  pl.BlockSpec((D, 2 * LANES), const), pl.BlockSpec((LANES, D), const)],
        out_specs=[wide, wide, narrow,
                   pl.BlockSpec((None, SUBLANES, tm), lambda i: (i, 0, 0)),
                   pl.BlockSpec((None, SUBLANES, LANES), lambda i: (i, 0, 0)),
                   pl.BlockSpec((8, LANES), const)],
        out_shape=[jax.ShapeDtypeStruct((T, D), jnp.float32),
                   jax.ShapeDtypeStruct((T, D), jnp.bfloat16),
                   jax.ShapeDtypeStruct((T, LANES), jnp.float32),
                   jax.ShapeDtypeStruct((T // tm, SUBLANES, tm), jnp.float32),
                   jax.ShapeDtypeStruct((T // tm, SUBLANES, LANES), jnp.float32),
                   jax.ShapeDtypeStruct((8, LANES), jnp.float32)],
        scratch_shapes=[pltpu.VMEM((8, LANES), jnp.float32)],
        compiler_params=pltpu.CompilerParams(
            dimension_semantics=("arbitrary",), vmem_limit_bytes=VMEM_LIMIT),
        name="merge_outproj_router",
    )(*outs, *lses, h, w_o, ffn_norm.reshape(1, D), r2, expand)


def _run_offsets(cnt_ref, tile):
    sizes, offs, off = [], [], 0
    for e in range(N_EXPERTS):
        size = ((cnt_ref[tile * N_EXPERTS + e] + (RUN_ALIGN - 1)) // RUN_ALIGN) * RUN_ALIGN
        sizes.append(size)
        offs.append(off)
        off = off + size
    return sizes, offs


def _for_each_run_piece(start_ref, cnt_ref, tile, fn):
    sizes, offs = _run_offsets(cnt_ref, tile)
    for e in range(N_EXPERTS):
        slot0 = start_ref[tile * N_EXPERTS + e]
        bit = MERGE_ROWS
        while bit >= RUN_ALIGN:
            done = sizes[e] & -(2 * bit)

            @pl.when((sizes[e] & bit) != 0)
            def _():
                fn(pl.multiple_of(offs[e] + done, RUN_ALIGN),
                   pl.multiple_of(slot0 + done, RUN_ALIGN), bit)
            bit //= 2


def _staging_columns(expert, rank, offs):
    col = rank
    for e in range(N_EXPERTS):
        col = col + jnp.where(expert == e, offs[e], 0)
    return col


def _dispatch_kernel(start_ref, cnt_ref, pad_start_ref, pad_len_ref, used_ref,
                     hn_ref, route_t_ref, side_in_ref, xs_ref, side_out_ref,
                     xbuf, zbuf, sems, zsem):
    side_out_ref[...] = side_in_ref[...].astype(side_out_ref.dtype)
    i = pl.program_id(0)
    n_steps = pl.num_programs(0)
    slot = lax.rem(i, 2)
    n_tiles = xs_ref.shape[0] // MOE_ROWS

    def zero_fill(act):
        for e in range(N_EXPERTS):
            run_start = pl.multiple_of(pad_start_ref[e], RUN_ALIGN)
            run_len = pad_len_ref[e]
            bit = MOE_ROWS // 2
            while bit >= RUN_ALIGN:
                off = pl.multiple_of(run_start + (run_len & -(2 * bit)), RUN_ALIGN)

                @pl.when((run_len & bit) != 0)
                def _():
                    act(pltpu.make_async_copy(zbuf.at[pl.ds(0, bit), :],
                                              xs_ref.at[pl.ds(off, bit), :], zsem))
                bit //= 2
        for t in range(n_tiles):
            @pl.when(t >= used_ref[0])
            def _():
                act(pltpu.make_async_copy(zbuf, xs_ref.at[pl.ds(t * MOE_ROWS, MOE_ROWS), :], zsem))

    def run_copies(tile, buf_slot, act):
        def piece(buf_row, slot_row, rows):
            act(pltpu.make_async_copy(xbuf.at[buf_slot, pl.ds(buf_row, rows), :],
                                      xs_ref.at[pl.ds(slot_row, rows), :], sems.at[buf_slot]))
        _for_each_run_piece(start_ref, cnt_ref, tile, piece)

    @pl.when(i == 0)
    def _():
        zbuf[...] = jnp.zeros_like(zbuf)
        zero_fill(lambda c: c.start())

    @pl.when(i >= 2)
    def _():
        run_copies(i - 2, slot, lambda c: c.wait())

    _, offs = _run_offsets(cnt_ref, i)
    route_t = route_t_ref[...]
    cols = [_staging_columns(route_t[R_E0 + k:R_E0 + k + 1, :].astype(jnp.int32),
                             route_t[R_RANK0 + k:R_RANK0 + k + 1, :].astype(jnp.int32), offs)
            for k in range(TOP_K)]
    buf_row = lax.broadcasted_iota(jnp.int32, (STAGE_ROWS, hn_ref.shape[0]), 0)
    onehot = jnp.where(jnp.logical_or(buf_row == cols[0], buf_row == cols[1]), 1.0, 0.0)
    xbuf[slot] = jnp.dot(onehot.astype(jnp.bfloat16), hn_ref[...],
                         preferred_element_type=jnp.float32).astype(xbuf.dtype)
    run_copies(i, slot, lambda c: c.start())

    @pl.when(i == n_steps - 1)
    def _():
        @pl.when(i >= 1)
        def _():
            run_copies(i - 1, 1 - slot, lambda c: c.wait())
        run_copies(i, slot, lambda c: c.wait())
        zero_fill(lambda c: c.wait())


def _dispatch(run_start, run_cnt, pad_start, pad_len, used, hn, route_t, n_slots, side):
    T, D = hn.shape
    tm = MERGE_ROWS
    side_spec, side_shape = _side_cast_specs(side, T // tm)
    return pl.pallas_call(
        _dispatch_kernel,
        grid_spec=pltpu.PrefetchScalarGridSpec(
            num_scalar_prefetch=5,
            grid=(T // tm,),
            in_specs=[pl.BlockSpec((tm, D), lambda i, *_: (i, 0)),
                      pl.BlockSpec((None, SUBLANES, tm), lambda i, *_: (i, 0, 0)),
                      side_spec],
            out_specs=[pl.BlockSpec(memory_space=pl.ANY), side_spec],
            scratch_shapes=[pltpu.VMEM((2, STAGE_ROWS, D), jnp.bfloat16),
                            pltpu.VMEM((MOE_ROWS, D), jnp.bfloat16),
                            pltpu.SemaphoreType.DMA((2,)), pltpu.SemaphoreType.DMA(())],
        ),
        out_shape=[jax.ShapeDtypeStruct((n_slots, D), jnp.bfloat16), side_shape],
        compiler_params=pltpu.CompilerParams(
            dimension_semantics=("arbitrary",), vmem_limit_bytes=VMEM_LIMIT),
        name="moe_dispatch",
    )(run_start, run_cnt, pad_start, pad_len, used, hn, route_t, side)


def _expert_kernel(tile_e_ref, used_ref, valid_ref, x_ref, wg_ref, wu_ref, wd_ref, y_ref,
                   hid_ref):
    del tile_e_ref
    i = pl.program_id(0)
    s = pl.program_id(1)
    n_f, tm, tf = hid_ref.shape
    active = i < used_ref[0]
    n_valid = valid_ref[jnp.minimum(i, used_ref[0] - 1)]
    full = n_valid == tm
    gate_step = jnp.logical_and(active, s < n_f)
    down_step = jnp.logical_and(active, s >= n_f)

    def gate_up(rows):
        x = x_ref[rows, :]
        for c0 in range(0, tf, MOE_CHUNK):
            cols = slice(c0, min(c0 + MOE_CHUNK, tf))
            g = jnp.dot(x, wg_ref[:, cols], preferred_element_type=jnp.float32)
            u = jnp.dot(x, wu_ref[:, cols], preferred_element_type=jnp.float32)
            hid_ref[s, rows, cols] = (_silu(g) * u).astype(hid_ref.dtype)

    def down(rows):
        hid = jnp.concatenate([hid_ref[c, rows, :] for c in range(n_f)], axis=1)
        y_ref[rows, :] = jnp.dot(hid, wd_ref[...],
                                 preferred_element_type=jnp.float32).astype(y_ref.dtype)

    @pl.when(jnp.logical_and(gate_step, full))
    def _():
        gate_up(slice(None))

    @pl.when(jnp.logical_and(down_step, full))
    def _():
        down(slice(None))

    for r0 in range(0, tm, MOE_SUBROWS):
        rows = slice(r0, r0 + MOE_SUBROWS)
        occupied = jnp.logical_and(jnp.logical_not(full), r0 < n_valid)

        @pl.when(jnp.logical_and(gate_step, occupied))
        def _():
            gate_up(rows)

        @pl.when(jnp.logical_and(down_step, occupied))
        def _():
            down(rows)

        @pl.when(jnp.logical_and(down_step, r0 >= n_valid))
        def _():
            y_ref[rows, :] = jnp.zeros((MOE_SUBROWS, y_ref.shape[1]), y_ref.dtype)

    @pl.when(jnp.logical_and(jnp.logical_not(active), s >= n_f))
    def _():
        y_ref[...] = jnp.zeros_like(y_ref)


def _expert_ffn(tile_e, used, tile_valid, xs, w_gate, w_up, w_down):
    n_slots, D = xs.shape
    F = w_gate.shape[2]
    tm, tf, tn = MOE_ROWS, MOE_FT, MOE_NT
    n_f, n_n = F // tf, D // tn
    n_tiles = n_slots // tm
    last = n_f + n_n - 1

    def eff(i, s, used_ref):
        idle = i >= used_ref[0]
        return jnp.where(idle, used_ref[0] - 1, i), jnp.where(idle, last, s)

    def x_map(i, s, te, us, nv):
        ie, se = eff(i, s, us)
        return jnp.minimum(ie + jnp.where(se >= n_f, 1, 0), us[0] - 1), 0

    def gate_map(i, s, te, us, nv):
        ie, se = eff(i, s, us)
        return te[ie], 0, jnp.minimum(se, n_f - 1)

    def down_map(i, s, te, us, nv):
        ie, se = eff(i, s, us)
        tile = jnp.where(se == 0, jnp.maximum(ie - 1, 0), ie)
        return te[tile], 0, jnp.where(se == 0, n_n - 1, jnp.clip(se - n_f, 0, n_n - 1))

    def out_map(i, s, te, us, nv):
        return i, jnp.clip(s - n_f, 0, n_n - 1)

    return pl.pallas_call(
        _expert_kernel,
        grid_spec=pltpu.PrefetchScalarGridSpec(
            num_scalar_prefetch=3,
            grid=(n_tiles, n_f + n_n),
            in_specs=[
                pl.BlockSpec((tm, D), x_map),
                pl.BlockSpec((None, D, tf), gate_map),
                pl.BlockSpec((None, D, tf), gate_map),
                pl.BlockSpec((None, F, tn), down_map),
            ],
            out_specs=pl.BlockSpec((tm, tn), out_map),
            scratch_shapes=[pltpu.VMEM((n_f, tm, tf), jnp.bfloat16)],
        ),
        out_shape=jax.ShapeDtypeStruct((n_slots, D), jnp.bfloat16),
        compiler_params=pltpu.CompilerParams(
            dimension_semantics=("arbitrary", "arbitrary"), vmem_limit_bytes=VMEM_LIMIT),
        name="moe_experts",
    )(tile_e, used, tile_valid, xs, w_gate, w_up, w_down)


def _combine_kernel(start_ref, cnt_ref, h_ref, route_ref, fin_ref, ys_ref, o_ref, ybuf, sems):
    i = pl.program_id(0)
    n_steps = pl.num_programs(0)
    tile = h_ref.shape[0]
    slot = lax.rem(i, 2)

    def run_copies(step, buf_slot, act):
        def piece(buf_row, slot_row, rows):
            act(pltpu.make_async_copy(ys_ref.at[pl.ds(slot_row, rows), :],
                                      ybuf.at[buf_slot, pl.ds(buf_row, rows), :],
                                      sems.at[buf_slot]))
        _for_each_run_piece(start_ref, cnt_ref, step, piece)

    @pl.when(i == 0)
    def _():
        ybuf[...] = jnp.zeros_like(ybuf)
        run_copies(0, 0, lambda c: c.start())

    @pl.when(i + 1 < n_steps)
    def _():
        run_copies(i + 1, 1 - slot, lambda c: c.start())

    run_copies(i, slot, lambda c: c.wait())
    _, offs = _run_offsets(cnt_ref, i)
    route = route_ref[...]
    staged = ybuf[slot]
    buf_col = lax.broadcasted_iota(jnp.int32, (tile, STAGE_ROWS), 1)
    ys = []
    for k in range(TOP_K):
        col = _staging_columns(route[:, R_E0 + k:R_E0 + k + 1].astype(jnp.int32),
                               route[:, R_RANK0 + k:R_RANK0 + k + 1].astype(jnp.int32), offs)
        onehot = jnp.where(buf_col == col, 1.0, 0.0).astype(jnp.bfloat16)
        ys.append(jnp.dot(onehot, staged, preferred_element_type=jnp.float32))
    g0 = route[:, R_G0:R_G0 + 1]
    g1 = route[:, R_G1:R_G1 + 1]
    h = h_ref[...] + (g0 * ys[0] + g1 * ys[1])
    o_ref[...] = h * _rms_scale(h) * fin_ref[...]


def _combine(run_start, run_cnt, h, route, final_norm, ys):
    T, D = h.shape
    tc = MERGE_ROWS
    return pl.pallas_call(
        _combine_kernel,
        grid_spec=pltpu.PrefetchScalarGridSpec(
            num_scalar_prefetch=2,
            grid=(T // tc,),
            in_specs=[pl.BlockSpec((tc, D), lambda i, *_: (i, 0)),
                      pl.BlockSpec((tc, LANES), lambda i, *_: (i, 0)),
                      pl.BlockSpec((1, D), lambda i, *_: (0, 0)),
                      pl.BlockSpec(memory_space=pl.ANY)],
            out_specs=pl.BlockSpec((tc, D), lambda i, *_: (i, 0)),
            scratch_shapes=[pltpu.VMEM((2, STAGE_ROWS, D), jnp.bfloat16),
                            pltpu.SemaphoreType.DMA((2,))],
        ),
        out_shape=jax.ShapeDtypeStruct((T, D), jnp.float32),
        compiler_params=pltpu.CompilerParams(
            dimension_semantics=("arbitrary",), vmem_limit_bytes=VMEM_LIMIT),
        name="moe_combine_final_norm",
    )(run_start, run_cnt, h, route, final_norm.reshape(1, D), ys)


def _routing_tables(tinfo, counts_row, n_tokens):
    counts = counts_row[0, :N_EXPERTS].astype(jnp.int32)
    padded = ((counts + MOE_ROWS - 1) // MOE_ROWS) * MOE_ROWS
    ends = jnp.cumsum(padded)
    starts = ends - padded
    run_start = starts[None, :] + tinfo[:, T_BASE, :N_EXPERTS].astype(jnp.int32)
    run_cnt = tinfo[:, T_COUNT, :N_EXPERTS].astype(jnp.int32)
    run_padding = (n_tokens // MERGE_ROWS) * N_EXPERTS * (RUN_ALIGN - 1)
    n_tiles = (TOP_K * n_tokens + run_padding + MOE_ROWS - 1) // MOE_ROWS + N_EXPERTS
    tile_start = jnp.arange(n_tiles, dtype=jnp.int32) * MOE_ROWS
    tile_e = jnp.minimum(jnp.sum(ends[None, :] <= tile_start[:, None], axis=1),
                         N_EXPERTS - 1).astype(jnp.int32)
    used = (ends[-1:] // MOE_ROWS).astype(jnp.int32)
    tile_valid = jnp.clip(counts[tile_e] - (tile_start - starts[tile_e]), 0, MOE_ROWS).astype(jnp.int32)
    pad_start = (starts + counts).astype(jnp.int32)
    pad_len = (padded - counts).astype(jnp.int32)
    return (run_start.reshape(-1), run_cnt.reshape(-1), tile_e, used, tile_valid, pad_start,
            pad_len, n_tiles * MOE_ROWS)


def kernel(x, a_norm, a_proj, a_scale, kv_norm, w_kv, b_norm, w_q, w_o, rel_bias, ffn_norm,
           dense_w_gate, dense_w_up, dense_w_down, moe_router, moe_w_gate, moe_w_up,
           moe_w_down, final_norm):
    B, S, D = x.shape
    T = B * S
    E, _, F = moe_w_gate[0].shape
    h2, (wg16, wq16, wkv16, wo16) = _layer0(
        x.reshape(T, D), S, a_norm[0], a_proj[0], a_scale[0], ffn_norm[0],
        dense_w_gate[0], dense_w_up[0], dense_w_down[0],
        (moe_w_gate[0].reshape(E * D, F), w_q[0], w_kv, w_o[0]))
    qkv, wu16 = _qkv_project(h2, kv_norm, b_norm[0], wq16, wkv16, B, S,
                             moe_w_up[0].reshape(E * D, F))
    outs, stats = [], []
    for g, (q, k, v) in enumerate(qkv):
        o, st = _attention_group(q, k, v, rel_bias, g, B, S)
        outs.append(o)
        stats.append(st)
    h3, hn3, route, route_t, tinfo, counts = _merge_layer(outs, stats, h2, wo16, ffn_norm[1],
                                                          moe_router[0])
    (run_start, run_cnt, tile_e, used, tile_valid, pad_start, pad_len,
     n_slots) = _routing_tables(tinfo, counts, T)
    xs, wd16 = _dispatch(run_start, run_cnt, pad_start, pad_len, used, hn3, route_t, n_slots,
                         moe_w_down[0].reshape(E * F, D))
    ys = _expert_ffn(tile_e, used, tile_valid, xs, wg16.reshape(E, D, F), wu16.reshape(E, D, F),
                     wd16.reshape(E, F, D))
    out = _combine(run_start, run_cnt, h3, route, final_norm, ys)
    return out.reshape(B, S, D)
```
